```python
import math
import jax, jax.numpy as jnp
from jax import lax
import numpy as np

D_MODEL = 2048
BATCH = 4
SEQ = 2048
DEPTH = 1
DEC_BATCH = 128
DEC_SEQ = 4
PAST_LEN = 16384
PAGE_SIZE = 128

MIX_WIDTH = D_MODEL
S5_WIDTH = MIX_WIDTH // 2
S5_GROUP = 16
S5_GROUPS = S5_WIDTH // S5_GROUP
S5_STATE = 64
GLA_WIDTH = MIX_WIDTH - S5_WIDTH
GLA_HEADS = 4
GLA_DV = GLA_WIDTH // GLA_HEADS
GLA_DK = GLA_DV // 2
GLA_QK_WIDTH = GLA_HEADS * GLA_DK
GLA_RANK = 16
GLA_TAU = 16.0
GLA_CHUNK = 64
N_MEM = 256
X_HEADS = 4
X_HEAD_DIM = D_MODEL // X_HEADS
D_FF = 5632
CONV_W = 3
EPS = 1e-6
IN_WIDTH = S5_WIDTH + 2 * GLA_QK_WIDTH + GLA_WIDTH + GLA_RANK + GLA_WIDTH
SPLITS = [S5_WIDTH,
          S5_WIDTH + GLA_QK_WIDTH,
          S5_WIDTH + 2 * GLA_QK_WIDTH,
          S5_WIDTH + 2 * GLA_QK_WIDTH + GLA_WIDTH,
          S5_WIDTH + 2 * GLA_QK_WIDTH + GLA_WIDTH + GLA_RANK]

kernel_name = "hymba_s5_gla_memxattn_convffn_step"


def _rmsnorm(x, gain):
    xf = x.astype(jnp.float32)
    r = lax.rsqrt(jnp.mean(xf * xf, axis=-1, keepdims=True) + EPS)
    return (xf * r).astype(x.dtype) * gain.astype(x.dtype)


def _complex_affine_combine(e1, e2):
    a1r, a1i, b1r, b1i = e1
    a2r, a2i, b2r, b2i = e2
    ar = a2r * a1r - a2i * a1i
    ai = a2r * a1i + a2i * a1r
    br = a2r * b1r - a2i * b1i + b2r
    bi = a2r * b1i + a2i * b1r + b2i
    return (ar, ai, br, bi)


def _s5_mixer(u, h0_re, h0_im, lam_re, lam_im, log_dt, b_re, b_im, c_re, c_im, d, w_glu, b_glu):
    bsz, L, _ = u.shape
    f32 = jnp.float32
    uf = u.astype(f32).reshape(bsz, L, S5_GROUPS, S5_GROUP)
    dt = jnp.exp(log_dt.astype(f32))[:, None]
    lr = lam_re.astype(f32)
    li = lam_im.astype(f32)
    mag = jnp.exp(lr * dt)
    a_re = mag * jnp.cos(li * dt)
    a_im = mag * jnp.sin(li * dt)
    den = lr * lr + li * li
    p_re = a_re - 1.0
    q_re = (p_re * lr + a_im * li) / den
    q_im = (a_im * lr - p_re * li) / den
    br = b_re.astype(f32)
    bi = b_im.astype(f32)
    bb_re = q_re[..., None] * br - q_im[..., None] * bi
    bb_im = q_re[..., None] * bi + q_im[..., None] * br
    bu_re = jnp.einsum('blgp,gnp->blgn', uf, bb_re)
    bu_im = jnp.einsum('blgp,gnp->blgn', uf, bb_im)
    h0r = h0_re.astype(f32)
    h0i = h0_im.astype(f32)
    bu_re = bu_re.at[:, 0].add(a_re * h0r - a_im * h0i)
    bu_im = bu_im.at[:, 0].add(a_re * h0i + a_im * h0r)
    a_re_l = jnp.broadcast_to(a_re, bu_re.shape)
    a_im_l = jnp.broadcast_to(a_im, bu_im.shape)
    _, _, h_re, h_im = lax.associative_scan(
        _complex_affine_combine, (a_re_l, a_im_l, bu_re, bu_im), axis=1)
    y = (jnp.einsum('blgn,gpn->blgp', h_re, c_re.astype(f32))
         - jnp.einsum('blgn,gpn->blgp', h_im, c_im.astype(f32))
         + d.astype(f32) * uf)
    z = jax.nn.gelu(y, approximate=False)
    z = z * jax.nn.sigmoid(jnp.einsum('blgp,gpq->blgq', z, w_glu.astype(f32)) + b_glu.astype(f32))
    return z.reshape(bsz, L, S5_WIDTH), h_re[:, -1], h_im[:, -1]


def _gla_chunked(q, k, v, log_a, s0):
    bsz, L = q.shape[0], q.shape[1]
    c = min(GLA_CHUNK, L)
    n = L // c

    def to_chunks(t):
        return t.reshape(bsz, n, c, GLA_HEADS, t.shape[-1]).transpose(1, 0, 3, 2, 4)

    qc, kc, vc, gc = to_chunks(q), to_chunks(k), to_chunks(v), to_chunks(log_a)
    bcum = jnp.cumsum(gc, axis=3)
    b_last = bcum[..., -1:, :]
    q_t = qc * jnp.exp(bcum)
    k_t = kc * jnp.exp(-bcum)
    k_end = kc * jnp.exp(b_last - bcum)
    causal = jnp.tril(jnp.ones((c, c), dtype=bool))
    scores = jnp.where(causal, jnp.einsum('nbhtk,nbhsk->nbhts', q_t, k_t), 0.0)
    o_intra = jnp.einsum('nbhts,nbhsv->nbhtv', scores, vc)
    kv_chunk = jnp.einsum('nbhsk,nbhsv->nbhkv', k_end, vc)
    decay = jnp.exp(b_last[..., 0, :])

    def step(s, inp):
        q_i, kv_i, d_i = inp
        o = jnp.einsum('bhtk,bhkv->bhtv', q_i, s)
        s = d_i[..., None] * s + kv_i
        return s, o

    s_fin, o_inter = lax.scan(step, s0, (q_t, kv_chunk, decay))
    o = (o_intra + o_inter).transpose(1, 0, 3, 2, 4).reshape(bsz, L, GLA_HEADS, GLA_DV)
    return o, s_fin


def _gla_mixer(q, k, v, g_low, r, s0, w_g2, b_g, norm_gla):
    bsz, L, _ = q.shape
    f32 = jnp.float32
    qf = q.astype(f32).reshape(bsz, L, GLA_HEADS, GLA_DK) * (GLA_DK ** -0.5)
    kf = k.astype(f32).reshape(bsz, L, GLA_HEADS, GLA_DK)
    vf = v.astype(f32).reshape(bsz, L, GLA_HEADS, GLA_DV)
    log_a = jax.nn.log_sigmoid((g_low @ w_g2 + b_g).astype(f32)) / GLA_TAU
    log_a = log_a.reshape(bsz, L, GLA_HEADS, GLA_DK)
    o, s_fin = _gla_chunked(qf, kf, vf, log_a, s0.astype(f32))
    o = _rmsnorm(o, norm_gla).reshape(bsz, L, GLA_WIDTH).astype(q.dtype)
    return o * jax.nn.silu(r), s_fin


def _mem_kv(mem, norm_mem, w_xk, w_xv):
    bsz = mem.shape[0]
    m = _rmsnorm(mem, norm_mem)
    mk = (m @ w_xk).reshape(bsz, N_MEM, X_HEADS, X_HEAD_DIM)
    mv = (m @ w_xv).reshape(bsz, N_MEM, X_HEADS, X_HEAD_DIM)
    return mk, mv


def _cross_attn(x, mk, mv, norm_x, w_xq, w_xo):
    bsz, L, _ = x.shape
    h = _rmsnorm(x, norm_x)
    q = (h @ w_xq).reshape(bsz, L, X_HEADS, X_HEAD_DIM)
    s = jnp.einsum('blhd,bmhd->bhlm', q, mk.astype(q.dtype)).astype(jnp.float32) * (X_HEAD_DIM ** -0.5)
    p = jax.nn.softmax(s, axis=-1).astype(x.dtype)
    o = jnp.einsum('bhlm,bmhd->blhd', p, mv.astype(x.dtype)).reshape(bsz, L, D_MODEL)
    return o @ w_xo


def _conv_ffn(x, buf, norm_ffn, w_up, conv_w, conv_b, w_down):
    L = x.shape[1]
    h = _rmsnorm(x, norm_ffn)
    hu = h @ w_up
    a, g = jnp.split(hu, [D_FF], axis=-1)
    ext = jnp.concatenate([buf.astype(a.dtype), a], axis=1)
    conv = conv_b
    for i in range(CONV_W):
        conv = conv + ext[:, i:i + L] * conv_w[i]
    out = (jax.nn.gelu(conv, approximate=False) * g) @ w_down
    return out, ext[:, -(CONV_W - 1):]


def _layer(x, mk, mv, s5_re, s5_im, gla_s, conv_buf, w):
    h = _rmsnorm(x, w['norm_mix'])
    proj = h @ w['w_in']
    u, q, k, v, g_low, r = jnp.split(proj, SPLITS, axis=-1)
    y_s5, s5_re_n, s5_im_n = _s5_mixer(u, s5_re, s5_im, w['lam_re'], w['lam_im'], w['log_dt'],
                                      w['b_re'], w['b_im'], w['c_re'], w['c_im'], w['d'],
                                      w['w_glu'], w['b_glu'])
    y_s5 = _rmsnorm(y_s5.astype(x.dtype), w['norm_s5_out'])
    y_gla, gla_n = _gla_mixer(q, k, v, g_low, r, gla_s, w['w_g2'], w['b_g'], w['norm_gla_out'])
    x = x + jnp.concatenate([y_s5, y_gla], axis=-1) @ w['w_out']
    x = x + _cross_attn(x, mk, mv, w['norm_xattn'], w['w_xq'], w['w_xo'])
    ffn_out, conv_n = _conv_ffn(x, conv_buf, w['norm_ffn'], w['w_up'], w['conv_w'], w['conv_b'], w['w_down'])
    x = x + ffn_out
    return x, s5_re_n, s5_im_n, gla_n, conv_n


def setup_inputs(seed: int = 0) -> dict:
    key = jax.random.key(seed)
    ks = jax.random.split(key, 48)
    f32 = jnp.float32

    def nrm(k, shape, scale):
        return scale * jax.random.normal(k, shape, f32)

    def gain(k, shape):
        return 1.0 + nrm(k, shape, 0.01)

    Ld = DEPTH
    n_idx = jnp.arange(S5_STATE, dtype=f32)
    inp = {}
    inp['x_prompt'] = nrm(ks[0], (BATCH, SEQ, D_MODEL), 1.0)
    inp['x_sample'] = nrm(ks[1], (DEC_BATCH, DEC_SEQ, D_MODEL), 1.0)
    inp['mem_prompt'] = nrm(ks[2], (BATCH, N_MEM, D_MODEL), 1.0)
    inp['cache_mem_k'] = nrm(ks[3], (Ld, DEC_BATCH, N_MEM, X_HEADS, X_HEAD_DIM), 1.0)
    inp['cache_mem_v'] = nrm(ks[4], (Ld, DEC_BATCH, N_MEM, X_HEADS, X_HEAD_DIM), 1.0)
    inp['state_s5_re'] = nrm(ks[5], (Ld, DEC_BATCH, S5_GROUPS, S5_STATE), 0.5)
    inp['state_s5_im'] = nrm(ks[6], (Ld, DEC_BATCH, S5_GROUPS, S5_STATE), 0.5)
    inp['state_gla'] = nrm(ks[7], (Ld, DEC_BATCH, GLA_HEADS, GLA_DK, GLA_DV), 0.3)
    inp['state_conv'] = nrm(ks[8], (Ld, DEC_BATCH, CONV_W - 1, D_FF), 1.0)
    inp['norm_mix'] = gain(ks[9], (Ld, D_MODEL))
    inp['w_in'] = nrm(ks[10], (Ld, D_MODEL, IN_WIDTH), D_MODEL ** -0.5)
    inp['s5_lambda_re'] = -0.5 + nrm(ks[11], (Ld, S5_GROUPS, S5_STATE), 0.01)
    inp['s5_lambda_im'] = jnp.pi * n_idx + nrm(ks[12], (Ld, S5_GROUPS, S5_STATE), 0.01)
    inp['s5_log_dt'] = jax.random.uniform(ks[13], (Ld, S5_GROUPS), f32, math.log(0.001), math.log(0.1))
    inp['s5_b_re'] = nrm(ks[14], (Ld, S5_GROUPS, S5_STATE, S5_GROUP), (2.0 * S5_GROUP) ** -0.5)
    inp['s5_b_im'] = nrm(ks[15], (Ld, S5_GROUPS, S5_STATE, S5_GROUP), (2.0 * S5_GROUP) ** -0.5)
    inp['s5_c_re'] = nrm(ks[16], (Ld, S5_GROUPS, S5_GROUP, S5_STATE), (2.0 * S5_STATE) ** -0.5)
    inp['s5_c_im'] = nrm(ks[17], (Ld, S5_GROUPS, S5_GROUP, S5_STATE), (2.0 * S5_STATE) ** -0.5)
    inp['s5_d'] = nrm(ks[18], (Ld, S5_GROUPS, S5_GROUP), 1.0)
    inp['s5_w_glu'] = nrm(ks[19], (Ld, S5_GROUPS, S5_GROUP, S5_GROUP), S5_GROUP ** -0.5)
    inp['s5_b_glu'] = nrm(ks[20], (Ld, S5_GROUPS, S5_GROUP), 0.01)
    inp['norm_s5_out'] = gain(ks[21], (Ld, S5_WIDTH))
    inp['gla_w_g2'] = nrm(ks[22], (Ld, GLA_RANK, GLA_QK_WIDTH), GLA_RANK ** -0.5)
    inp['gla_b_g'] = nrm(ks[23], (Ld, GLA_QK_WIDTH), 0.1)
    inp['norm_gla_out'] = gain(ks[24], (Ld, GLA_DV))
    inp['w_out'] = nrm(ks[25], (Ld, MIX_WIDTH, D_MODEL), MIX_WIDTH ** -0.5)
    inp['norm_xattn'] = gain(ks[26], (Ld, D_MODEL))
    inp['norm_mem'] = gain(ks[27], (Ld, D_MODEL))
    inp['w_xq'] = nrm(ks[28], (Ld, D_MODEL, D_MODEL), D_MODEL ** -0.5)
    inp['w_xk'] = nrm(ks[29], (Ld, D_MODEL, D_MODEL), D_MODEL ** -0.5)
    inp['w_xv'] = nrm(ks[30], (Ld, D_MODEL, D_MODEL), D_MODEL ** -0.5)
    inp['w_xo'] = nrm(ks[31], (Ld, D_MODEL, D_MODEL), D_MODEL ** -0.5)
    inp['norm_ffn'] = gain(ks[32], (Ld, D_MODEL))
    inp['w_up'] = nrm(ks[33], (Ld, D_MODEL, 2 * D_FF), D_MODEL ** -0.5)
    inp['conv_w'] = nrm(ks[34], (Ld, CONV_W, D_FF), CONV_W ** -0.5)
    inp['conv_b'] = nrm(ks[35], (Ld, D_FF), 0.01)
    inp['w_down'] = nrm(ks[36], (Ld, D_FF, D_MODEL), D_FF ** -0.5)
    inp['norm_final'] = gain(ks[37], (D_MODEL,))
    return inp


def reference(x_prompt, x_sample, mem_prompt, cache_mem_k, cache_mem_v, state_s5_re, state_s5_im,
              state_gla, state_conv, norm_mix, w_in, s5_lambda_re, s5_lambda_im, s5_log_dt,
              s5_b_re, s5_b_im, s5_c_re, s5_c_im, s5_d, s5_w_glu, s5_b_glu, norm_s5_out,
              gla_w_g2, gla_b_g, norm_gla_out, w_out, norm_xattn, norm_mem, w_xq, w_xk, w_xv,
              w_xo, norm_ffn, w_up, conv_w, conv_b, w_down, norm_final):
    f32 = jnp.float32
    bp = x_prompt.shape[0]
    xp = x_prompt
    xs = x_sample
    p_mk, p_mv, p_re, p_im, p_gla, p_conv = [], [], [], [], [], []
    s_re, s_im, s_gla, s_conv = [], [], [], []
    for l in range(DEPTH):
        w = dict(norm_mix=norm_mix[l], w_in=w_in[l], lam_re=s5_lambda_re[l], lam_im=s5_lambda_im[l],
                 log_dt=s5_log_dt[l], b_re=s5_b_re[l], b_im=s5_b_im[l], c_re=s5_c_re[l],
                 c_im=s5_c_im[l], d=s5_d[l], w_glu=s5_w_glu[l], b_glu=s5_b_glu[l],
                 norm_s5_out=norm_s5_out[l], w_g2=gla_w_g2[l], b_g=gla_b_g[l],
                 norm_gla_out=norm_gla_out[l], w_out=w_out[l], norm_xattn=norm_xattn[l],
                 w_xq=w_xq[l], w_xo=w_xo[l], norm_ffn=norm_ffn[l], w_up=w_up[l],
                 conv_w=conv_w[l], conv_b=conv_b[l], w_down=w_down[l])
        mk, mv = _mem_kv(mem_prompt, norm_mem[l], w_xk[l], w_xv[l])
        z_s5 = jnp.zeros((bp, S5_GROUPS, S5_STATE), f32)
        z_gla = jnp.zeros((bp, GLA_HEADS, GLA_DK, GLA_DV), f32)
        z_conv = jnp.zeros((bp, CONV_W - 1, D_FF), xp.dtype)
        xp, n_re, n_im, n_gla, n_conv = _layer(xp, mk, mv, z_s5, z_s5, z_gla, z_conv, w)
        p_mk.append(mk)
        p_mv.append(mv)
        p_re.append(n_re)
        p_im.append(n_im)
        p_gla.append(n_gla)
        p_conv.append(n_conv)
        xs, m_re, m_im, m_gla, m_conv = _layer(xs, cache_mem_k[l], cache_mem_v[l], state_s5_re[l],
                                               state_s5_im[l], state_gla[l], state_conv[l], w)
        s_re.append(m_re)
        s_im.append(m_im)
        s_gla.append(m_gla)
        s_conv.append(m_conv)
    y_prompt = _rmsnorm(xp, norm_final)
    y_sample = _rmsnorm(xs, norm_final)
    return (y_prompt, y_sample,
            jnp.stack(p_mk), jnp.stack(p_mv), jnp.stack(p_re), jnp.stack(p_im),
            jnp.stack(p_gla), jnp.stack(p_conv),
            jnp.stack(s_re), jnp.stack(s_im), jnp.stack(s_gla), jnp.stack(s_conv))
```

```python
import functools

import jax
import jax.numpy as jnp
from jax import lax
from jax.experimental import pallas as pl
from jax.experimental.pallas import tpu as pltpu

F32 = jnp.float32
BF16 = jnp.bfloat16

EPS = 1e-6
S5_GROUP = 16
S5_STATE = 64
S5_BLOCK_GROUPS = 16
GLA_HEADS = 4
GLA_RANK = 16
GLA_TAU = 16.0
GLA_CHUNK = 64
X_HEADS = 4
CONV_W = 3

SUBLANES = 8
VMEM_LIMIT_BYTES = 56 * 1024 * 1024

_CONTRACT_LAST = (((1,), (1,)), ((), ()))
_CONTRACT_FIRST = (((0,), (0,)), ((), ()))


def _params(*semantics):
    return pltpu.CompilerParams(dimension_semantics=semantics, vmem_limit_bytes=VMEM_LIMIT_BYTES)


def _dot(a, b):
    return jnp.dot(a, b, preferred_element_type=F32)


def _for_row_chunks(n_rows, chunk, fn):
    def body(i, carry):
        fn(pl.multiple_of(i * chunk, chunk))
        return carry
    lax.fori_loop(0, n_rows // chunk, body, 0)


def _rms(x, gain):
    r = lax.rsqrt(jnp.mean(x * x, axis=-1, keepdims=True) + EPS)
    return (x * r) * gain


def _gelu(x):
    return 0.5 * x * (1.0 + lax.erf(x * (2.0 ** -0.5)))


def _rms_rows_to_bf16(x_ref, gain_ref, h_ref, n_rows, row0=0, chunk=128):
    def f(r0):
        h_ref[pl.ds(row0 + r0, chunk), :] = _rms(x_ref[pl.ds(r0, chunk), :], gain_ref[...]).astype(BF16)
    _for_row_chunks(n_rows, chunk, f)


def _s5_prep_body(lr_ref, li_ref, ldt_ref, br_ref, bi_ref, apr_ref, api_ref, bbr_ref, bbi_ref):
    lr = lr_ref[...]
    li = li_ref[...]
    dt = jnp.exp(ldt_ref[...])
    mag = jnp.exp(lr * dt)
    ar = mag * jnp.cos(li * dt)
    ai = mag * jnp.sin(li * dt)
    den = lr * lr + li * li
    pr = ar - 1.0
    qr = (pr * lr + ai * li) / den
    qi = (ai * lr - pr * li) / den
    for p in range(S5_GROUP):
        br = br_ref[p]
        bi = bi_ref[p]
        bbr_ref[p] = qr * br - qi * bi
        bbi_ref[p] = qr * bi + qi * br
    cr, ci = ar, ai
    apr_ref[0] = cr
    api_ref[0] = ci
    for k in range(1, SUBLANES):
        cr, ci = cr * ar - ci * ai, cr * ai + ci * ar
        apr_ref[k] = cr
        api_ref[k] = ci


def _s5_prep(lam_re, lam_im, log_dt, b_re, b_im):
    g, n = lam_re.shape
    out = jax.ShapeDtypeStruct((SUBLANES, g, n), F32)
    outb = jax.ShapeDtypeStruct((S5_GROUP, g, n), F32)
    return pl.pallas_call(
        _s5_prep_body, out_shape=(out, out, outb, outb), name="s5_prep",
    )(lam_re, lam_im, log_dt.reshape(g, 1), jnp.transpose(b_re, (2, 0, 1)), jnp.transpose(b_im, (2, 0, 1)))


def _s5_tables(lam_re, lam_im, log_dt, b_re, b_im, c_re, c_im, d, w_glu, b_glu, seg):
    g, n = lam_re.shape
    p = S5_GROUP
    gb = S5_BLOCK_GROUPS
    nb = g // gb
    apr, api, bbr, bbi = _s5_prep(lam_re, lam_im, log_dt, b_re, b_im)
    eye = jnp.eye(gb, dtype=F32)

    def bu_block(bb):
        return jnp.einsum('pbgn,gh->bgphn', bb.reshape(p, nb, gb, n), eye).reshape(nb, gb * p, gb * n)

    def c_block(c):
        return jnp.einsum('bgpn,gh->bgnhp', c.reshape(nb, gb, p, n), eye).reshape(nb, gb * n, gb * p)

    w_bu = jnp.concatenate([bu_block(bbr), bu_block(bbi)], axis=-1).astype(BF16)
    w_c = jnp.concatenate([c_block(c_re), -c_block(c_im)], axis=1).astype(BF16)
    w_g = jnp.einsum('bgpq,gh->bgphq', w_glu.reshape(nb, gb, p, p), eye).reshape(nb, gb * p, gb * p).astype(BF16)
    d_b = d.reshape(nb, 1, gb * p)
    bg_b = b_glu.reshape(nb, 1, gb * p)

    t = jnp.arange(SUBLANES) % seg
    pw_r = apr.reshape(SUBLANES, nb, gb * n)
    pw_i = api.reshape(SUBLANES, nb, gb * n)

    def masked(k):
        m = (t >= k)[:, None, None]
        return jnp.where(m, pw_r[k - 1][None], 0.0), jnp.where(m, pw_i[k - 1][None], 0.0)

    tabs = []
    k = 1
    while k < seg:
        tabs.extend(masked(k))
        k *= 2
    tabs.append(pw_r[t])
    tabs.append(pw_i[t])
    tab = jnp.transpose(jnp.stack(tabs), (2, 0, 1, 3))
    return w_bu, w_c, w_g, d_b, bg_b, tab


def _in_proj_body(x_ref, gain_ref, w_ref, wr_ref, wg_ref, wg2_ref, bg_ref, o_ref, h_ref, *, tm, n_main, n_r):
    j = pl.program_id(1)

    @pl.when(j == 0)
    def _():
        _rms_rows_to_bf16(x_ref, gain_ref, h_ref, tm)

    @pl.when(j < n_main)
    def _():
        o_ref[...] = _dot(h_ref[...], w_ref[...].astype(BF16))

    @pl.when((j >= n_main) & (j < n_main + n_r))
    def _():
        o_ref[...] = _dot(h_ref[...], wr_ref[...].astype(BF16))

    @pl.when(j == n_main + n_r)
    def _():
        g_low = _dot(h_ref[...], wg_ref[...].astype(BF16))
        pre = _dot(g_low.astype(BF16), wg2_ref[...].astype(BF16)) + bg_ref[...]
        log_sig = jnp.minimum(pre, 0.0) - jnp.log1p(jnp.exp(-jnp.abs(pre)))
        o_ref[...] = log_sig * (1.0 / GLA_TAU)


def _in_proj(x, gain, w_in, w_g2, b_g, *, tm, tn=512):
    n, dm = x.shape
    mix = dm
    n_u_q_k_v = mix // 2 + 2 * (mix // 4) + mix // 2
    col_g = n_u_q_k_v
    col_r = n_u_q_k_v + GLA_RANK
    w_r = w_in[:, col_r:]
    w_g = w_in[:, col_g:col_r]
    n_main = n_u_q_k_v // tn
    n_r = w_r.shape[1] // tn
    n_cols = n_u_q_k_v + w_r.shape[1] + w_g2.shape[1]
    body = functools.partial(_in_proj_body, tm=tm, n_main=n_main, n_r=n_r)
    return pl.pallas_call(
        body,
        grid=(n // tm, n_main + n_r + 1),
        in_specs=[
            pl.BlockSpec((tm, dm), lambda i, j: (i, 0)),
            pl.BlockSpec((1, dm), lambda i, j: (0, 0)),
            pl.BlockSpec((dm, tn), lambda i, j: (0, jnp.minimum(j, n_main - 1))),
            pl.BlockSpec((dm, tn), lambda i, j: (0, jnp.clip(j - n_main, 0, n_r - 1))),
            pl.BlockSpec((dm, GLA_RANK), lambda i, j: (0, 0)),
            pl.BlockSpec(w_g2.shape, lambda i, j: (0, 0)),
            pl.BlockSpec((1, w_g2.shape[1]), lambda i, j: (0, 0)),
        ],
        out_specs=pl.BlockSpec((tm, tn), lambda i, j: (i, j)),
        out_shape=jax.ShapeDtypeStruct((n, n_cols), F32),
        scratch_shapes=[pltpu.VMEM((tm, dm), BF16)],
        compiler_params=_params("arbitrary", "arbitrary"),
        name="in_proj",
    )(x, gain.reshape(1, dm), w_in, w_r, w_g, w_g2, b_g.reshape(1, -1))


def _s5_body(*refs, tt, seg, ns):
    if seg == SUBLANES:
        u_ref, wbu_ref, wc_ref, wg_ref, d_ref, bg_ref, tab_ref, z_ref, hr_ref, hi_ref, bu_ref, cr_ref, ci_ref = refs
    else:
        u_ref, wbu_ref, wc_ref, wg_ref, d_ref, bg_ref, tab_ref, h0r_ref, h0i_ref, z_ref, hr_ref, hi_ref, bu_ref = refs
    n_lvl = seg.bit_length() - 1
    bu_ref[...] = _dot(u_ref[...].astype(BF16), wbu_ref[...])

    if seg == SUBLANES:
        @pl.when(pl.program_id(2) == 0)
        def _():
            cr_ref[...] = jnp.zeros_like(cr_ref)
            ci_ref[...] = jnp.zeros_like(ci_ref)

    def block(i, carry):
        r0 = pl.multiple_of(i * SUBLANES, SUBLANES)
        xr = bu_ref[pl.ds(r0, SUBLANES), 0:ns]
        xi = bu_ref[pl.ds(r0, SUBLANES), ns:2 * ns]
        for lvl in range(n_lvl):
            k = 1 << lvl
            ar = tab_ref[2 * lvl]
            ai = tab_ref[2 * lvl + 1]
            sr = pltpu.roll(xr, k, 0)
            si = pltpu.roll(xi, k, 0)
            xr, xi = xr + (ar * sr - ai * si), xi + (ar * si + ai * sr)
        pr = tab_ref[2 * n_lvl]
        pi = tab_ref[2 * n_lvl + 1]
        if seg == SUBLANES:
            c_r, c_i = carry
        else:
            c_r = h0r_ref[pl.ds(r0, SUBLANES), :]
            c_i = h0i_ref[pl.ds(r0, SUBLANES), :]
        hr = xr + (pr * c_r - pi * c_i)
        hi = xi + (pr * c_i + pi * c_r)
        bu_ref[pl.ds(r0, SUBLANES), 0:ns] = hr
        bu_ref[pl.ds(r0, SUBLANES), ns:2 * ns] = hi
        if seg == SUBLANES:
            last = SUBLANES - 1
            return (jnp.broadcast_to(hr[last:last + 1], hr.shape), jnp.broadcast_to(hi[last:last + 1], hi.shape))
        return carry

    if seg == SUBLANES:
        c_r, c_i = lax.fori_loop(0, tt // SUBLANES, block, (cr_ref[...], ci_ref[...]))
        cr_ref[...] = c_r
        ci_ref[...] = c_i
        hr_ref[...] = c_r[0:1]
        hi_ref[...] = c_i[0:1]
    else:
        lax.fori_loop(0, tt // SUBLANES, block, 0)
        hr_ref[...] = bu_ref[:, 0:ns]
        hi_ref[...] = bu_ref[:, ns:2 * ns]

    y = _dot(bu_ref[...].astype(BF16), wc_ref[...]) + d_ref[...] * u_ref[...]
    z = _gelu(y)
    gate = jax.nn.sigmoid(_dot(z.astype(BF16), wg_ref[...]) + bg_ref[...])
    z_ref[...] = z * gate


def _s5_mixer(proj3, tables, *, tt, seg, h0=None):
    w_bu, w_c, w_g, d_b, bg_b, tab = tables
    bsz, seq, _ = proj3.shape
    nb, cw, ns2 = w_bu.shape
    ns = ns2 // 2
    n_tab = tab.shape[1]
    body = functools.partial(_s5_body, tt=tt, seg=seg, ns=ns)
    wspec = lambda shape: pl.BlockSpec((None,) + shape, lambda b, g, t: (g,) + (0,) * len(shape))
    in_specs = [
        pl.BlockSpec((None, tt, cw), lambda b, g, t: (b, t, g)),
        wspec((cw, ns2)), wspec((ns2, cw)), wspec((cw, cw)), wspec((1, cw)), wspec((1, cw)),
        wspec((n_tab, SUBLANES, ns)),
    ]
    args = [proj3, w_bu, w_c, w_g, d_b, bg_b, tab]
    z_shape = jax.ShapeDtypeStruct((bsz, seq, nb * cw), F32)
    z_spec = pl.BlockSpec((None, tt, cw), lambda b, g, t: (b, t, g))
    scratch = [pltpu.VMEM((tt, ns2), F32)]
    if seg == SUBLANES:
        st_shape = jax.ShapeDtypeStruct((bsz, nb, 1, ns), F32)
        st_spec = pl.BlockSpec((None, None, 1, ns), lambda b, g, t: (b, g, 0, 0))
        scratch += [pltpu.VMEM((SUBLANES, ns), F32), pltpu.VMEM((SUBLANES, ns), F32)]
    else:
        h0r, h0i = h0
        in_specs += [pl.BlockSpec((None, tt, ns), lambda b, g, t: (b, t, g))] * 2
        args += [h0r, h0i]
        st_shape = jax.ShapeDtypeStruct((bsz, seq, nb * ns), F32)
        st_spec = pl.BlockSpec((None, tt, ns), lambda b, g, t: (b, t, g))
    return pl.pallas_call(
        body,
        grid=(bsz, nb, seq // tt),
        in_specs=in_specs,
        out_specs=(z_spec, st_spec, st_spec),
        out_shape=(z_shape, st_shape, st_shape),
        scratch_shapes=scratch,
        compiler_params=_params("arbitrary", "arbitrary", "arbitrary"),
        name="s5_seg%d" % seg,
    )(*args)


def _split3_bf16(x):
    hi = x.astype(BF16)
    r1 = x - hi.astype(F32)
    mid = r1.astype(BF16)
    lo = (r1 - mid.astype(F32)).astype(BF16)
    return hi, mid, lo


def _gla_out(o, gain, r):
    return _rms(o, gain) * (r * jax.nn.sigmoid(r))


def _gla_seq_body(q_ref, k_ref, v_ref, la_ref, r_ref, gain_ref, y_ref, st_out_ref, st_ref, *, tt, dk):
    c = GLA_CHUNK

    @pl.when(pl.program_id(2) == 0)
    def _():
        st_ref[...] = jnp.zeros_like(st_ref)

    row = lax.broadcasted_iota(jnp.int32, (tt, tt), 0)
    col = lax.broadcasted_iota(jnp.int32, (tt, tt), 1)
    shift = c.bit_length() - 1
    tri = (col <= row) & ((row >> shift) == (col >> shift))
    tri_b = jnp.where(tri, 1.0, 0.0).astype(BF16)
    hi, mid, lo = _split3_bf16(la_ref[...])
    bcum = _dot(tri_b, hi) + _dot(tri_b, mid) + _dot(tri_b, lo)
    causal = lax.broadcasted_iota(jnp.int32, (c, c), 1) <= lax.broadcasted_iota(jnp.int32, (c, c), 0)
    scale = dk ** -0.5
    for ci in range(tt // c):
        sl = slice(ci * c, (ci + 1) * c)
        bc = bcum[sl]
        bl = bc[c - 1:c]
        qt = ((q_ref[sl, :] * scale) * jnp.exp(bc)).astype(BF16)
        kc = k_ref[sl, :]
        kt = (kc * jnp.exp(-bc)).astype(BF16)
        ke = (kc * jnp.exp(bl - bc)).astype(BF16)
        vb = v_ref[sl, :].astype(BF16)
        s = lax.dot_general(qt, kt, _CONTRACT_LAST, preferred_element_type=F32)
        s = jnp.where(causal, s, 0.0)
        st = st_ref[...]
        o = _dot(s.astype(BF16), vb) + lax.dot_general(qt, st.astype(BF16), _CONTRACT_LAST, preferred_element_type=F32)
        st_ref[...] = st * jnp.exp(bl) + lax.dot_general(vb, ke, _CONTRACT_FIRST, preferred_element_type=F32)
        y_ref[sl, :] = _gla_out(o, gain_ref[...], r_ref[sl, :])
    st_out_ref[...] = st_ref[...]


def _gla_seq(proj3, gain, *, tt):
    bsz, seq, _ = proj3.shape
    hh = GLA_HEADS
    dv = gain.shape[0]
    dk = dv // 2
    col_q = 1024 // dk
    col_k = (1024 + hh * dk) // dk
    col_v = (1024 + 2 * hh * dk) // dv
    col_r = (1024 + 2 * hh * dk + hh * dv) // dv
    col_la = (1024 + 2 * hh * dk + 2 * hh * dv) // dk
    spec = lambda w, c0: pl.BlockSpec((None, tt, w), lambda b, h, t: (b, t, c0 + h))
    body = functools.partial(_gla_seq_body, tt=tt, dk=dk)
    return pl.pallas_call(
        body,
        grid=(bsz, hh, seq // tt),
        in_specs=[spec(dk, col_q), spec(dk, col_k), spec(dv, col_v), spec(dk, col_la), spec(dv, col_r),
                  pl.BlockSpec((1, dv), lambda b, h, t: (0, 0))],
        out_specs=(pl.BlockSpec((None, tt, dv), lambda b, h, t: (b, t, h)),
                   pl.BlockSpec((None, None, dv, dk), lambda b, h, t: (b, h, 0, 0))),
        out_shape=(jax.ShapeDtypeStruct((bsz, seq, hh * dv), F32),
                   jax.ShapeDtypeStruct((bsz, hh, dv, dk), F32)),
        scratch_shapes=[pltpu.VMEM((dv, dk), F32)],
        compiler_params=_params("arbitrary", "arbitrary", "arbitrary"),
        name="gla_seq",
    )(proj3, proj3, proj3, proj3, proj3, gain.reshape(1, dv))


def _gla_step_body(q_ref, k_ref, v_ref, la_ref, r_ref, gain_ref, s_ref, y_ref, so_ref, *, nb, seg, dk, dv):
    hh = GLA_HEADS
    rows = SUBLANES
    per = rows // seg
    row = lax.broadcasted_iota(jnp.int32, (rows, rows), 0)
    col = lax.broadcasted_iota(jnp.int32, (rows, rows), 1)
    shift = seg.bit_length() - 1
    causal = (col <= row) & ((row >> shift) == (col >> shift))
    rid = lax.broadcasted_iota(jnp.int32, (rows, 1), 0)
    t_in = rid & (seg - 1)
    ones_b = jnp.ones((rows, dv), BF16)
    scale = dk ** -0.5

    def pair(i, carry):
        r0 = pl.multiple_of(i * rows, rows)
        for h in range(hh):
            ks = slice(h * dk, (h + 1) * dk)
            vs = slice(h * dv, (h + 1) * dv)
            la = la_ref[pl.ds(r0, rows), ks]
            bc = la
            k = 1
            while k < seg:
                bc = bc + jnp.where(t_in >= k, pltpu.roll(bc, k, 0), 0.0)
                k *= 2
            bl = jnp.zeros_like(bc)
            for e in range(per):
                last = e * seg + seg - 1
                bl = jnp.where((rid >> shift) == e, bc[last:last + 1], bl)
            qt = ((q_ref[pl.ds(r0, rows), ks] * scale) * jnp.exp(bc)).astype(BF16)
            kc = k_ref[pl.ds(r0, rows), ks]
            kt = (kc * jnp.exp(-bc)).astype(BF16)
            ke = kc * jnp.exp(bl - bc)
            vb = v_ref[pl.ds(r0, rows), vs].astype(BF16)
            s = lax.dot_general(qt, kt, _CONTRACT_LAST, preferred_element_type=F32)
            s = jnp.where(causal, s, 0.0)
            o = _dot(s.astype(BF16), vb)
            o_inter = jnp.zeros_like(o)
            for e in range(per):
                mine = (rid >> shift) == e
                st = s_ref[i * per + e, h]
                o_e = _dot(qt, st.astype(BF16))
                o_inter = jnp.where(mine, o_e, o_inter)
                last = e * seg + seg - 1
                dec = jnp.exp(bc[last:last + 1])
                d_hi, d_mid, d_lo = _split3_bf16(dec)
                d3 = jnp.where(rid == 0, d_hi.astype(F32),
                               jnp.where(rid == 1, d_mid.astype(F32),
                                         jnp.where(rid == 2, d_lo.astype(F32), 0.0))).astype(BF16)
                dec_cols = lax.dot_general(d3, ones_b, _CONTRACT_FIRST, preferred_element_type=F32)
                ke_e = jnp.where(mine, ke, 0.0).astype(BF16)
                kv = lax.dot_general(ke_e, vb, _CONTRACT_FIRST, preferred_element_type=F32)
                so_ref[i * per + e, h] = dec_cols * st + kv
            y_ref[pl.ds(r0, rows), vs] = _gla_out(o + o_inter, gain_ref[...], r_ref[pl.ds(r0, rows), vs])
        return carry

    lax.fori_loop(0, nb // per, pair, 0)


def _gla_step(proj, gain, state, *, seg, nb=8):
    n = proj.shape[0]
    bsz, hh, dk, dv = state.shape
    rows = nb * seg
    c_q = 1024 // (hh * dk)
    c_k = c_q + 1
    c_v = (1024 + 2 * hh * dk) // (hh * dv)
    c_r = c_v + 1
    c_la = (1024 + 2 * hh * dk + 2 * hh * dv) // (hh * dk)
    spec = lambda w, c0: pl.BlockSpec((rows, w), lambda i: (i, c0))
    body = functools.partial(_gla_step_body, nb=nb, seg=seg, dk=dk, dv=dv)
    return pl.pallas_call(
        body,
        grid=(bsz // nb,),
        in_specs=[spec(hh * dk, c_q), spec(hh * dk, c_k), spec(hh * dv, c_v), spec(hh * dk, c_la), spec(hh * dv, c_r),
                  pl.BlockSpec((1, dv), lambda i: (0, 0)),
                  pl.BlockSpec((nb, hh, dk, dv), lambda i: (i, 0, 0, 0))],
        out_specs=(pl.BlockSpec((rows, hh * dv), lambda i: (i, 0)),
                   pl.BlockSpec((nb, hh, dk, dv), lambda i: (i, 0, 0, 0))),
        out_shape=(jax.ShapeDtypeStruct((n, hh * dv), F32), jax.ShapeDtypeStruct(state.shape, F32)),
        compiler_params=_params("arbitrary"),
        name="gla_step",
    )(proj, proj, proj, proj, proj, gain.reshape(1, dv), state)


def _tiled_linear(name, prologue, row_ins, aux_ins, aux_specs, w, *, tm, tn, residual=None):
    n = row_ins[0].shape[0]
    k_dim, n_cols = w.shape
    nr, na = len(row_ins), len(aux_ins)
    has_res = residual is not None

    def body(*refs):
        row_refs = refs[:nr]
        aux_refs = refs[nr:nr + na]
        w_ref = refs[nr + na]
        o_ref, h_ref = refs[-2], refs[-1]

        @pl.when(pl.program_id(1) == 0)
        def _():
            prologue(row_refs, aux_refs, h_ref)

        acc = _dot(h_ref[...], w_ref[...].astype(BF16))
        if has_res:
            acc = acc + refs[nr + na + 1][...]
        o_ref[...] = acc

    in_specs = [pl.BlockSpec((tm, a.shape[1]), lambda i, j: (i, 0)) for a in row_ins]
    in_specs += list(aux_specs)
    in_specs.append(pl.BlockSpec((k_dim, tn), lambda i, j: (0, j)))
    args = list(row_ins) + list(aux_ins) + [w]
    if has_res:
        in_specs.append(pl.BlockSpec((tm, tn), lambda i, j: (i, j)))
        args.append(residual)
    return pl.pallas_call(
        body,
        grid=(n // tm, n_cols // tn),
        in_specs=in_specs,
        out_specs=pl.BlockSpec((tm, tn), lambda i, j: (i, j)),
        out_shape=jax.ShapeDtypeStruct((n, n_cols), F32),
        scratch_shapes=[pltpu.VMEM((tm, k_dim), BF16)],
        compiler_params=_params("arbitrary", "arbitrary"),
        name=name,
    )(*args)


def _whole(a):
    return pl.BlockSpec(a.shape, lambda i, j: (0,) * a.ndim)


def _norm_linear(name, x, gain, w, *, tm, tn=512):
    def prologue(row_refs, aux_refs, h_ref):
        _rms_rows_to_bf16(row_refs[0], aux_refs[0], h_ref, tm)
    g = gain.reshape(1, -1)
    return _tiled_linear(name, prologue, [x], [g], [_whole(g)], w, tm=tm, tn=tn)


def _cast_linear(name, x, w, residual, *, tm, tn=512):
    def prologue(row_refs, aux_refs, h_ref):
        def f(r0):
            h_ref[pl.ds(r0, 128), :] = row_refs[0][pl.ds(r0, 128), :].astype(BF16)
        _for_row_chunks(tm, 128, f)
    return _tiled_linear(name, prologue, [x], [], [], w, tm=tm, tn=tn, residual=residual)


def _out_proj(z_s5, y_gla, gain_s5, w_out, x, *, tm, tn=512):
    ws = z_s5.shape[1]

    def prologue(row_refs, aux_refs, h_ref):
        def f(r0):
            h_ref[pl.ds(r0, 128), 0:ws] = _rms(row_refs[0][pl.ds(r0, 128), :], aux_refs[0][...]).astype(BF16)
            h_ref[pl.ds(r0, 128), ws:] = row_refs[1][pl.ds(r0, 128), :].astype(BF16)
        _for_row_chunks(tm, 128, f)
    g = gain_s5.reshape(1, -1)
    return _tiled_linear("out_proj", prologue, [z_s5, y_gla], [g], [_whole(g)], w_out, tm=tm, tn=tn, residual=x)


def _attend(q, kb, vb, scale):
    s = lax.dot_general(q, kb, _CONTRACT_LAST, preferred_element_type=F32) * scale
    e = jnp.exp(s - jnp.max(s, axis=-1, keepdims=True))
    p = e / jnp.sum(e, axis=-1, keepdims=True)
    return _dot(p.astype(BF16), vb)


def _xattn_seq_out(q, mk_b, mv_b, w_xo, x, *, tm, seq, tn=512, chunk=256):
    dm = q.shape[1]
    hd = dm // X_HEADS
    scale = hd ** -0.5
    tiles_per_seq = seq // tm

    def prologue(row_refs, aux_refs, h_ref):
        q_ref = row_refs[0]
        mk_ref, mv_ref = aux_refs
        for h in range(X_HEADS):
            hs = slice(h * hd, (h + 1) * hd)

            def f(r0, hs=hs):
                o = _attend(q_ref[pl.ds(r0, chunk), hs].astype(BF16), mk_ref[:, hs], mv_ref[:, hs], scale)
                h_ref[pl.ds(r0, chunk), hs] = o.astype(BF16)
            _for_row_chunks(tm, chunk, f)

    mspec = pl.BlockSpec((None,) + mk_b.shape[1:], lambda i, j: (i // tiles_per_seq, 0, 0))
    return _tiled_linear("xattn_seq_out", prologue, [q], [mk_b, mv_b], [mspec, mspec], w_xo, tm=tm, tn=tn, residual=x)


def _xattn_step_body(q_ref, k_ref, v_ref, o_ref, *, nb, seg, hd):
    rows = SUBLANES
    per = rows // seg
    shift = seg.bit_length() - 1
    rid = lax.broadcasted_iota(jnp.int32, (rows, 1), 0)
    scale = hd ** -0.5
    for i in range(nb // per):
        r0 = i * rows
        for h in range(X_HEADS):
            hs = slice(h * hd, (h + 1) * hd)
            q = q_ref[r0:r0 + rows, hs].astype(BF16)
            o = jnp.zeros((rows, hd), F32)
            for e in range(per):
                b = i * per + e
                o_e = _attend(q, k_ref[b, :, hs].astype(BF16), v_ref[b, :, hs].astype(BF16), scale)
                o = jnp.where((rid >> shift) == e, o_e, o)
            o_ref[r0:r0 + rows, hs] = o


def _xattn_step(q, mem_k, mem_v, *, seg, nb=2):
    n, dm = q.shape
    bsz, m, _ = mem_k.shape
    body = functools.partial(_xattn_step_body, nb=nb, seg=seg, hd=dm // X_HEADS)
    mspec = pl.BlockSpec((nb, m, dm), lambda i: (i, 0, 0))
    return pl.pallas_call(
        body,
        grid=(bsz // nb,),
        in_specs=[pl.BlockSpec((nb * seg, dm), lambda i: (i, 0)), mspec, mspec],
        out_specs=pl.BlockSpec((nb * seg, dm), lambda i: (i, 0)),
        out_shape=jax.ShapeDtypeStruct((n, dm), F32),
        compiler_params=_params("arbitrary"),
        name="xattn_step",
    )(q, mem_k, mem_v)


def _ffn_finish(conv, gate, wd_ref, o_ref, gfin_ref, j, nj, tm):
    act = (_gelu(conv) * gate).astype(BF16)
    o_ref[...] += _dot(act, wd_ref[...].astype(BF16))

    @pl.when(j == nj - 1)
    def _():
        def f(r0):
            o_ref[pl.ds(r0, 128), :] = _rms(o_ref[pl.ds(r0, 128), :], gfin_ref[...])
        _for_row_chunks(tm, 128, f)


def _ffn_seq_body(x_ref, gain_ref, wa_ref, wg_ref, wd_ref, cw_ref, cb_ref, gfin_ref, o_ref, tail_ref, h_ref, a_ref,
                  *, tm, halo, tiles_per_seq, nj):
    i = pl.program_id(0)
    j = pl.program_id(1)

    @pl.when(j == 0)
    def _():
        @pl.when(i % tiles_per_seq == 0)
        def _():
            h_ref[0:halo, :] = jnp.zeros((halo, h_ref.shape[1]), BF16)

        @pl.when(i % tiles_per_seq != 0)
        def _():
            h_ref[0:halo, :] = h_ref[tm:tm + halo, :]

        _rms_rows_to_bf16(x_ref, gain_ref, h_ref, tm, row0=halo)
        o_ref[...] = x_ref[...]

    a_ref[...] = _dot(h_ref[...], wa_ref[...].astype(BF16))
    gate = _dot(h_ref[halo:halo + tm, :], wg_ref[...].astype(BF16))
    conv = cb_ref[...]
    for tap in range(CONV_W):
        conv = conv + a_ref[pl.ds(halo - (CONV_W - 1) + tap, tm), :] * cw_ref[tap:tap + 1, :]
    tail_ref[...] = a_ref[pl.ds(halo + tm - SUBLANES, SUBLANES), :]
    _ffn_finish(conv, gate, wd_ref, o_ref, gfin_ref, j, nj, tm)


def _ffn_seq(x, gain, w_up, conv_w, conv_b, w_down, gain_final, *, tm, seq, tf=256, halo=16):
    n, dm = x.shape
    dff = w_down.shape[0]
    nj = dff // tf
    body = functools.partial(_ffn_seq_body, tm=tm, halo=halo, tiles_per_seq=seq // tm, nj=nj)
    return pl.pallas_call(
        body,
        grid=(n // tm, nj),
        in_specs=[
            pl.BlockSpec((tm, dm), lambda i, j: (i, 0)),
            pl.BlockSpec((1, dm), lambda i, j: (0, 0)),
            pl.BlockSpec((dm, tf), lambda i, j: (0, j)),
            pl.BlockSpec((dm, tf), lambda i, j: (0, nj + j)),
            pl.BlockSpec((tf, dm), lambda i, j: (j, 0)),
            pl.BlockSpec((CONV_W, tf), lambda i, j: (0, j)),
            pl.BlockSpec((1, tf), lambda i, j: (0, j)),
            pl.BlockSpec((1, dm), lambda i, j: (0, 0)),
        ],
        out_specs=(pl.BlockSpec((tm, dm), lambda i, j: (i, 0)),
                   pl.BlockSpec((None, SUBLANES, tf), lambda i, j: (i, 0, j))),
        out_shape=(jax.ShapeDtypeStruct((n, dm), F32),
                   jax.ShapeDtypeStruct((n // tm, SUBLANES, dff), F32)),
        scratch_shapes=[pltpu.VMEM((halo + tm, dm), BF16), pltpu.VMEM((halo + tm, tf), F32)],
        compiler_params=_params("arbitrary", "arbitrary"),
        name="ffn_seq",
    )(x, gain.reshape(1, dm), w_up, w_up, w_down, conv_w, conv_b.reshape(1, dff), gain_final.reshape(1, dm))


def _ffn_step_body(x_ref, gain_ref, wa_ref, wg_ref, wd_ref, cw_ref, cb_ref, gfin_ref, p1_ref, p2_ref,
                   o_ref, a_out_ref, h_ref, *, tm, seg, nj):
    j = pl.program_id(1)

    @pl.when(j == 0)
    def _():
        _rms_rows_to_bf16(x_ref, gain_ref, h_ref, tm)
        o_ref[...] = x_ref[...]

    a = _dot(h_ref[...], wa_ref[...].astype(BF16))
    gate = _dot(h_ref[...], wg_ref[...].astype(BF16))
    a_out_ref[...] = a
    t_in = lax.broadcasted_iota(jnp.int32, (tm, 1), 0) & (seg - 1)
    prev1 = jnp.where(t_in >= 1, pltpu.roll(a, 1, 0), p1_ref[...])
    prev2 = jnp.where(t_in >= 2, pltpu.roll(a, 2, 0), p2_ref[...])
    conv = cb_ref[...] + prev2 * cw_ref[0:1, :]
    conv = conv + prev1 * cw_ref[1:2, :]
    conv = conv + a * cw_ref[2:3, :]
    _ffn_finish(conv, gate, wd_ref, o_ref, gfin_ref, j, nj, tm)


def _ffn_step(x, gain, w_up, conv_w, conv_b, w_down, gain_final, prev1, prev2, *, seg, tf=256):
    n, dm = x.shape
    dff = w_down.shape[0]
    nj = dff // tf
    tm = n
    body = functools.partial(_ffn_step_body, tm=tm, seg=seg, nj=nj)
    return pl.pallas_call(
        body,
        grid=(1, nj),
        in_specs=[
            pl.BlockSpec((tm, dm), lambda i, j: (i, 0)),
            pl.BlockSpec((1, dm), lambda i, j: (0, 0)),
            pl.BlockSpec((dm, tf), lambda i, j: (0, j)),
            pl.BlockSpec((dm, tf), lambda i, j: (0, nj + j)),
            pl.BlockSpec((tf, dm), lambda i, j: (j, 0)),
            pl.BlockSpec((CONV_W, tf), lambda i, j: (0, j)),
            pl.BlockSpec((1, tf), lambda i, j: (0, j)),
            pl.BlockSpec((1, dm), lambda i, j: (0, 0)),
            pl.BlockSpec((tm, tf), lambda i, j: (i, j)),
            pl.BlockSpec((tm, tf), lambda i, j: (i, j)),
        ],
        out_specs=(pl.BlockSpec((tm, dm), lambda i, j: (i, 0)),
                   pl.BlockSpec((tm, tf), lambda i, j: (i, j))),
        out_shape=(jax.ShapeDtypeStruct((n, dm), F32), jax.ShapeDtypeStruct((n, dff), F32)),
        scratch_shapes=[pltpu.VMEM((tm, dm), BF16)],
        compiler_params=_params("arbitrary", "arbitrary"),
        name="ffn_step",
    )(x, gain.reshape(1, dm), w_up, w_up, w_down, conv_w, conv_b.reshape(1, dff), gain_final.reshape(1, dm),
      prev1, prev2)


def kernel(x_prompt, x_sample, mem_prompt, cache_mem_k, cache_mem_v, state_s5_re, state_s5_im, state_gla, state_conv, norm_mix, w_in, s5_lambda_re, s5_lambda_im, s5_log_dt, s5_b_re, s5_b_im, s5_c_re, s5_c_im, s5_d, s5_w_glu, s5_b_glu, norm_s5_out, gla_w_g2, gla_b_g, norm_gla_out, w_out, norm_xattn, norm_mem, w_xq, w_xk, w_xv, w_xo, norm_ffn, w_up, conv_w, conv_b, w_down, norm_final):
    depth = w_in.shape[0]
    assert depth == 1
    bp, seq, dm = x_prompt.shape
    bs, seg, _ = x_sample.shape
    n_mem = mem_prompt.shape[1]
    dff = w_down.shape[1]
    l = 0
    s5_args = (s5_lambda_re[l], s5_lambda_im[l], s5_log_dt[l], s5_b_re[l], s5_b_im[l], s5_c_re[l], s5_c_im[l],
               s5_d[l], s5_w_glu[l], s5_b_glu[l])
    tm_p = min(1024, seq)
    n_s = bs * seg

    mem2 = mem_prompt.reshape(bp * n_mem, dm)
    mk = _norm_linear("mem_k", mem2, norm_mem[l], w_xk[l], tm=min(1024, bp * n_mem))
    mv = _norm_linear("mem_v", mem2, norm_mem[l], w_xv[l], tm=min(1024, bp * n_mem))

    xp = x_prompt.reshape(bp * seq, dm)
    proj = _in_proj(xp, norm_mix[l], w_in[l], gla_w_g2[l], gla_b_g[l], tm=tm_p)
    proj3 = proj.reshape(bp, seq, -1)
    z_s5, hend_r, hend_i = _s5_mixer(proj3, _s5_tables(*s5_args, seg=SUBLANES), tt=min(512, seq), seg=SUBLANES)
    y_gla, st_gla = _gla_seq(proj3, norm_gla_out[l], tt=min(256, seq))
    x1 = _out_proj(z_s5.reshape(bp * seq, -1), y_gla.reshape(bp * seq, -1), norm_s5_out[l], w_out[l], xp, tm=tm_p)
    q = _norm_linear("xattn_q", x1, norm_xattn[l], w_xq[l], tm=tm_p)
    x2 = _xattn_seq_out(q, mk.reshape(bp, n_mem, dm).astype(BF16), mv.reshape(bp, n_mem, dm).astype(BF16),
                        w_xo[l], x1, tm=tm_p, seq=seq)
    yp, tails = _ffn_seq(x2, norm_ffn[l], w_up[l], conv_w[l], conv_b[l], w_down[l], norm_final, tm=tm_p, seq=seq)

    g_total = s5_lambda_re.shape[1]
    n_state = s5_lambda_re.shape[2]
    p_re = hend_r.reshape(1, bp, g_total, n_state)
    p_im = hend_i.reshape(1, bp, g_total, n_state)
    p_gla = jnp.swapaxes(st_gla, -1, -2)[None]
    tiles_per_seq = seq // tm_p
    p_conv = tails.reshape(bp, tiles_per_seq, SUBLANES, dff)[:, -1, SUBLANES - (CONV_W - 1):, :][None]

    xs = x_sample.reshape(n_s, dm)
    proj_s = _in_proj(xs, norm_mix[l], w_in[l], gla_w_g2[l], gla_b_g[l], tm=n_s)
    h0r = jnp.repeat(state_s5_re[l].reshape(bs, -1), seg, axis=0)[None]
    h0i = jnp.repeat(state_s5_im[l].reshape(bs, -1), seg, axis=0)[None]
    z_s5_s, hs_r, hs_i = _s5_mixer(proj_s[None], _s5_tables(*s5_args, seg=seg), tt=n_s, seg=seg, h0=(h0r, h0i))
    y_gla_s, s_gla = _gla_step(proj_s, norm_gla_out[l], state_gla[l], seg=seg)
    x1s = _out_proj(z_s5_s[0], y_gla_s, norm_s5_out[l], w_out[l], xs, tm=n_s)
    qs = _norm_linear("xattn_q_s", x1s, norm_xattn[l], w_xq[l], tm=n_s)
    att = _xattn_step(qs, cache_mem_k[l].reshape(bs, n_mem, dm), cache_mem_v[l].reshape(bs, n_mem, dm), seg=seg)
    x2s = _cast_linear("xattn_o_s", att, w_xo[l], x1s, tm=n_s)
    buf = state_conv[l]
    zeros = jnp.zeros((bs, dff), F32)
    prev1 = jnp.stack([buf[:, 1]] + [zeros] * (seg - 1), axis=1).reshape(n_s, dff)
    prev2 = jnp.stack([buf[:, 0], buf[:, 1]] + [zeros] * (seg - 2), axis=1).reshape(n_s, dff)
    ys, a_s = _ffn_step(x2s, norm_ffn[l], w_up[l], conv_w[l], conv_b[l], w_down[l], norm_final, prev1, prev2, seg=seg)

    s_re = hs_r.reshape(bs, seg, g_total, n_state)[:, seg - 1][None]
    s_im = hs_i.reshape(bs, seg, g_total, n_state)[:, seg - 1][None]
    s_conv = a_s.reshape(bs, seg, dff)[:, seg - (CONV_W - 1):, :][None]

    hd = dm // X_HEADS
    return (yp.reshape(bp, seq, dm), ys.reshape(bs, seg, dm),
            mk.reshape(1, bp, n_mem, X_HEADS, hd), mv.reshape(1, bp, n_mem, X_HEADS, hd),
            p_re, p_im, p_gla, p_conv, s_re, s_im, s_gla[None], s_conv)
```

```python
import functools

import jax
import jax.numpy as jnp
from jax import lax
from jax.experimental import pallas as pl
from jax.experimental.pallas import tpu as pltpu

F32 = jnp.float32
BF16 = jnp.bfloat16

EPS = 1e-6
S5_GROUP = 16
S5_STATE = 64
S5_BLOCK_GROUPS = 16
GLA_HEADS = 4
GLA_RANK = 16
GLA_TAU = 16.0
GLA_CHUNK = 64
X_HEADS = 4
CONV_W = 3

SUBLANES = 8
LANES = 128
VMEM_LIMIT_BYTES = 56 * 1024 * 1024

_CONTRACT_LAST = (((1,), (1,)), ((), ()))
_CONTRACT_FIRST = (((0,), (0,)), ((), ()))


def _params(*semantics):
    return pltpu.CompilerParams(dimension_semantics=semantics, vmem_limit_bytes=VMEM_LIMIT_BYTES)


def _dot(a, b):
    return jnp.dot(a, b, preferred_element_type=F32)


def _for_row_chunks(n_rows, chunk, fn):
    def body(i, carry):
        fn(pl.multiple_of(i * chunk, chunk))
        return carry
    lax.fori_loop(0, n_rows // chunk, body, 0)


def _rms(x, gain):
    r = lax.rsqrt(jnp.mean(x * x, axis=-1, keepdims=True) + EPS)
    return (x * r) * gain


def _gelu(x):
    return 0.5 * x * (1.0 + lax.erf(x * (2.0 ** -0.5)))


def _rms_rows_to_bf16(x_ref, gain_ref, h_ref, n_rows, row0=0, chunk=128):
    def f(r0):
        h_ref[pl.ds(row0 + r0, chunk), :] = _rms(x_ref[pl.ds(r0, chunk), :], gain_ref[...]).astype(BF16)
    _for_row_chunks(n_rows, chunk, f)


def _s5_prep_body(lr_ref, li_ref, ldt_ref, br_ref, bi_ref, apr_ref, api_ref, bbr_ref, bbi_ref):
    lr = lr_ref[...]
    li = li_ref[...]
    dt = jnp.exp(ldt_ref[...])
    mag = jnp.exp(lr * dt)
    ar = mag * jnp.cos(li * dt)
    ai = mag * jnp.sin(li * dt)
    den = lr * lr + li * li
    pr = ar - 1.0
    qr = (pr * lr + ai * li) / den
    qi = (ai * lr - pr * li) / den
    for p in range(S5_GROUP):
        br = br_ref[p]
        bi = bi_ref[p]
        bbr_ref[p] = qr * br - qi * bi
        bbi_ref[p] = qr * bi + qi * br
    cr, ci = ar, ai
    apr_ref[0] = cr
    api_ref[0] = ci
    for k in range(1, SUBLANES):
        cr, ci = cr * ar - ci * ai, cr * ai + ci * ar
        apr_ref[k] = cr
        api_ref[k] = ci


def _s5_prep(lam_re, lam_im, log_dt, b_re, b_im):
    g, n = lam_re.shape
    out = jax.ShapeDtypeStruct((SUBLANES, g, n), F32)
    outb = jax.ShapeDtypeStruct((S5_GROUP, g, n), F32)
    return pl.pallas_call(
        _s5_prep_body, out_shape=(out, out, outb, outb), name="s5_prep",
    )(lam_re, lam_im, log_dt.reshape(g, 1), jnp.transpose(b_re, (2, 0, 1)), jnp.transpose(b_im, (2, 0, 1)))


def _s5_tables(lam_re, lam_im, log_dt, b_re, b_im, c_re, c_im, d, w_glu, b_glu, seg):
    g, n = lam_re.shape
    p = S5_GROUP
    gb = S5_BLOCK_GROUPS
    nb = g // gb
    apr, api, bbr, bbi = _s5_prep(lam_re, lam_im, log_dt, b_re, b_im)
    eye = jnp.eye(gb, dtype=F32)

    def bu_block(bb):
        return jnp.einsum('pbgn,gh->bgphn', bb.reshape(p, nb, gb, n), eye).reshape(nb, gb * p, gb * n)

    def c_block(c):
        return jnp.einsum('bgpn,gh->bgnhp', c.reshape(nb, gb, p, n), eye).reshape(nb, gb * n, gb * p)

    w_bu = jnp.concatenate([bu_block(bbr), bu_block(bbi)], axis=-1).astype(BF16)
    w_c = jnp.concatenate([c_block(c_re), -c_block(c_im)], axis=1).astype(BF16)
    w_g = jnp.einsum('bgpq,gh->bgphq', w_glu.reshape(nb, gb, p, p), eye).reshape(nb, gb * p, gb * p).astype(BF16)
    d_b = d.reshape(nb, 1, gb * p)
    bg_b = b_glu.reshape(nb, 1, gb * p)

    t = jnp.arange(SUBLANES) % seg
    pw_r = apr.reshape(SUBLANES, nb, gb * n)
    pw_i = api.reshape(SUBLANES, nb, gb * n)

    def masked(k):
        m = (t >= k)[:, None, None]
        return jnp.where(m, pw_r[k - 1][None], 0.0), jnp.where(m, pw_i[k - 1][None], 0.0)

    tabs = []
    k = 1
    while k < seg:
        tabs.extend(masked(k))
        k *= 2
    tabs.append(pw_r[t])
    tabs.append(pw_i[t])
    tab = jnp.transpose(jnp.stack(tabs), (2, 0, 1, 3))
    return w_bu, w_c, w_g, d_b, bg_b, tab


def _in_proj_body(x_ref, gain_ref, w_ref, wr_ref, wg_ref, wg2_ref, bg_ref, o_ref, h_ref, *, tm, n_main, n_r):
    j = pl.program_id(1)

    @pl.when(j == 0)
    def _():
        _rms_rows_to_bf16(x_ref, gain_ref, h_ref, tm)

    @pl.when(j < n_main)
    def _():
        o_ref[...] = _dot(h_ref[...], w_ref[...].astype(BF16))

    @pl.when((j >= n_main) & (j < n_main + n_r))
    def _():
        o_ref[...] = _dot(h_ref[...], wr_ref[...].astype(BF16))

    @pl.when(j == n_main + n_r)
    def _():
        g_low = _dot(h_ref[...], wg_ref[...].astype(BF16))
        pre = _dot(g_low.astype(BF16), wg2_ref[...].astype(BF16)) + bg_ref[...]
        log_sig = jnp.minimum(pre, 0.0) - jnp.log1p(jnp.exp(-jnp.abs(pre)))
        o_ref[...] = log_sig * (1.0 / GLA_TAU)


def _in_proj(x, gain, w_in, w_g2, b_g, *, tm, tn=512):
    n, dm = x.shape
    mix = dm
    n_u_q_k_v = mix // 2 + 2 * (mix // 4) + mix // 2
    col_g = n_u_q_k_v
    col_r = n_u_q_k_v + GLA_RANK
    w_r = w_in[:, col_r:]
    w_g = w_in[:, col_g:col_r]
    n_main = n_u_q_k_v // tn
    n_r = w_r.shape[1] // tn
    n_cols = n_u_q_k_v + w_r.shape[1] + w_g2.shape[1]
    body = functools.partial(_in_proj_body, tm=tm, n_main=n_main, n_r=n_r)
    return pl.pallas_call(
        body,
        grid=(n // tm, n_main + n_r + 1),
        in_specs=[
            pl.BlockSpec((tm, dm), lambda i, j: (i, 0)),
            pl.BlockSpec((1, dm), lambda i, j: (0, 0)),
            pl.BlockSpec((dm, tn), lambda i, j: (0, jnp.minimum(j, n_main - 1))),
            pl.BlockSpec((dm, tn), lambda i, j: (0, jnp.clip(j - n_main, 0, n_r - 1))),
            pl.BlockSpec((dm, GLA_RANK), lambda i, j: (0, 0)),
            pl.BlockSpec(w_g2.shape, lambda i, j: (0, 0)),
            pl.BlockSpec((1, w_g2.shape[1]), lambda i, j: (0, 0)),
        ],
        out_specs=pl.BlockSpec((tm, tn), lambda i, j: (i, j)),
        out_shape=jax.ShapeDtypeStruct((n, n_cols), F32),
        scratch_shapes=[pltpu.VMEM((tm, dm), BF16)],
        compiler_params=_params("arbitrary", "arbitrary"),
        name="in_proj",
    )(x, gain.reshape(1, dm), w_in, w_r, w_g, w_g2, b_g.reshape(1, -1))


def _s5_body(*refs, tt, seg, ns):
    if seg == SUBLANES:
        u_ref, wbu_ref, wc_ref, wg_ref, d_ref, bg_ref, tab_ref, z_ref, hr_ref, hi_ref, bu_ref, cr_ref, ci_ref = refs
    else:
        u_ref, wbu_ref, wc_ref, wg_ref, d_ref, bg_ref, tab_ref, h0r_ref, h0i_ref, z_ref, hr_ref, hi_ref, bu_ref = refs
    n_lvl = seg.bit_length() - 1
    bu_ref[...] = _dot(u_ref[...].astype(BF16), wbu_ref[...])

    if seg == SUBLANES:
        @pl.when(pl.program_id(2) == 0)
        def _():
            cr_ref[...] = jnp.zeros_like(cr_ref)
            ci_ref[...] = jnp.zeros_like(ci_ref)

    def block(i, carry):
        r0 = pl.multiple_of(i * SUBLANES, SUBLANES)
        xr = bu_ref[pl.ds(r0, SUBLANES), 0:ns]
        xi = bu_ref[pl.ds(r0, SUBLANES), ns:2 * ns]
        for lvl in range(n_lvl):
            k = 1 << lvl
            ar = tab_ref[2 * lvl]
            ai = tab_ref[2 * lvl + 1]
            sr = pltpu.roll(xr, k, 0)
            si = pltpu.roll(xi, k, 0)
            xr, xi = xr + (ar * sr - ai * si), xi + (ar * si + ai * sr)
        pr = tab_ref[2 * n_lvl]
        pi = tab_ref[2 * n_lvl + 1]
        if seg == SUBLANES:
            c_r, c_i = carry
        else:
            c_r = h0r_ref[pl.ds(r0, SUBLANES), :]
            c_i = h0i_ref[pl.ds(r0, SUBLANES), :]
        hr = xr + (pr * c_r - pi * c_i)
        hi = xi + (pr * c_i + pi * c_r)
        bu_ref[pl.ds(r0, SUBLANES), 0:ns] = hr
        bu_ref[pl.ds(r0, SUBLANES), ns:2 * ns] = hi
        if seg == SUBLANES:
            last = SUBLANES - 1
            return (jnp.broadcast_to(hr[last:last + 1], hr.shape), jnp.broadcast_to(hi[last:last + 1], hi.shape))
        return carry

    if seg == SUBLANES:
        c_r, c_i = lax.fori_loop(0, tt // SUBLANES, block, (cr_ref[...], ci_ref[...]))
        cr_ref[...] = c_r
        ci_ref[...] = c_i
        hr_ref[...] = c_r[0:1]
        hi_ref[...] = c_i[0:1]
    else:
        lax.fori_loop(0, tt // SUBLANES, block, 0)
        hr_ref[...] = bu_ref[:, 0:ns]
        hi_ref[...] = bu_ref[:, ns:2 * ns]

    y = _dot(bu_ref[...].astype(BF16), wc_ref[...]) + d_ref[...] * u_ref[...]
    z = _gelu(y)
    gate = jax.nn.sigmoid(_dot(z.astype(BF16), wg_ref[...]) + bg_ref[...])
    z_ref[...] = z * gate


def _s5_mixer(proj3, tables, *, tt, seg, h0=None):
    w_bu, w_c, w_g, d_b, bg_b, tab = tables
    bsz, seq, _ = proj3.shape
    nb, cw, ns2 = w_bu.shape
    ns = ns2 // 2
    n_tab = tab.shape[1]
    body = functools.partial(_s5_body, tt=tt, seg=seg, ns=ns)
    wspec = lambda shape: pl.BlockSpec((None,) + shape, lambda b, g, t: (g,) + (0,) * len(shape))
    in_specs = [
        pl.BlockSpec((None, tt, cw), lambda b, g, t: (b, t, g)),
        wspec((cw, ns2)), wspec((ns2, cw)), wspec((cw, cw)), wspec((1, cw)), wspec((1, cw)),
        wspec((n_tab, SUBLANES, ns)),
    ]
    args = [proj3, w_bu, w_c, w_g, d_b, bg_b, tab]
    z_shape = jax.ShapeDtypeStruct((bsz, seq, nb * cw), F32)
    z_spec = pl.BlockSpec((None, tt, cw), lambda b, g, t: (b, t, g))
    scratch = [pltpu.VMEM((tt, ns2), F32)]
    if seg == SUBLANES:
        st_shape = jax.ShapeDtypeStruct((bsz, nb, 1, ns), F32)
        st_spec = pl.BlockSpec((None, None, 1, ns), lambda b, g, t: (b, g, 0, 0))
        scratch += [pltpu.VMEM((SUBLANES, ns), F32), pltpu.VMEM((SUBLANES, ns), F32)]
    else:
        h0r, h0i = h0
        in_specs += [pl.BlockSpec((None, tt, ns), lambda b, g, t: (b, t, g))] * 2
        args += [h0r, h0i]
        st_shape = jax.ShapeDtypeStruct((bsz, seq, nb * ns), F32)
        st_spec = pl.BlockSpec((None, tt, ns), lambda b, g, t: (b, t, g))
    return pl.pallas_call(
        body,
        grid=(bsz, nb, seq // tt),
        in_specs=in_specs,
        out_specs=(z_spec, st_spec, st_spec),
        out_shape=(z_shape, st_shape, st_shape),
        scratch_shapes=scratch,
        compiler_params=_params("arbitrary", "arbitrary", "arbitrary"),
        name="s5_seg%d" % seg,
    )(*args)


def _split3_bf16(x):
    hi = x.astype(BF16)
    r1 = x - hi.astype(F32)
    mid = r1.astype(BF16)
    lo = (r1 - mid.astype(F32)).astype(BF16)
    return hi, mid, lo


def _gla_out(o, gain, r):
    return _rms(o, gain) * (r * jax.nn.sigmoid(r))


def _gla_seq_body(q_ref, k_ref, v_ref, la_ref, r_ref, gain_ref, y_ref, st_out_ref, st_ref, *, tt, dk):
    c = GLA_CHUNK

    @pl.when(pl.program_id(2) == 0)
    def _():
        st_ref[...] = jnp.zeros_like(st_ref)

    row = lax.broadcasted_iota(jnp.int32, (tt, tt), 0)
    col = lax.broadcasted_iota(jnp.int32, (tt, tt), 1)
    shift = c.bit_length() - 1
    tri = (col <= row) & ((row >> shift) == (col >> shift))
    tri_b = jnp.where(tri, 1.0, 0.0).astype(BF16)
    hi, mid, lo = _split3_bf16(la_ref[...])
    bcum = _dot(tri_b, hi) + _dot(tri_b, mid) + _dot(tri_b, lo)
    causal = lax.broadcasted_iota(jnp.int32, (c, c), 1) <= lax.broadcasted_iota(jnp.int32, (c, c), 0)
    scale = dk ** -0.5
    for ci in range(tt // c):
        sl = slice(ci * c, (ci + 1) * c)
        bc = bcum[sl]
        bl = bc[c - 1:c]
        qt = ((q_ref[sl, :] * scale) * jnp.exp(bc)).astype(BF16)
        kc = k_ref[sl, :]
        kt = (kc * jnp.exp(-bc)).astype(BF16)
        ke = (kc * jnp.exp(bl - bc)).astype(BF16)
        vb = v_ref[sl, :].astype(BF16)
        s = lax.dot_general(qt, kt, _CONTRACT_LAST, preferred_element_type=F32)
        s = jnp.where(causal, s, 0.0)
        st = st_ref[...]
        o = _dot(s.astype(BF16), vb) + lax.dot_general(qt, st.astype(BF16), _CONTRACT_LAST, preferred_element_type=F32)
        st_ref[...] = st * jnp.exp(bl) + lax.dot_general(vb, ke, _CONTRACT_FIRST, preferred_element_type=F32)
        y_ref[sl, :] = _gla_out(o, gain_ref[...], r_ref[sl, :])
    st_out_ref[...] = st_ref[...]


def _gla_seq(proj3, gain, *, tt):
    bsz, seq, _ = proj3.shape
    hh = GLA_HEADS
    dv = gain.shape[0]
    dk = dv // 2
    col_q = 1024 // dk
    col_k = (1024 + hh * dk) // dk
    col_v = (1024 + 2 * hh * dk) // dv
    col_r = (1024 + 2 * hh * dk + hh * dv) // dv
    col_la = (1024 + 2 * hh * dk + 2 * hh * dv) // dk
    spec = lambda w, c0: pl.BlockSpec((None, tt, w), lambda b, h, t: (b, t, c0 + h))
    body = functools.partial(_gla_seq_body, tt=tt, dk=dk)
    return pl.pallas_call(
        body,
        grid=(bsz, hh, seq // tt),
        in_specs=[spec(dk, col_q), spec(dk, col_k), spec(dv, col_v), spec(dk, col_la), spec(dv, col_r),
                  pl.BlockSpec((1, dv), lambda b, h, t: (0, 0))],
        out_specs=(pl.BlockSpec((None, tt, dv), lambda b, h, t: (b, t, h)),
                   pl.BlockSpec((None, None, dv, dk), lambda b, h, t: (b, h, 0, 0))),
        out_shape=(jax.ShapeDtypeStruct((bsz, seq, hh * dv), F32),
                   jax.ShapeDtypeStruct((bsz, hh, dv, dk), F32)),
        scratch_shapes=[pltpu.VMEM((dv, dk), F32)],
        compiler_params=_params("arbitrary", "arbitrary", "arbitrary"),
        name="gla_seq",
    )(proj3, proj3, proj3, proj3, proj3, gain.reshape(1, dv))


def _gla_step_body(q_ref, k_ref, v_ref, la_ref, r_ref, gain_ref, s_ref, y_ref, so_ref, *, nb, seg, dk, dv):
    hh = GLA_HEADS
    rows = SUBLANES
    per = rows // seg
    row = lax.broadcasted_iota(jnp.int32, (rows, rows), 0)
    col = lax.broadcasted_iota(jnp.int32, (rows, rows), 1)
    shift = seg.bit_length() - 1
    causal = (col <= row) & ((row >> shift) == (col >> shift))
    rid = lax.broadcasted_iota(jnp.int32, (rows, 1), 0)
    t_in = rid & (seg - 1)
    ones_b = jnp.ones((rows, dv), BF16)
    scale = dk ** -0.5

    def pair(i, carry):
        r0 = pl.multiple_of(i * rows, rows)
        for h in range(hh):
            ks = slice(h * dk, (h + 1) * dk)
            vs = slice(h * dv, (h + 1) * dv)
            la = la_ref[pl.ds(r0, rows), ks]
            bc = la
            k = 1
            while k < seg:
                bc = bc + jnp.where(t_in >= k, pltpu.roll(bc, k, 0), 0.0)
                k *= 2
            bl = jnp.zeros_like(bc)
            for e in range(per):
                last = e * seg + seg - 1
                bl = jnp.where((rid >> shift) == e, bc[last:last + 1], bl)
            qt = ((q_ref[pl.ds(r0, rows), ks] * scale) * jnp.exp(bc)).astype(BF16)
            kc = k_ref[pl.ds(r0, rows), ks]
            kt = (kc * jnp.exp(-bc)).astype(BF16)
            ke = kc * jnp.exp(bl - bc)
            vb = v_ref[pl.ds(r0, rows), vs].astype(BF16)
            s = lax.dot_general(qt, kt, _CONTRACT_LAST, preferred_element_type=F32)
            s = jnp.where(causal, s, 0.0)
            o = _dot(s.astype(BF16), vb)
            o_inter = jnp.zeros_like(o)
            for e in range(per):
                mine = (rid >> shift) == e
                st = s_ref[i * per + e, h]
                o_e = _dot(qt, st.astype(BF16))
                o_inter = jnp.where(mine, o_e, o_inter)
                last = e * seg + seg - 1
                dec = jnp.exp(bc[last:last + 1])
                d_hi, d_mid, d_lo = _split3_bf16(dec)
                d3 = jnp.where(rid == 0, d_hi.astype(F32),
                               jnp.where(rid == 1, d_mid.astype(F32),
                                         jnp.where(rid == 2, d_lo.astype(F32), 0.0))).astype(BF16)
                dec_cols = lax.dot_general(d3, ones_b, _CONTRACT_FIRST, preferred_element_type=F32)
                ke_e = jnp.where(mine, ke, 0.0).astype(BF16)
                kv = lax.dot_general(ke_e, vb, _CONTRACT_FIRST, preferred_element_type=F32)
                so_ref[i * per + e, h] = dec_cols * st + kv
            y_ref[pl.ds(r0, rows), vs] = _gla_out(o + o_inter, gain_ref[...], r_ref[pl.ds(r0, rows), vs])
        return carry

    lax.fori_loop(0, nb // per, pair, 0)


def _gla_step(proj, gain, state, *, seg, nb=8):
    n = proj.shape[0]
    bsz, hh, dk, dv = state.shape
    rows = nb * seg
    c_q = 1024 // (hh * dk)
    c_k = c_q + 1
    c_v = (1024 + 2 * hh * dk) // (hh * dv)
    c_r = c_v + 1
    c_la = (1024 + 2 * hh * dk + 2 * hh * dv) // (hh * dk)
    spec = lambda w, c0: pl.BlockSpec((rows, w), lambda i: (i, c0))
    body = functools.partial(_gla_step_body, nb=nb, seg=seg, dk=dk, dv=dv)
    return pl.pallas_call(
        body,
        grid=(bsz // nb,),
        in_specs=[spec(hh * dk, c_q), spec(hh * dk, c_k), spec(hh * dv, c_v), spec(hh * dk, c_la), spec(hh * dv, c_r),
                  pl.BlockSpec((1, dv), lambda i: (0, 0)),
                  pl.BlockSpec((nb, hh, dk, dv), lambda i: (i, 0, 0, 0))],
        out_specs=(pl.BlockSpec((rows, hh * dv), lambda i: (i, 0)),
                   pl.BlockSpec((nb, hh, dk, dv), lambda i: (i, 0, 0, 0))),
        out_shape=(jax.ShapeDtypeStruct((n, hh * dv), F32), jax.ShapeDtypeStruct(state.shape, F32)),
        compiler_params=_params("arbitrary"),
        name="gla_step",
    )(proj, proj, proj, proj, proj, gain.reshape(1, dv), state)


def _tiled_linear(name, prologue, row_ins, aux_ins, aux_specs, w, *, tm, tn, residual=None):
    n = row_ins[0].shape[0]
    k_dim, n_cols = w.shape
    nr, na = len(row_ins), len(aux_ins)
    has_res = residual is not None

    def body(*refs):
        row_refs = refs[:nr]
        aux_refs = refs[nr:nr + na]
        w_ref = refs[nr + na]
        o_ref, h_ref = refs[-2], refs[-1]

        @pl.when(pl.program_id(1) == 0)
        def _():
            prologue(row_refs, aux_refs, h_ref)

        acc = _dot(h_ref[...], w_ref[...].astype(BF16))
        if has_res:
            acc = acc + refs[nr + na + 1][...]
        o_ref[...] = acc

    in_specs = [pl.BlockSpec((tm, a.shape[1]), lambda i, j: (i, 0)) for a in row_ins]
    in_specs += list(aux_specs)
    in_specs.append(pl.BlockSpec((k_dim, tn), lambda i, j: (0, j)))
    args = list(row_ins) + list(aux_ins) + [w]
    if has_res:
        in_specs.append(pl.BlockSpec((tm, tn), lambda i, j: (i, j)))
        args.append(residual)
    return pl.pallas_call(
        body,
        grid=(n // tm, n_cols // tn),
        in_specs=in_specs,
        out_specs=pl.BlockSpec((tm, tn), lambda i, j: (i, j)),
        out_shape=jax.ShapeDtypeStruct((n, n_cols), F32),
        scratch_shapes=[pltpu.VMEM((tm, k_dim), BF16)],
        compiler_params=_params("arbitrary", "arbitrary"),
        name=name,
    )(*args)


def _whole(a):
    return pl.BlockSpec(a.shape, lambda i, j: (0,) * a.ndim)


def _norm_linear(name, x, gain, w, *, tm, tn=512):
    def prologue(row_refs, aux_refs, h_ref):
        _rms_rows_to_bf16(row_refs[0], aux_refs[0], h_ref, tm)
    g = gain.reshape(1, -1)
    return _tiled_linear(name, prologue, [x], [g], [_whole(g)], w, tm=tm, tn=tn)


def _cast_linear(name, x, w, residual, *, tm, tn=512):
    def prologue(row_refs, aux_refs, h_ref):
        def f(r0):
            h_ref[pl.ds(r0, 128), :] = row_refs[0][pl.ds(r0, 128), :].astype(BF16)
        _for_row_chunks(tm, 128, f)
    return _tiled_linear(name, prologue, [x], [], [], w, tm=tm, tn=tn, residual=residual)


def _out_proj(z_s5, y_gla, gain_s5, w_out, x, *, tm, tn=512):
    ws = z_s5.shape[1]

    def prologue(row_refs, aux_refs, h_ref):
        def f(r0):
            h_ref[pl.ds(r0, 128), 0:ws] = _rms(row_refs[0][pl.ds(r0, 128), :], aux_refs[0][...]).astype(BF16)
            h_ref[pl.ds(r0, 128), ws:] = row_refs[1][pl.ds(r0, 128), :].astype(BF16)
        _for_row_chunks(tm, 128, f)
    g = gain_s5.reshape(1, -1)
    return _tiled_linear("out_proj", prologue, [z_s5, y_gla], [g], [_whole(g)], w_out, tm=tm, tn=tn, residual=x)


def _attend(q, kb, vb, scale):
    s = lax.dot_general(q, kb, _CONTRACT_LAST, preferred_element_type=F32) * scale
    e = jnp.exp(s - jnp.max(s, axis=-1, keepdims=True))
    p = e / jnp.sum(e, axis=-1, keepdims=True)
    return _dot(p.astype(BF16), vb)


def _xattn_seq_out(q, mk_b, mv_b, w_xo, x, *, tm, seq, tn=512, chunk=256):
    dm = q.shape[1]
    hd = dm // X_HEADS
    scale = hd ** -0.5
    tiles_per_seq = seq // tm

    def prologue(row_refs, aux_refs, h_ref):
        q_ref = row_refs[0]
        mk_ref, mv_ref = aux_refs
        for h in range(X_HEADS):
            hs = slice(h * hd, (h + 1) * hd)

            def f(r0, hs=hs):
                o = _attend(q_ref[pl.ds(r0, chunk), hs].astype(BF16), mk_ref[:, hs], mv_ref[:, hs], scale)
                h_ref[pl.ds(r0, chunk), hs] = o.astype(BF16)
            _for_row_chunks(tm, chunk, f)

    mspec = pl.BlockSpec((None,) + mk_b.shape[1:], lambda i, j: (i // tiles_per_seq, 0, 0))
    return _tiled_linear("xattn_seq_out", prologue, [q], [mk_b, mv_b], [mspec, mspec], w_xo, tm=tm, tn=tn, residual=x)


def _cache_rows(c):
    b, m, h, hd = c.shape
    n_chunk = hd // LANES
    return c.reshape(b, m, h, n_chunk, LANES).transpose(0, 1, 3, 2, 4).reshape(b * m * n_chunk * h, LANES)


def _xattn_step_body(q_ref, k_ref, v_ref, o_ref, *, nb, seg, hd, n_mem):
    rows = SUBLANES
    per = rows // seg
    shift = seg.bit_length() - 1
    rid = lax.broadcasted_iota(jnp.int32, (rows, 1), 0)
    scale = hd ** -0.5
    n_chunk = hd // LANES
    pitch = n_chunk * X_HEADS

    def head_rows(ref, b, h):
        parts = [ref[pl.ds(b * n_mem * pitch + ck * X_HEADS + h, n_mem, stride=pitch), :] for ck in range(n_chunk)]
        return jnp.concatenate(parts, axis=1).astype(BF16)

    pairs = [(b, h) for b in range(nb) for h in range(X_HEADS)]
    scores = []
    for b, h in pairs:
        r0 = (b // per) * rows
        q = q_ref[r0:r0 + rows, h * hd:(h + 1) * hd].astype(BF16)
        scores.append(lax.dot_general(q, head_rows(k_ref, b, h), _CONTRACT_LAST, preferred_element_type=F32) * scale)
    probs = []
    for s in scores:
        e = jnp.exp(s - jnp.max(s, axis=-1, keepdims=True))
        probs.append((e / jnp.sum(e, axis=-1, keepdims=True)).astype(BF16))
    outs = {}
    for (b, h), p in zip(pairs, probs):
        outs[b, h] = _dot(p, head_rows(v_ref, b, h))
    for g in range(nb // per):
        for h in range(X_HEADS):
            o = outs[g * per, h]
            for e in range(1, per):
                o = jnp.where((rid >> shift) == e, outs[g * per + e, h], o)
            o_ref[g * rows:(g + 1) * rows, h * hd:(h + 1) * hd] = o


def _xattn_step(q, mem_k, mem_v, *, seg, nb=4):
    n, dm = q.shape
    bsz, m, _, hd = mem_k.shape
    body = functools.partial(_xattn_step_body, nb=nb, seg=seg, hd=hd, n_mem=m)
    rows_per_seq = m * dm // LANES
    mspec = pl.BlockSpec((nb * rows_per_seq, LANES), lambda i: (i, 0))
    return pl.pallas_call(
        body,
        grid=(bsz // nb,),
        in_specs=[pl.BlockSpec((nb * seg, dm), lambda i: (i, 0)), mspec, mspec],
        out_specs=pl.BlockSpec((nb * seg, dm), lambda i: (i, 0)),
        out_shape=jax.ShapeDtypeStruct((n, dm), F32),
        compiler_params=_params("arbitrary"),
        name="xattn_step",
    )(q, _cache_rows(mem_k), _cache_rows(mem_v))


def _ffn_finish(conv, gate, wd_ref, o_ref, gfin_ref, j, nj, tm):
    act = (_gelu(conv) * gate).astype(BF16)
    o_ref[...] += _dot(act, wd_ref[...].astype(BF16))

    @pl.when(j == nj - 1)
    def _():
        def f(r0):
            o_ref[pl.ds(r0, 128), :] = _rms(o_ref[pl.ds(r0, 128), :], gfin_ref[...])
        _for_row_chunks(tm, 128, f)


def _ffn_seq_body(x_ref, gain_ref, wa_ref, wg_ref, wd_ref, cw_ref, cb_ref, gfin_ref, o_ref, tail_ref, h_ref, a_ref,
                  *, tm, halo, tiles_per_seq, nj):
    i = pl.program_id(0)
    j = pl.program_id(1)

    @pl.when(j == 0)
    def _():
        @pl.when(i % tiles_per_seq == 0)
        def _():
            h_ref[0:halo, :] = jnp.zeros((halo, h_ref.shape[1]), BF16)

        @pl.when(i % tiles_per_seq != 0)
        def _():
            h_ref[0:halo, :] = h_ref[tm:tm + halo, :]

        _rms_rows_to_bf16(x_ref, gain_ref, h_ref, tm, row0=halo)
        o_ref[...] = x_ref[...]

    a_ref[...] = _dot(h_ref[...], wa_ref[...].astype(BF16))
    gate = _dot(h_ref[halo:halo + tm, :], wg_ref[...].astype(BF16))
    conv = cb_ref[...]
    for tap in range(CONV_W):
        conv = conv + a_ref[pl.ds(halo - (CONV_W - 1) + tap, tm), :] * cw_ref[tap:tap + 1, :]
    tail_ref[...] = a_ref[pl.ds(halo + tm - SUBLANES, SUBLANES), :]
    _ffn_finish(conv, gate, wd_ref, o_ref, gfin_ref, j, nj, tm)


def _ffn_seq(x, gain, w_up, conv_w, conv_b, w_down, gain_final, *, tm, seq, tf=256, halo=16):
    n, dm = x.shape
    dff = w_down.shape[0]
    nj = dff // tf
    body = functools.partial(_ffn_seq_body, tm=tm, halo=halo, tiles_per_seq=seq // tm, nj=nj)
    return pl.pallas_call(
        body,
        grid=(n // tm, nj),
        in_specs=[
            pl.BlockSpec((tm, dm), lambda i, j: (i, 0)),
            pl.BlockSpec((1, dm), lambda i, j: (0, 0)),
            pl.BlockSpec((dm, tf), lambda i, j: (0, j)),
            pl.BlockSpec((dm, tf), lambda i, j: (0, nj + j)),
            pl.BlockSpec((tf, dm), lambda i, j: (j, 0)),
            pl.BlockSpec((CONV_W, tf), lambda i, j: (0, j)),
            pl.BlockSpec((1, tf), lambda i, j: (0, j)),
            pl.BlockSpec((1, dm), lambda i, j: (0, 0)),
        ],
        out_specs=(pl.BlockSpec((tm, dm), lambda i, j: (i, 0)),
                   pl.BlockSpec((None, SUBLANES, tf), lambda i, j: (i, 0, j))),
        out_shape=(jax.ShapeDtypeStruct((n, dm), F32),
                   jax.ShapeDtypeStruct((n // tm, SUBLANES, dff), F32)),
        scratch_shapes=[pltpu.VMEM((halo + tm, dm), BF16), pltpu.VMEM((halo + tm, tf), F32)],
        compiler_params=_params("arbitrary", "arbitrary"),
        name="ffn_seq",
    )(x, gain.reshape(1, dm), w_up, w_up, w_down, conv_w, conv_b.reshape(1, dff), gain_final.reshape(1, dm))


def _ffn_step_body(x_ref, gain_ref, wa_ref, wg_ref, wd_ref, cw_ref, cb_ref, gfin_ref, p1_ref, p2_ref,
                   o_ref, a_out_ref, h_ref, *, tm, seg, nj):
    j = pl.program_id(1)

    @pl.when(j == 0)
    def _():
        _rms_rows_to_bf16(x_ref, gain_ref, h_ref, tm)
        o_ref[...] = x_ref[...]

    a = _dot(h_ref[...], wa_ref[...].astype(BF16))
    gate = _dot(h_ref[...], wg_ref[...].astype(BF16))
    a_out_ref[...] = a
    t_in = lax.broadcasted_iota(jnp.int32, (tm, 1), 0) & (seg - 1)
    prev1 = jnp.where(t_in >= 1, pltpu.roll(a, 1, 0), p1_ref[...])
    prev2 = jnp.where(t_in >= 2, pltpu.roll(a, 2, 0), p2_ref[...])
    conv = cb_ref[...] + prev2 * cw_ref[0:1, :]
    conv = conv + prev1 * cw_ref[1:2, :]
    conv = conv + a * cw_ref[2:3, :]
    _ffn_finish(conv, gate, wd_ref, o_ref, gfin_ref, j, nj, tm)


def _ffn_step(x, gain, w_up, conv_w, conv_b, w_down, gain_final, prev1, prev2, *, seg, tf=256):
    n, dm = x.shape
    dff = w_down.shape[0]
    nj = dff // tf
    tm = n
    body = functools.partial(_ffn_step_body, tm=tm, seg=seg, nj=nj)
    return pl.pallas_call(
        body,
        grid=(1, nj),
        in_specs=[
            pl.BlockSpec((tm, dm), lambda i, j: (i, 0)),
            pl.BlockSpec((1, dm), lambda i, j: (0, 0)),
            pl.BlockSpec((dm, tf), lambda i, j: (0, j)),
            pl.BlockSpec((dm, tf), lambda i, j: (0, nj + j)),
            pl.BlockSpec((tf, dm), lambda i, j: (j, 0)),
            pl.BlockSpec((CONV_W, tf), lambda i, j: (0, j)),
            pl.BlockSpec((1, tf), lambda i, j: (0, j)),
            pl.BlockSpec((1, dm), lambda i, j: (0, 0)),
            pl.BlockSpec((tm, tf), lambda i, j: (i, j)),
            pl.BlockSpec((tm, tf), lambda i, j: (i, j)),
        ],
        out_specs=(pl.BlockSpec((tm, dm), lambda i, j: (i, 0)),
                   pl.BlockSpec((tm, tf), lambda i, j: (i, j))),
        out_shape=(jax.ShapeDtypeStruct((n, dm), F32), jax.ShapeDtypeStruct((n, dff), F32)),
        scratch_shapes=[pltpu.VMEM((tm, dm), BF16)],
        compiler_params=_params("arbitrary", "arbitrary"),
        name="ffn_step",
    )(x, gain.reshape(1, dm), w_up, w_up, w_down, conv_w, conv_b.reshape(1, dff), gain_final.reshape(1, dm),
      prev1, prev2)


def kernel(x_prompt, x_sample, mem_prompt, cache_mem_k, cache_mem_v, state_s5_re, state_s5_im, state_gla, state_conv, norm_mix, w_in, s5_lambda_re, s5_lambda_im, s5_log_dt, s5_b_re, s5_b_im, s5_c_re, s5_c_im, s5_d, s5_w_glu, s5_b_glu, norm_s5_out, gla_w_g2, gla_b_g, norm_gla_out, w_out, norm_xattn, norm_mem, w_xq, w_xk, w_xv, w_xo, norm_ffn, w_up, conv_w, conv_b, w_down, norm_final):
    depth = w_in.shape[0]
    assert depth == 1
    bp, seq, dm = x_prompt.shape
    bs, seg, _ = x_sample.shape
    n_mem = mem_prompt.shape[1]
    dff = w_down.shape[1]
    l = 0
    s5_args = (s5_lambda_re[l], s5_lambda_im[l], s5_log_dt[l], s5_b_re[l], s5_b_im[l], s5_c_re[l], s5_c_im[l],
               s5_d[l], s5_w_glu[l], s5_b_glu[l])
    tm_p = min(1024, seq)
    n_s = bs * seg

    mem2 = mem_prompt.reshape(bp * n_mem, dm)
    mk = _norm_linear("mem_k", mem2, norm_mem[l], w_xk[l], tm=min(1024, bp * n_mem))
    mv = _norm_linear("mem_v", mem2, norm_mem[l], w_xv[l], tm=min(1024, bp * n_mem))

    xp = x_prompt.reshape(bp * seq, dm)
    proj = _in_proj(xp, norm_mix[l], w_in[l], gla_w_g2[l], gla_b_g[l], tm=tm_p)
    proj3 = proj.reshape(bp, seq, -1)
    z_s5, hend_r, hend_i = _s5_mixer(proj3, _s5_tables(*s5_args, seg=SUBLANES), tt=min(512, seq), seg=SUBLANES)
    y_gla, st_gla = _gla_seq(proj3, norm_gla_out[l], tt=min(256, seq))
    x1 = _out_proj(z_s5.reshape(bp * seq, -1), y_gla.reshape(bp * seq, -1), norm_s5_out[l], w_out[l], xp, tm=tm_p)
    q = _norm_linear("xattn_q", x1, norm_xattn[l], w_xq[l], tm=tm_p)
    x2 = _xattn_seq_out(q, mk.reshape(bp, n_mem, dm).astype(BF16), mv.reshape(bp, n_mem, dm).astype(BF16),
                        w_xo[l], x1, tm=tm_p, seq=seq)
    yp, tails = _ffn_seq(x2, norm_ffn[l], w_up[l], conv_w[l], conv_b[l], w_down[l], norm_final, tm=tm_p, seq=seq)

    g_total = s5_lambda_re.shape[1]
    n_state = s5_lambda_re.shape[2]
    p_re = hend_r.reshape(1, bp, g_total, n_state)
    p_im = hend_i.reshape(1, bp, g_total, n_state)
    p_gla = jnp.swapaxes(st_gla, -1, -2)[None]
    tiles_per_seq = seq // tm_p
    p_conv = tails.reshape(bp, tiles_per_seq, SUBLANES, dff)[:, -1, SUBLANES - (CONV_W - 1):, :][None]

    xs = x_sample.reshape(n_s, dm)
    proj_s = _in_proj(xs, norm_mix[l], w_in[l], gla_w_g2[l], gla_b_g[l], tm=n_s)
    h0r = jnp.repeat(state_s5_re[l].reshape(bs, -1), seg, axis=0)[None]
    h0i = jnp.repeat(state_s5_im[l].reshape(bs, -1), seg, axis=0)[None]
    z_s5_s, hs_r, hs_i = _s5_mixer(proj_s[None], _s5_tables(*s5_args, seg=seg), tt=n_s, seg=seg, h0=(h0r, h0i))
    y_gla_s, s_gla = _gla_step(proj_s, norm_gla_out[l], state_gla[l], seg=seg)
    x1s = _out_proj(z_s5_s[0], y_gla_s, norm_s5_out[l], w_out[l], xs, tm=n_s)
    qs = _norm_linear("xattn_q_s", x1s, norm_xattn[l], w_xq[l], tm=n_s)
    att = _xattn_step(qs, cache_mem_k[l], cache_mem_v[l], seg=seg)
    x2s = _cast_linear("xattn_o_s", att, w_xo[l], x1s, tm=n_s)
    buf = state_conv[l]
    zeros = jnp.zeros((bs, dff), F32)
    prev1 = jnp.stack([buf[:, 1]] + [zeros] * (seg - 1), axis=1).reshape(n_s, dff)
    prev2 = jnp.stack([buf[:, 0], buf[:, 1]] + [zeros] * (seg - 2), axis=1).reshape(n_s, dff)
    ys, a_s = _ffn_step(x2s, norm_ffn[l], w_up[l], conv_w[l], conv_b[l], w_down[l], norm_final, prev1, prev2, seg=seg)

    s_re = hs_r.reshape(bs, seg, g_total, n_state)[:, seg - 1][None]
    s_im = hs_i.reshape(bs, seg, g_total, n_state)[:, seg - 1][None]
    s_conv = a_s.reshape(bs, seg, dff)[:, seg - (CONV_W - 1):, :][None]

    hd = dm // X_HEADS
    return (yp.reshape(bp, seq, dm), ys.reshape(bs, seg, dm),
            mk.reshape(1, bp, n_mem, X_HEADS, hd), mv.reshape(1, bp, n_mem, X_HEADS, hd),
            p_re, p_im, p_gla, p_conv, s_re, s_im, s_gla[None], s_conv)
```

```python
import functools

import jax
import jax.numpy as jnp
from jax import lax
from jax.experimental import pallas as pl
from jax.experimental.pallas import tpu as pltpu

F32 = jnp.float32
BF16 = jnp.bfloat16

EPS = 1e-6
S5_GROUP = 16
S5_STATE = 64
S5_BLOCK_GROUPS = 16
GLA_HEADS = 4
GLA_RANK = 16
GLA_TAU = 16.0
GLA_CHUNK = 64
X_HEADS = 4
CONV_W = 3

SUBLANES = 8
LANES = 128
VMEM_LIMIT_BYTES = 56 * 1024 * 1024

_CONTRACT_LAST = (((1,), (1,)), ((), ()))
_CONTRACT_FIRST = (((0,), (0,)), ((), ()))


def _params(*semantics):
    return pltpu.CompilerParams(dimension_semantics=semantics, vmem_limit_bytes=VMEM_LIMIT_BYTES)


def _dot(a, b):
    return jnp.dot(a, b, preferred_element_type=F32)


def _for_row_chunks(n_rows, chunk, fn):
    def body(i, carry):
        fn(pl.multiple_of(i * chunk, chunk))
        return carry
    lax.fori_loop(0, n_rows // chunk, body, 0)


def _rms(x, gain):
    r = lax.rsqrt(jnp.mean(x * x, axis=-1, keepdims=True) + EPS)
    return (x * r) * gain


def _gelu(x):
    return 0.5 * x * (1.0 + lax.erf(x * (2.0 ** -0.5)))


def _rms_rows_to_bf16(x_ref, gain_ref, h_ref, n_rows, row0=0, chunk=128):
    def f(r0):
        h_ref[pl.ds(row0 + r0, chunk), :] = _rms(x_ref[pl.ds(r0, chunk), :], gain_ref[...]).astype(BF16)
    _for_row_chunks(n_rows, chunk, f)


def _s5_prep_body(lr_ref, li_ref, ldt_ref, br_ref, bi_ref, apr_ref, api_ref, bbr_ref, bbi_ref):
    lr = lr_ref[...]
    li = li_ref[...]
    dt = jnp.exp(ldt_ref[...])
    mag = jnp.exp(lr * dt)
    ar = mag * jnp.cos(li * dt)
    ai = mag * jnp.sin(li * dt)
    den = lr * lr + li * li
    pr = ar - 1.0
    qr = (pr * lr + ai * li) / den
    qi = (ai * lr - pr * li) / den
    for p in range(S5_GROUP):
        br = br_ref[p]
        bi = bi_ref[p]
        bbr_ref[p] = qr * br - qi * bi
        bbi_ref[p] = qr * bi + qi * br
    cr, ci = ar, ai
    apr_ref[0] = cr
    api_ref[0] = ci
    for k in range(1, SUBLANES):
        cr, ci = cr * ar - ci * ai, cr * ai + ci * ar
        apr_ref[k] = cr
        api_ref[k] = ci


def _s5_prep(lam_re, lam_im, log_dt, b_re, b_im):
    g, n = lam_re.shape
    out = jax.ShapeDtypeStruct((SUBLANES, g, n), F32)
    outb = jax.ShapeDtypeStruct((S5_GROUP, g, n), F32)
    return pl.pallas_call(
        _s5_prep_body, out_shape=(out, out, outb, outb), name="s5_prep",
    )(lam_re, lam_im, log_dt.reshape(g, 1), jnp.transpose(b_re, (2, 0, 1)), jnp.transpose(b_im, (2, 0, 1)))


def _s5_tables(lam_re, lam_im, log_dt, b_re, b_im, c_re, c_im, d, w_glu, b_glu, seg):
    g, n = lam_re.shape
    p = S5_GROUP
    gb = S5_BLOCK_GROUPS
    nb = g // gb
    apr, api, bbr, bbi = _s5_prep(lam_re, lam_im, log_dt, b_re, b_im)
    eye = jnp.eye(gb, dtype=F32)

    def bu_block(bb):
        return jnp.einsum('pbgn,gh->bgphn', bb.reshape(p, nb, gb, n), eye).reshape(nb, gb * p, gb * n)

    def c_block(c):
        return jnp.einsum('bgpn,gh->bgnhp', c.reshape(nb, gb, p, n), eye).reshape(nb, gb * n, gb * p)

    w_bu = jnp.concatenate([bu_block(bbr), bu_block(bbi)], axis=-1).astype(BF16)
    w_c = jnp.concatenate([c_block(c_re), -c_block(c_im)], axis=1).astype(BF16)
    w_g = jnp.einsum('bgpq,gh->bgphq', w_glu.reshape(nb, gb, p, p), eye).reshape(nb, gb * p, gb * p).astype(BF16)
    d_b = d.reshape(nb, 1, gb * p)
    bg_b = b_glu.reshape(nb, 1, gb * p)

    t = jnp.arange(SUBLANES) % seg
    pw_r = apr.reshape(SUBLANES, nb, gb * n)
    pw_i = api.reshape(SUBLANES, nb, gb * n)

    def masked(k):
        m = (t >= k)[:, None, None]
        return jnp.where(m, pw_r[k - 1][None], 0.0), jnp.where(m, pw_i[k - 1][None], 0.0)

    tabs = []
    k = 1
    while k < seg:
        tabs.extend(masked(k))
        k *= 2
    tabs.append(pw_r[t])
    tabs.append(pw_i[t])
    tab = jnp.transpose(jnp.stack(tabs), (2, 0, 1, 3))
    return w_bu, w_c, w_g, d_b, bg_b, tab


def _in_proj_body(x_ref, gain_ref, w_ref, wr_ref, wg_ref, wg2_ref, bg_ref, o_ref, h_ref, *, tm, n_main, n_r):
    j = pl.program_id(1)

    @pl.when(j == 0)
    def _():
        _rms_rows_to_bf16(x_ref, gain_ref, h_ref, tm)

    @pl.when(j < n_main)
    def _():
        o_ref[...] = _dot(h_ref[...], w_ref[...].astype(BF16))

    @pl.when((j >= n_main) & (j < n_main + n_r))
    def _():
        o_ref[...] = _dot(h_ref[...], wr_ref[...].astype(BF16))

    @pl.when(j == n_main + n_r)
    def _():
        g_low = _dot(h_ref[...], wg_ref[...].astype(BF16))
        pre = _dot(g_low.astype(BF16), wg2_ref[...].astype(BF16)) + bg_ref[...]
        log_sig = jnp.minimum(pre, 0.0) - jnp.log1p(jnp.exp(-jnp.abs(pre)))
        o_ref[...] = log_sig * (1.0 / GLA_TAU)


def _in_proj(x, gain, w_in, w_g2, b_g, *, tm, tn=512):
    n, dm = x.shape
    mix = dm
    n_u_q_k_v = mix // 2 + 2 * (mix // 4) + mix // 2
    col_g = n_u_q_k_v
    col_r = n_u_q_k_v + GLA_RANK
    w_r = w_in[:, col_r:]
    w_g = w_in[:, col_g:col_r]
    n_main = n_u_q_k_v // tn
    n_r = w_r.shape[1] // tn
    n_cols = n_u_q_k_v + w_r.shape[1] + w_g2.shape[1]
    body = functools.partial(_in_proj_body, tm=tm, n_main=n_main, n_r=n_r)
    return pl.pallas_call(
        body,
        grid=(n // tm, n_main + n_r + 1),
        in_specs=[
            pl.BlockSpec((tm, dm), lambda i, j: (i, 0)),
            pl.BlockSpec((1, dm), lambda i, j: (0, 0)),
            pl.BlockSpec((dm, tn), lambda i, j: (0, jnp.minimum(j, n_main - 1))),
            pl.BlockSpec((dm, tn), lambda i, j: (0, jnp.clip(j - n_main, 0, n_r - 1))),
            pl.BlockSpec((dm, GLA_RANK), lambda i, j: (0, 0)),
            pl.BlockSpec(w_g2.shape, lambda i, j: (0, 0)),
            pl.BlockSpec((1, w_g2.shape[1]), lambda i, j: (0, 0)),
        ],
        out_specs=pl.BlockSpec((tm, tn), lambda i, j: (i, j)),
        out_shape=jax.ShapeDtypeStruct((n, n_cols), F32),
        scratch_shapes=[pltpu.VMEM((tm, dm), BF16)],
        compiler_params=_params("arbitrary", "arbitrary"),
        name="in_proj",
    )(x, gain.reshape(1, dm), w_in, w_r, w_g, w_g2, b_g.reshape(1, -1))


def _s5_body(*refs, tt, seg, ns, chunk):
    if seg == SUBLANES:
        u_ref, wbu_ref, wc_ref, wg_ref, d_ref, bg_ref, tab_ref, z_ref, hr_ref, hi_ref, bu_ref, cr_ref, ci_ref = refs
    else:
        u_ref, wbu_ref, wc_ref, wg_ref, d_ref, bg_ref, tab_ref, h0r_ref, h0i_ref, z_ref, hr_ref, hi_ref, bu_ref = refs
    n_lvl = seg.bit_length() - 1
    last = SUBLANES - 1
    lane_group = 2 * LANES

    if seg == SUBLANES:
        @pl.when(pl.program_id(2) == 0)
        def _():
            cr_ref[...] = jnp.zeros_like(cr_ref)
            ci_ref[...] = jnp.zeros_like(ci_ref)

    def project_in(c):
        rows = slice(c * chunk, (c + 1) * chunk)
        bu_ref[rows, :] = _dot(u_ref[rows, :].astype(BF16), wbu_ref[...])

    def scan(c, carry):
        carry = list(carry)
        for blk in range(chunk // SUBLANES):
            rows = slice(c * chunk + blk * SUBLANES, c * chunk + (blk + 1) * SUBLANES)
            for gi in range(ns // lane_group):
                cols = slice(gi * lane_group, (gi + 1) * lane_group)
                cols_im = slice(ns + gi * lane_group, ns + (gi + 1) * lane_group)
                xr = bu_ref[rows, cols]
                xi = bu_ref[rows, cols_im]
                for lvl in range(n_lvl):
                    k = 1 << lvl
                    ar = tab_ref[2 * lvl, :, cols]
                    ai = tab_ref[2 * lvl + 1, :, cols]
                    sr = pltpu.roll(xr, k, 0)
                    si = pltpu.roll(xi, k, 0)
                    xr, xi = xr + (ar * sr - ai * si), xi + (ar * si + ai * sr)
                pr = tab_ref[2 * n_lvl, :, cols]
                pi = tab_ref[2 * n_lvl + 1, :, cols]
                if seg == SUBLANES:
                    c_r, c_i = carry[gi]
                else:
                    c_r = h0r_ref[rows, cols]
                    c_i = h0i_ref[rows, cols]
                hr = xr + (pr * c_r - pi * c_i)
                hi = xi + (pr * c_i + pi * c_r)
                bu_ref[rows, cols] = hr
                bu_ref[rows, cols_im] = hi
                if seg == SUBLANES:
                    carry[gi] = (jnp.broadcast_to(hr[last:last + 1], hr.shape),
                                 jnp.broadcast_to(hi[last:last + 1], hi.shape))
        return carry

    def project_out(c):
        rows = slice(c * chunk, (c + 1) * chunk)
        h = bu_ref[rows, :]
        if seg != SUBLANES:
            hr_ref[rows, :] = h[:, 0:ns]
            hi_ref[rows, :] = h[:, ns:2 * ns]
        y = _dot(h.astype(BF16), wc_ref[...]) + d_ref[...] * u_ref[rows, :]
        z = _gelu(y)
        gate = jax.nn.sigmoid(_dot(z.astype(BF16), wg_ref[...]) + bg_ref[...])
        z_ref[rows, :] = z * gate

    n_chunks = tt // chunk
    n_groups = ns // lane_group
    group_cols = [slice(gi * lane_group, (gi + 1) * lane_group) for gi in range(n_groups)]
    carry = [(cr_ref[:, cols], ci_ref[:, cols]) for cols in group_cols] if seg == SUBLANES else [None] * n_groups
    project_in(0)
    for c in range(n_chunks):
        if c + 1 < n_chunks:
            project_in(c + 1)
        carry = scan(c, carry)
        project_out(c)
    if seg == SUBLANES:
        for cols, (c_r, c_i) in zip(group_cols, carry):
            cr_ref[:, cols] = c_r
            ci_ref[:, cols] = c_i
            hr_ref[:, cols] = c_r[0:1]
            hi_ref[:, cols] = c_i[0:1]


def _s5_mixer(proj3, tables, *, tt, seg, h0=None, chunk=128):
    w_bu, w_c, w_g, d_b, bg_b, tab = tables
    bsz, seq, _ = proj3.shape
    nb, cw, ns2 = w_bu.shape
    ns = ns2 // 2
    n_tab = tab.shape[1]
    body = functools.partial(_s5_body, tt=tt, seg=seg, ns=ns, chunk=min(chunk, tt))
    wspec = lambda shape: pl.BlockSpec((None,) + shape, lambda b, g, t: (g,) + (0,) * len(shape))
    in_specs = [
        pl.BlockSpec((None, tt, cw), lambda b, g, t: (b, t, g)),
        wspec((cw, ns2)), wspec((ns2, cw)), wspec((cw, cw)), wspec((1, cw)), wspec((1, cw)),
        wspec((n_tab, SUBLANES, ns)),
    ]
    args = [proj3, w_bu, w_c, w_g, d_b, bg_b, tab]
    z_shape = jax.ShapeDtypeStruct((bsz, seq, nb * cw), F32)
    z_spec = pl.BlockSpec((None, tt, cw), lambda b, g, t: (b, t, g))
    scratch = [pltpu.VMEM((tt, ns2), F32)]
    if seg == SUBLANES:
        st_shape = jax.ShapeDtypeStruct((bsz, nb, 1, ns), F32)
        st_spec = pl.BlockSpec((None, None, 1, ns), lambda b, g, t: (b, g, 0, 0))
        scratch += [pltpu.VMEM((SUBLANES, ns), F32), pltpu.VMEM((SUBLANES, ns), F32)]
    else:
        h0r, h0i = h0
        in_specs += [pl.BlockSpec((None, tt, ns), lambda b, g, t: (b, t, g))] * 2
        args += [h0r, h0i]
        st_shape = jax.ShapeDtypeStruct((bsz, seq, nb * ns), F32)
        st_spec = pl.BlockSpec((None, tt, ns), lambda b, g, t: (b, t, g))
    return pl.pallas_call(
        body,
        grid=(bsz, nb, seq // tt),
        in_specs=in_specs,
        out_specs=(z_spec, st_spec, st_spec),
        out_shape=(z_shape, st_shape, st_shape),
        scratch_shapes=scratch,
        compiler_params=_params("arbitrary", "arbitrary", "arbitrary"),
        name="s5_seg%d" % seg,
    )(*args)


def _split3_bf16(x):
    hi = x.astype(BF16)
    r1 = x - hi.astype(F32)
    mid = r1.astype(BF16)
    lo = (r1 - mid.astype(F32)).astype(BF16)
    return hi, mid, lo


def _gla_out(o, gain, r):
    return _rms(o, gain) * (r * jax.nn.sigmoid(r))


def _gla_seq_body(q_ref, k_ref, v_ref, la_ref, r_ref, gain_ref, y_ref, st_out_ref, st_ref, *, tt, dk):
    c = GLA_CHUNK

    @pl.when(pl.program_id(2) == 0)
    def _():
        st_ref[...] = jnp.zeros_like(st_ref)

    row = lax.broadcasted_iota(jnp.int32, (tt, tt), 0)
    col = lax.broadcasted_iota(jnp.int32, (tt, tt), 1)
    shift = c.bit_length() - 1
    tri = (col <= row) & ((row >> shift) == (col >> shift))
    tri_b = jnp.where(tri, 1.0, 0.0).astype(BF16)
    hi, mid, lo = _split3_bf16(la_ref[...])
    bcum = _dot(tri_b, hi) + _dot(tri_b, mid) + _dot(tri_b, lo)
    causal = lax.broadcasted_iota(jnp.int32, (c, c), 1) <= lax.broadcasted_iota(jnp.int32, (c, c), 0)
    scale = dk ** -0.5
    chunks = [slice(ci * c, (ci + 1) * c) for ci in range(tt // c)]
    qts, decays, o_intra, kvs = [], [], [], []
    for sl in chunks:
        bc = bcum[sl]
        bl = bc[c - 1:c]
        qt = ((q_ref[sl, :] * scale) * jnp.exp(bc)).astype(BF16)
        kc = k_ref[sl, :]
        kt = (kc * jnp.exp(-bc)).astype(BF16)
        ke = (kc * jnp.exp(bl - bc)).astype(BF16)
        vb = v_ref[sl, :].astype(BF16)
        s = lax.dot_general(qt, kt, _CONTRACT_LAST, preferred_element_type=F32)
        s = jnp.where(causal, s, 0.0)
        qts.append(qt)
        decays.append(jnp.exp(bl))
        o_intra.append(_dot(s.astype(BF16), vb))
        kvs.append(lax.dot_general(vb, ke, _CONTRACT_FIRST, preferred_element_type=F32))
    st = st_ref[...]
    for sl, qt, dec, oi, kv in zip(chunks, qts, decays, o_intra, kvs):
        o = oi + lax.dot_general(qt, st.astype(BF16), _CONTRACT_LAST, preferred_element_type=F32)
        st = st * dec + kv
        y_ref[sl, :] = _gla_out(o, gain_ref[...], r_ref[sl, :]).astype(y_ref.dtype)
    st_ref[...] = st
    st_out_ref[...] = st


def _gla_seq(proj3, gain, *, tt):
    bsz, seq, _ = proj3.shape
    hh = GLA_HEADS
    dv = gain.shape[0]
    dk = dv // 2
    col_q = 1024 // dk
    col_k = (1024 + hh * dk) // dk
    col_v = (1024 + 2 * hh * dk) // dv
    col_r = (1024 + 2 * hh * dk + hh * dv) // dv
    col_la = (1024 + 2 * hh * dk + 2 * hh * dv) // dk
    spec = lambda w, c0: pl.BlockSpec((None, tt, w), lambda b, h, t: (b, t, c0 + h))
    body = functools.partial(_gla_seq_body, tt=tt, dk=dk)
    return pl.pallas_call(
        body,
        grid=(bsz, hh, seq // tt),
        in_specs=[spec(dk, col_q), spec(dk, col_k), spec(dv, col_v), spec(dk, col_la), spec(dv, col_r),
                  pl.BlockSpec((1, dv), lambda b, h, t: (0, 0))],
        out_specs=(pl.BlockSpec((None, tt, dv), lambda b, h, t: (b, t, h)),
                   pl.BlockSpec((None, None, dv, dk), lambda b, h, t: (b, h, 0, 0))),
        out_shape=(jax.ShapeDtypeStruct((bsz, seq, hh * dv), BF16),
                   jax.ShapeDtypeStruct((bsz, hh, dv, dk), F32)),
        scratch_shapes=[pltpu.VMEM((dv, dk), F32)],
        compiler_params=_params("arbitrary", "arbitrary", "arbitrary"),
        name="gla_seq",
    )(proj3, proj3, proj3, proj3, proj3, gain.reshape(1, dv))


def _gla_step_body(q_ref, k_ref, v_ref, la_ref, r_ref, gain_ref, s_ref, y_ref, so_ref, *, nb, seg, dk, dv):
    hh = GLA_HEADS
    rows = SUBLANES
    per = rows // seg
    row = lax.broadcasted_iota(jnp.int32, (rows, rows), 0)
    col = lax.broadcasted_iota(jnp.int32, (rows, rows), 1)
    shift = seg.bit_length() - 1
    causal = (col <= row) & ((row >> shift) == (col >> shift))
    rid = lax.broadcasted_iota(jnp.int32, (rows, 1), 0)
    t_in = rid & (seg - 1)
    ones_b = jnp.ones((rows, dv), BF16)
    scale = dk ** -0.5

    def pair(i, carry):
        r0 = pl.multiple_of(i * rows, rows)
        for h in range(hh):
            ks = slice(h * dk, (h + 1) * dk)
            vs = slice(h * dv, (h + 1) * dv)
            la = la_ref[pl.ds(r0, rows), ks]
            bc = la
            k = 1
            while k < seg:
                bc = bc + jnp.where(t_in >= k, pltpu.roll(bc, k, 0), 0.0)
                k *= 2
            bl = jnp.zeros_like(bc)
            for e in range(per):
                last = e * seg + seg - 1
                bl = jnp.where((rid >> shift) == e, bc[last:last + 1], bl)
            qt = ((q_ref[pl.ds(r0, rows), ks] * scale) * jnp.exp(bc)).astype(BF16)
            kc = k_ref[pl.ds(r0, rows), ks]
            kt = (kc * jnp.exp(-bc)).astype(BF16)
            ke = kc * jnp.exp(bl - bc)
            vb = v_ref[pl.ds(r0, rows), vs].astype(BF16)
            s = lax.dot_general(qt, kt, _CONTRACT_LAST, preferred_element_type=F32)
            s = jnp.where(causal, s, 0.0)
            o = _dot(s.astype(BF16), vb)
            o_inter = jnp.zeros_like(o)
            for e in range(per):
                mine = (rid >> shift) == e
                st = s_ref[i * per + e, h]
                o_e = _dot(qt, st.astype(BF16))
                o_inter = jnp.where(mine, o_e, o_inter)
                last = e * seg + seg - 1
                dec = jnp.exp(bc[last:last + 1])
                d_hi, d_mid, d_lo = _split3_bf16(dec)
                d3 = jnp.where(rid == 0, d_hi.astype(F32),
                               jnp.where(rid == 1, d_mid.astype(F32),
                                         jnp.where(rid == 2, d_lo.astype(F32), 0.0))).astype(BF16)
                dec_cols = lax.dot_general(d3, ones_b, _CONTRACT_FIRST, preferred_element_type=F32)
                ke_e = jnp.where(mine, ke, 0.0).astype(BF16)
                kv = lax.dot_general(ke_e, vb, _CONTRACT_FIRST, preferred_element_type=F32)
                so_ref[i * per + e, h] = dec_cols * st + kv
            y_ref[pl.ds(r0, rows), vs] = _gla_out(o + o_inter, gain_ref[...], r_ref[pl.ds(r0, rows), vs])
        return carry

    lax.fori_loop(0, nb // per, pair, 0)


def _gla_step(proj, gain, state, *, seg, nb=8):
    n = proj.shape[0]
    bsz, hh, dk, dv = state.shape
    rows = nb * seg
    c_q = 1024 // (hh * dk)
    c_k = c_q + 1
    c_v = (1024 + 2 * hh * dk) // (hh * dv)
    c_r = c_v + 1
    c_la = (1024 + 2 * hh * dk + 2 * hh * dv) // (hh * dk)
    spec = lambda w, c0: pl.BlockSpec((rows, w), lambda i: (i, c0))
    body = functools.partial(_gla_step_body, nb=nb, seg=seg, dk=dk, dv=dv)
    return pl.pallas_call(
        body,
        grid=(bsz // nb,),
        in_specs=[spec(hh * dk, c_q), spec(hh * dk, c_k), spec(hh * dv, c_v), spec(hh * dk, c_la), spec(hh * dv, c_r),
                  pl.BlockSpec((1, dv), lambda i: (0, 0)),
                  pl.BlockSpec((nb, hh, dk, dv), lambda i: (i, 0, 0, 0))],
        out_specs=(pl.BlockSpec((rows, hh * dv), lambda i: (i, 0)),
                   pl.BlockSpec((nb, hh, dk, dv), lambda i: (i, 0, 0, 0))),
        out_shape=(jax.ShapeDtypeStruct((n, hh * dv), F32), jax.ShapeDtypeStruct(state.shape, F32)),
        compiler_params=_params("arbitrary"),
        name="gla_step",
    )(proj, proj, proj, proj, proj, gain.reshape(1, dv), state)


def _tiled_linear(name, prologue, row_ins, aux_ins, aux_specs, w, *, tm, tn, residual=None):
    n = row_ins[0].shape[0]
    k_dim, n_cols = w.shape
    nr, na = len(row_ins), len(aux_ins)
    has_res = residual is not None

    def body(*refs):
        row_refs = refs[:nr]
        aux_refs = refs[nr:nr + na]
        w_ref = refs[nr + na]
        o_ref, h_ref = refs[-2], refs[-1]

        @pl.when(pl.program_id(1) == 0)
        def _():
            prologue(row_refs, aux_refs, h_ref)

        acc = _dot(h_ref[...], w_ref[...].astype(BF16))
        if has_res:
            acc = acc + refs[nr + na + 1][...]
        o_ref[...] = acc

    in_specs = [pl.BlockSpec((tm, a.shape[1]), lambda i, j: (i, 0)) for a in row_ins]
    in_specs += list(aux_specs)
    in_specs.append(pl.BlockSpec((k_dim, tn), lambda i, j: (0, j)))
    args = list(row_ins) + list(aux_ins) + [w]
    if has_res:
        in_specs.append(pl.BlockSpec((tm, tn), lambda i, j: (i, j)))
        args.append(residual)
    return pl.pallas_call(
        body,
        grid=(n // tm, n_cols // tn),
        in_specs=in_specs,
        out_specs=pl.BlockSpec((tm, tn), lambda i, j: (i, j)),
        out_shape=jax.ShapeDtypeStruct((n, n_cols), F32),
        scratch_shapes=[pltpu.VMEM((tm, k_dim), BF16)],
        compiler_params=_params("arbitrary", "arbitrary"),
        name=name,
    )(*args)


def _whole(a):
    return pl.BlockSpec(a.shape, lambda i, j: (0,) * a.ndim)


def _norm_linear(name, x, gain, w, *, tm, tn=512):
    def prologue(row_refs, aux_refs, h_ref):
        _rms_rows_to_bf16(row_refs[0], aux_refs[0], h_ref, tm)
    g = gain.reshape(1, -1)
    return _tiled_linear(name, prologue, [x], [g], [_whole(g)], w, tm=tm, tn=tn)


def _cast_linear(name, x, w, residual, *, tm, tn=512):
    def prologue(row_refs, aux_refs, h_ref):
        def f(r0):
            h_ref[pl.ds(r0, 128), :] = row_refs[0][pl.ds(r0, 128), :].astype(BF16)
        _for_row_chunks(tm, 128, f)
    return _tiled_linear(name, prologue, [x], [], [], w, tm=tm, tn=tn, residual=residual)


def _out_proj(z_s5, y_gla, gain_s5, w_out, x, *, tm, tn=512):
    ws = z_s5.shape[1]

    def prologue(row_refs, aux_refs, h_ref):
        def f(r0):
            h_ref[pl.ds(r0, 128), 0:ws] = _rms(row_refs[0][pl.ds(r0, 128), :], aux_refs[0][...]).astype(BF16)
            h_ref[pl.ds(r0, 128), ws:] = row_refs[1][pl.ds(r0, 128), :].astype(BF16)
        _for_row_chunks(tm, 128, f)
    g = gain_s5.reshape(1, -1)
    return _tiled_linear("out_proj", prologue, [z_s5, y_gla], [g], [_whole(g)], w_out, tm=tm, tn=tn, residual=x)


def _attend(q, kb, vb, scale):
    s = lax.dot_general(q, kb, _CONTRACT_LAST, preferred_element_type=F32) * scale
    e = jnp.exp(s - jnp.max(s, axis=-1, keepdims=True))
    p = e / jnp.sum(e, axis=-1, keepdims=True)
    return _dot(p.astype(BF16), vb)


def _mix_xattn_seq_body(z_ref, y_ref, xres_ref, gs5_ref, gx_ref, wout_ref, wq_ref, wo_ref, mk_ref, mv_ref,
                        o_ref, x1_ref, h_ref, q_ref, *, tm, tn, nt, hd, chunk):
    j = pl.program_id(1)
    ws = z_ref.shape[1]
    scale = hd ** -0.5

    @pl.when(j == 0)
    def _():
        def f(r0):
            h_ref[pl.ds(r0, 128), 0:ws] = _rms(z_ref[pl.ds(r0, 128), :], gs5_ref[...]).astype(BF16)
            h_ref[pl.ds(r0, 128), ws:] = y_ref[pl.ds(r0, 128), :].astype(BF16)
        _for_row_chunks(tm, 128, f)

    @pl.when(j == nt)
    def _():
        _rms_rows_to_bf16(x1_ref, gx_ref, h_ref, tm)

    @pl.when(j == 2 * nt)
    def _():
        for h in range(X_HEADS):
            hs = slice(h * hd, (h + 1) * hd)

            def f(r0, hs=hs):
                o = _attend(q_ref[pl.ds(r0, chunk), hs], mk_ref[:, hs], mv_ref[:, hs], scale)
                h_ref[pl.ds(r0, chunk), hs] = o.astype(BF16)
            _for_row_chunks(tm, chunk, f)

    for jj in range(nt):
        cols = slice(jj * tn, (jj + 1) * tn)

        @pl.when(j == jj)
        def _(cols=cols):
            x1_ref[:, cols] = _dot(h_ref[...], wout_ref[...]) + xres_ref[...]

        @pl.when(j == nt + jj)
        def _(cols=cols):
            q_ref[:, cols] = _dot(h_ref[...], wq_ref[...]).astype(BF16)

        @pl.when(j == 2 * nt + jj)
        def _(cols=cols):
            o_ref[...] = _dot(h_ref[...], wo_ref[...]) + x1_ref[:, cols]


def _mix_xattn_seq(z_s5, y_gla, x, gain_s5, gain_x, w_out, w_xq, w_xo, mk_b, mv_b, *, tm, seq, tn=512, chunk=256):
    n, dm = x.shape
    nt = dm // tn
    tiles_per_seq = seq // tm
    body = functools.partial(_mix_xattn_seq_body, tm=tm, tn=tn, nt=nt, hd=dm // X_HEADS, chunk=chunk)
    gs = gain_s5.reshape(1, -1)
    gx = gain_x.reshape(1, -1)
    phase = lambda p: (lambda i, j: (0, jnp.clip(j - p * nt, 0, nt - 1)))
    mspec = pl.BlockSpec((None,) + mk_b.shape[1:], lambda i, j: (i // tiles_per_seq, 0, 0))
    return pl.pallas_call(
        body,
        grid=(n // tm, 3 * nt),
        in_specs=[
            pl.BlockSpec((tm, z_s5.shape[1]), lambda i, j: (i, 0)),
            pl.BlockSpec((tm, y_gla.shape[1]), lambda i, j: (i, 0)),
            pl.BlockSpec((tm, tn), lambda i, j: (i, jnp.minimum(j, nt - 1))),
            _whole(gs), _whole(gx),
            pl.BlockSpec((dm, tn), phase(0)), pl.BlockSpec((dm, tn), phase(1)), pl.BlockSpec((dm, tn), phase(2)),
            mspec, mspec,
        ],
        out_specs=pl.BlockSpec((tm, tn), lambda i, j: (i, jnp.clip(j - 2 * nt, 0, nt - 1))),
        out_shape=jax.ShapeDtypeStruct((n, dm), F32),
        scratch_shapes=[pltpu.VMEM((tm, dm), F32), pltpu.VMEM((tm, dm), BF16), pltpu.VMEM((tm, dm), BF16)],
        compiler_params=_params("arbitrary", "arbitrary"),
        name="mix_xattn_seq",
    )(z_s5, y_gla, x, gs, gx, w_out, w_xq, w_xo, mk_b, mv_b)


def _cache_rows(c):
    b, m, h, hd = c.shape
    n_chunk = hd // LANES
    return c.reshape(b, m, h, n_chunk, LANES).transpose(0, 1, 3, 2, 4).reshape(b * m * n_chunk * h, LANES)


def _xattn_step_body(q_ref, k_ref, v_ref, o_ref, *, nb, seg, hd, n_mem):
    rows = SUBLANES
    per = rows // seg
    shift = seg.bit_length() - 1
    rid = lax.broadcasted_iota(jnp.int32, (rows, 1), 0)
    scale = hd ** -0.5
    n_chunk = hd // LANES
    pitch = n_chunk * X_HEADS

    def head_rows(ref, b, h):
        parts = [ref[pl.ds(b * n_mem * pitch + ck * X_HEADS + h, n_mem, stride=pitch), :] for ck in range(n_chunk)]
        return jnp.concatenate(parts, axis=1).astype(BF16)

    pairs = [(b, h) for b in range(nb) for h in range(X_HEADS)]
    scores = []
    for b, h in pairs:
        r0 = (b // per) * rows
        q = q_ref[r0:r0 + rows, h * hd:(h + 1) * hd].astype(BF16)
        scores.append(lax.dot_general(q, head_rows(k_ref, b, h), _CONTRACT_LAST, preferred_element_type=F32) * scale)
    probs = []
    for s in scores:
        e = jnp.exp(s - jnp.max(s, axis=-1, keepdims=True))
        probs.append((e / jnp.sum(e, axis=-1, keepdims=True)).astype(BF16))
    outs = {}
    for (b, h), p in zip(pairs, probs):
        outs[b, h] = _dot(p, head_rows(v_ref, b, h))
    for g in range(nb // per):
        for h in range(X_HEADS):
            o = outs[g * per, h]
            for e in range(1, per):
                o = jnp.where((rid >> shift) == e, outs[g * per + e, h], o)
            o_ref[g * rows:(g + 1) * rows, h * hd:(h + 1) * hd] = o


def _xattn_step(q, mem_k, mem_v, *, seg, nb=4):
    n, dm = q.shape
    bsz, m, _, hd = mem_k.shape
    body = functools.partial(_xattn_step_body, nb=nb, seg=seg, hd=hd, n_mem=m)
    rows_per_seq = m * dm // LANES
    mspec = pl.BlockSpec((nb * rows_per_seq, LANES), lambda i: (i, 0))
    return pl.pallas_call(
        body,
        grid=(bsz // nb,),
        in_specs=[pl.BlockSpec((nb * seg, dm), lambda i: (i, 0)), mspec, mspec],
        out_specs=pl.BlockSpec((nb * seg, dm), lambda i: (i, 0)),
        out_shape=jax.ShapeDtypeStruct((n, dm), F32),
        compiler_params=_params("arbitrary"),
        name="xattn_step",
    )(q, _cache_rows(mem_k), _cache_rows(mem_v))


def _ffn_finish(conv, gate, wd_ref, o_ref, gfin_ref, j, nj, tm):
    act = (_gelu(conv) * gate).astype(BF16)
    o_ref[...] += _dot(act, wd_ref[...].astype(BF16))

    @pl.when(j == nj - 1)
    def _():
        def f(r0):
            o_ref[pl.ds(r0, 128), :] = _rms(o_ref[pl.ds(r0, 128), :], gfin_ref[...])
        _for_row_chunks(tm, 128, f)


def _ffn_seq_body(x_ref, gain_ref, wa_ref, wg_ref, wd_ref, cw_ref, cb_ref, gfin_ref, o_ref, tail_ref, h_ref, a_ref,
                  *, tm, halo, tiles_per_seq, nj):
    i = pl.program_id(0)
    j = pl.program_id(1)

    @pl.when(j == 0)
    def _():
        @pl.when(i % tiles_per_seq == 0)
        def _():
            h_ref[0:halo, :] = jnp.zeros((halo, h_ref.shape[1]), BF16)

        @pl.when(i % tiles_per_seq != 0)
        def _():
            h_ref[0:halo, :] = h_ref[tm:tm + halo, :]

        _rms_rows_to_bf16(x_ref, gain_ref, h_ref, tm, row0=halo)
        o_ref[...] = x_ref[...]

    a_ref[...] = _dot(h_ref[...], wa_ref[...].astype(BF16))
    gate = _dot(h_ref[halo:halo + tm, :], wg_ref[...].astype(BF16))
    conv = cb_ref[...]
    for tap in range(CONV_W):
        conv = conv + a_ref[pl.ds(halo - (CONV_W - 1) + tap, tm), :] * cw_ref[tap:tap + 1, :]
    tail_ref[...] = a_ref[pl.ds(halo + tm - SUBLANES, SUBLANES), :]
    _ffn_finish(conv, gate, wd_ref, o_ref, gfin_ref, j, nj, tm)


def _ffn_seq(x, gain, w_up, conv_w, conv_b, w_down, gain_final, *, tm, seq, tf=512, halo=16):
    n, dm = x.shape
    dff = w_down.shape[0]
    nj = dff // tf
    body = functools.partial(_ffn_seq_body, tm=tm, halo=halo, tiles_per_seq=seq // tm, nj=nj)
    return pl.pallas_call(
        body,
        grid=(n // tm, nj),
        in_specs=[
            pl.BlockSpec((tm, dm), lambda i, j: (i, 0)),
            pl.BlockSpec((1, dm), lambda i, j: (0, 0)),
            pl.BlockSpec((dm, tf), lambda i, j: (0, j)),
            pl.BlockSpec((dm, tf), lambda i, j: (0, nj + j)),
            pl.BlockSpec((tf, dm), lambda i, j: (j, 0)),
            pl.BlockSpec((CONV_W, tf), lambda i, j: (0, j)),
            pl.BlockSpec((1, tf), lambda i, j: (0, j)),
            pl.BlockSpec((1, dm), lambda i, j: (0, 0)),
        ],
        out_specs=(pl.BlockSpec((tm, dm), lambda i, j: (i, 0)),
                   pl.BlockSpec((None, SUBLANES, tf), lambda i, j: (i, 0, j))),
        out_shape=(jax.ShapeDtypeStruct((n, dm), F32),
                   jax.ShapeDtypeStruct((n // tm, SUBLANES, dff), F32)),
        scratch_shapes=[pltpu.VMEM((halo + tm, dm), BF16), pltpu.VMEM((halo + tm, tf), F32)],
        compiler_params=_params("arbitrary", "arbitrary"),
        name="ffn_seq",
    )(x, gain.reshape(1, dm), w_up, w_up, w_down, conv_w, conv_b.reshape(1, dff), gain_final.reshape(1, dm))


def _ffn_step_body(x_ref, gain_ref, wa_ref, wg_ref, wd_ref, cw_ref, cb_ref, gfin_ref, p1_ref, p2_ref,
                   o_ref, a_out_ref, h_ref, *, tm, seg, nj):
    j = pl.program_id(1)

    @pl.when(j == 0)
    def _():
        _rms_rows_to_bf16(x_ref, gain_ref, h_ref, tm)
        o_ref[...] = x_ref[...]

    a = _dot(h_ref[...], wa_ref[...].astype(BF16))
    gate = _dot(h_ref[...], wg_ref[...].astype(BF16))
    a_out_ref[...] = a
    t_in = lax.broadcasted_iota(jnp.int32, (tm, 1), 0) & (seg - 1)
    prev1 = jnp.where(t_in >= 1, pltpu.roll(a, 1, 0), p1_ref[...])
    prev2 = jnp.where(t_in >= 2, pltpu.roll(a, 2, 0), p2_ref[...])
    conv = cb_ref[...] + prev2 * cw_ref[0:1, :]
    conv = conv + prev1 * cw_ref[1:2, :]
    conv = conv + a * cw_ref[2:3, :]
    _ffn_finish(conv, gate, wd_ref, o_ref, gfin_ref, j, nj, tm)


def _ffn_step(x, gain, w_up, conv_w, conv_b, w_down, gain_final, prev1, prev2, *, seg, tf=512):
    n, dm = x.shape
    dff = w_down.shape[0]
    nj = dff // tf
    tm = n
    body = functools.partial(_ffn_step_body, tm=tm, seg=seg, nj=nj)
    return pl.pallas_call(
        body,
        grid=(1, nj),
        in_specs=[
            pl.BlockSpec((tm, dm), lambda i, j: (i, 0)),
            pl.BlockSpec((1, dm), lambda i, j: (0, 0)),
            pl.BlockSpec((dm, tf), lambda i, j: (0, j)),
            pl.BlockSpec((dm, tf), lambda i, j: (0, nj + j)),
            pl.BlockSpec((tf, dm), lambda i, j: (j, 0)),
            pl.BlockSpec((CONV_W, tf), lambda i, j: (0, j)),
            pl.BlockSpec((1, tf), lambda i, j: (0, j)),
            pl.BlockSpec((1, dm), lambda i, j: (0, 0)),
            pl.BlockSpec((tm, tf), lambda i, j: (i, j)),
            pl.BlockSpec((tm, tf), lambda i, j: (i, j)),
        ],
        out_specs=(pl.BlockSpec((tm, dm), lambda i, j: (i, 0)),
                   pl.BlockSpec((tm, tf), lambda i, j: (i, j))),
        out_shape=(jax.ShapeDtypeStruct((n, dm), F32), jax.ShapeDtypeStruct((n, dff), F32)),
        scratch_shapes=[pltpu.VMEM((tm, dm), BF16)],
        compiler_params=_params("arbitrary", "arbitrary"),
        name="ffn_step",
    )(x, gain.reshape(1, dm), w_up, w_up, w_down, conv_w, conv_b.reshape(1, dff), gain_final.reshape(1, dm),
      prev1, prev2)


def kernel(x_prompt, x_sample, mem_prompt, cache_mem_k, cache_mem_v, state_s5_re, state_s5_im, state_gla, state_conv, norm_mix, w_in, s5_lambda_re, s5_lambda_im, s5_log_dt, s5_b_re, s5_b_im, s5_c_re, s5_c_im, s5_d, s5_w_glu, s5_b_glu, norm_s5_out, gla_w_g2, gla_b_g, norm_gla_out, w_out, norm_xattn, norm_mem, w_xq, w_xk, w_xv, w_xo, norm_ffn, w_up, conv_w, conv_b, w_down, norm_final):
    depth = w_in.shape[0]
    assert depth == 1
    bp, seq, dm = x_prompt.shape
    bs, seg, _ = x_sample.shape
    n_mem = mem_prompt.shape[1]
    dff = w_down.shape[1]
    l = 0
    s5_args = (s5_lambda_re[l], s5_lambda_im[l], s5_log_dt[l], s5_b_re[l], s5_b_im[l], s5_c_re[l], s5_c_im[l],
               s5_d[l], s5_w_glu[l], s5_b_glu[l])
    tm_p = min(1024, seq)
    n_s = bs * seg
    w_in, w_out, w_xq, w_xk, w_xv, w_xo, w_up, w_down = (
        w.astype(BF16) for w in (w_in, w_out, w_xq, w_xk, w_xv, w_xo, w_up, w_down))

    mem2 = mem_prompt.reshape(bp * n_mem, dm)
    mk = _norm_linear("mem_k", mem2, norm_mem[l], w_xk[l], tm=min(1024, bp * n_mem))
    mv = _norm_linear("mem_v", mem2, norm_mem[l], w_xv[l], tm=min(1024, bp * n_mem))

    xp = x_prompt.reshape(bp * seq, dm)
    proj = _in_proj(xp, norm_mix[l], w_in[l], gla_w_g2[l], gla_b_g[l], tm=tm_p)
    proj3 = proj.reshape(bp, seq, -1)
    z_s5, hend_r, hend_i = _s5_mixer(proj3, _s5_tables(*s5_args, seg=SUBLANES), tt=min(512, seq), seg=SUBLANES)
    y_gla, st_gla = _gla_seq(proj3, norm_gla_out[l], tt=min(512, seq))
    x2 = _mix_xattn_seq(z_s5.reshape(bp * seq, -1), y_gla.reshape(bp * seq, -1), xp, norm_s5_out[l], norm_xattn[l],
                        w_out[l], w_xq[l], w_xo[l], mk.reshape(bp, n_mem, dm).astype(BF16),
                        mv.reshape(bp, n_mem, dm).astype(BF16), tm=tm_p, seq=seq)
    yp, tails = _ffn_seq(x2, norm_ffn[l], w_up[l], conv_w[l], conv_b[l], w_down[l], norm_final, tm=tm_p, seq=seq)

    g_total = s5_lambda_re.shape[1]
    n_state = s5_lambda_re.shape[2]
    p_re = hend_r.reshape(1, bp, g_total, n_state)
    p_im = hend_i.reshape(1, bp, g_total, n_state)
    p_gla = jnp.swapaxes(st_gla, -1, -2)[None]
    tiles_per_seq = seq // tm_p
    p_conv = tails.reshape(bp, tiles_per_seq, SUBLANES, dff)[:, -1, SUBLANES - (CONV_W - 1):, :][None]

    xs = x_sample.reshape(n_s, dm)
    proj_s = _in_proj(xs, norm_mix[l], w_in[l], gla_w_g2[l], gla_b_g[l], tm=n_s)
    h0r = jnp.repeat(state_s5_re[l].reshape(bs, -1), seg, axis=0)[None]
    h0i = jnp.repeat(state_s5_im[l].reshape(bs, -1), seg, axis=0)[None]
    z_s5_s, hs_r, hs_i = _s5_mixer(proj_s[None], _s5_tables(*s5_args, seg=seg), tt=n_s, seg=seg, h0=(h0r, h0i))
    y_gla_s, s_gla = _gla_step(proj_s, norm_gla_out[l], state_gla[l], seg=seg)
    x1s = _out_proj(z_s5_s[0], y_gla_s, norm_s5_out[l], w_out[l], xs, tm=n_s)
    qs = _norm_linear("xattn_q_s", x1s, norm_xattn[l], w_xq[l], tm=n_s)
    att = _xattn_step(qs, cache_mem_k[l], cache_mem_v[l], seg=seg)
    x2s = _cast_linear("xattn_o_s", att, w_xo[l], x1s, tm=n_s)
    buf = state_conv[l]

    def at_step(a, t):
        return lax.pad(a, jnp.zeros((), a.dtype), [(t, seg - 1 - t, seg - 1), (0, 0, 0)])
    prev1 = at_step(buf[:, 1], 0)
    prev2 = at_step(buf[:, 0], 0) + at_step(buf[:, 1], 1)
    ys, a_s = _ffn_step(x2s, norm_ffn[l], w_up[l], conv_w[l], conv_b[l], w_down[l], norm_final, prev1, prev2, seg=seg)

    s_re = hs_r[0, seg - 1::seg].reshape(1, bs, g_total, n_state)
    s_im = hs_i[0, seg - 1::seg].reshape(1, bs, g_total, n_state)
    s_conv = jnp.stack([a_s[t::seg] for t in range(seg - (CONV_W - 1), seg)], axis=1)[None]

    hd = dm // X_HEADS
    return (yp.reshape(bp, seq, dm), ys.reshape(bs, seg, dm),
            mk.reshape(1, bp, n_mem, X_HEADS, hd), mv.reshape(1, bp, n_mem, X_HEADS, hd),
            p_re, p_im, p_gla, p_conv, s_re, s_im, s_gla[None], s_conv)
```

```python
import functools

import jax
import jax.numpy as jnp
from jax import lax
from jax.experimental import pallas as pl
from jax.experimental.pallas import tpu as pltpu

F32 = jnp.float32
BF16 = jnp.bfloat16

EPS = 1e-6
S5_GROUP = 16
S5_STATE = 64
S5_BLOCK_GROUPS = 16
GLA_HEADS = 4
GLA_RANK = 16
GLA_TAU = 16.0
GLA_CHUNK = 64
X_HEADS = 4
CONV_W = 3

SUBLANES = 8
LANES = 128
VMEM_LIMIT_BYTES = 56 * 1024 * 1024

_CONTRACT_LAST = (((1,), (1,)), ((), ()))
_CONTRACT_FIRST = (((0,), (0,)), ((), ()))


def _params(*semantics):
    return pltpu.CompilerParams(dimension_semantics=semantics, vmem_limit_bytes=VMEM_LIMIT_BYTES)


def _dot(a, b):
    return jnp.dot(a, b, preferred_element_type=F32)


def _for_row_chunks(n_rows, chunk, fn):
    def body(i, carry):
        fn(pl.multiple_of(i * chunk, chunk))
        return carry
    lax.fori_loop(0, n_rows // chunk, body, 0)


def _rms(x, gain):
    r = lax.rsqrt(jnp.mean(x * x, axis=-1, keepdims=True) + EPS)
    return (x * r) * gain


def _gelu(x):
    return 0.5 * x * (1.0 + lax.erf(x * (2.0 ** -0.5)))


def _rms_rows_to_bf16(x_ref, gain_ref, h_ref, n_rows, row0=0, chunk=128):
    def f(r0):
        h_ref[pl.ds(row0 + r0, chunk), :] = _rms(x_ref[pl.ds(r0, chunk), :], gain_ref[...]).astype(BF16)
    _for_row_chunks(n_rows, chunk, f)


def _s5_prep_body(lr_ref, li_ref, ldt_ref, br_ref, bi_ref, apr_ref, api_ref, bbr_ref, bbi_ref):
    lr = lr_ref[...]
    li = li_ref[...]
    dt = jnp.exp(ldt_ref[...])
    mag = jnp.exp(lr * dt)
    ar = mag * jnp.cos(li * dt)
    ai = mag * jnp.sin(li * dt)
    den = lr * lr + li * li
    pr = ar - 1.0
    qr = (pr * lr + ai * li) / den
    qi = (ai * lr - pr * li) / den
    for p in range(S5_GROUP):
        br = br_ref[p]
        bi = bi_ref[p]
        bbr_ref[p] = qr * br - qi * bi
        bbi_ref[p] = qr * bi + qi * br
    cr, ci = ar, ai
    apr_ref[0] = cr
    api_ref[0] = ci
    for k in range(1, SUBLANES):
        cr, ci = cr * ar - ci * ai, cr * ai + ci * ar
        apr_ref[k] = cr
        api_ref[k] = ci


def _s5_prep(lam_re, lam_im, log_dt, b_re, b_im):
    g, n = lam_re.shape
    out = jax.ShapeDtypeStruct((SUBLANES, g, n), F32)
    outb = jax.ShapeDtypeStruct((S5_GROUP, g, n), F32)
    return pl.pallas_call(
        _s5_prep_body, out_shape=(out, out, outb, outb), name="s5_prep",
    )(lam_re, lam_im, log_dt.reshape(g, 1), jnp.transpose(b_re, (2, 0, 1)), jnp.transpose(b_im, (2, 0, 1)))


def _s5_tables(lam_re, lam_im, log_dt, b_re, b_im, c_re, c_im, d, w_glu, b_glu, seg):
    g, n = lam_re.shape
    p = S5_GROUP
    gb = S5_BLOCK_GROUPS
    nb = g // gb
    apr, api, bbr, bbi = _s5_prep(lam_re, lam_im, log_dt, b_re, b_im)
    eye = jnp.eye(gb, dtype=F32)

    def bu_block(bb):
        return jnp.einsum('pbgn,gh->bgphn', bb.reshape(p, nb, gb, n), eye).reshape(nb, gb * p, gb * n)

    def c_block(c):
        return jnp.einsum('bgpn,gh->bgnhp', c.reshape(nb, gb, p, n), eye).reshape(nb, gb * n, gb * p)

    w_bu = jnp.concatenate([bu_block(bbr), bu_block(bbi)], axis=-1).astype(BF16)
    w_c = jnp.concatenate([c_block(c_re), -c_block(c_im)], axis=1).astype(BF16)
    w_g = jnp.einsum('bgpq,gh->bgphq', w_glu.reshape(nb, gb, p, p), eye).reshape(nb, gb * p, gb * p).astype(BF16)
    d_b = d.reshape(nb, 1, gb * p)
    bg_b = b_glu.reshape(nb, 1, gb * p)

    t = jnp.arange(SUBLANES) % seg
    pw_r = apr.reshape(SUBLANES, nb, gb * n)
    pw_i = api.reshape(SUBLANES, nb, gb * n)

    def masked(k):
        m = (t >= k)[:, None, None]
        return jnp.where(m, pw_r[k - 1][None], 0.0), jnp.where(m, pw_i[k - 1][None], 0.0)

    tabs = []
    k = 1
    while k < seg:
        tabs.extend(masked(k))
        k *= 2
    tabs.append(pw_r[t])
    tabs.append(pw_i[t])
    tab = jnp.transpose(jnp.stack(tabs), (2, 0, 1, 3))
    return w_bu, w_c, w_g, d_b, bg_b, tab


def _in_proj_body(x_ref, gain_ref, w_ref, wr_ref, wg_ref, wg2_ref, bg_ref, o_ref, h_ref, *, tm, n_main, n_r):
    j = pl.program_id(1)

    @pl.when(j == 0)
    def _():
        _rms_rows_to_bf16(x_ref, gain_ref, h_ref, tm)

    @pl.when(j < n_main)
    def _():
        o_ref[...] = _dot(h_ref[...], w_ref[...].astype(BF16))

    @pl.when((j >= n_main) & (j < n_main + n_r))
    def _():
        o_ref[...] = _dot(h_ref[...], wr_ref[...].astype(BF16))

    @pl.when(j == n_main + n_r)
    def _():
        g_low = _dot(h_ref[...], wg_ref[...].astype(BF16))
        pre = _dot(g_low.astype(BF16), wg2_ref[...].astype(BF16)) + bg_ref[...]
        log_sig = jnp.minimum(pre, 0.0) - jnp.log1p(jnp.exp(-jnp.abs(pre)))
        o_ref[...] = log_sig * (1.0 / GLA_TAU)


def _in_proj(x, gain, w_in, w_g2, b_g, *, tm, tn=512):
    n, dm = x.shape
    mix = dm
    n_u_q_k_v = mix // 2 + 2 * (mix // 4) + mix // 2
    col_g = n_u_q_k_v
    col_r = n_u_q_k_v + GLA_RANK
    w_r = w_in[:, col_r:]
    w_g = w_in[:, col_g:col_r]
    n_main = n_u_q_k_v // tn
    n_r = w_r.shape[1] // tn
    n_cols = n_u_q_k_v + w_r.shape[1] + w_g2.shape[1]
    body = functools.partial(_in_proj_body, tm=tm, n_main=n_main, n_r=n_r)
    return pl.pallas_call(
        body,
        grid=(n // tm, n_main + n_r + 1),
        in_specs=[
            pl.BlockSpec((tm, dm), lambda i, j: (i, 0)),
            pl.BlockSpec((1, dm), lambda i, j: (0, 0)),
            pl.BlockSpec((dm, tn), lambda i, j: (0, jnp.minimum(j, n_main - 1))),
            pl.BlockSpec((dm, tn), lambda i, j: (0, jnp.clip(j - n_main, 0, n_r - 1))),
            pl.BlockSpec((dm, GLA_RANK), lambda i, j: (0, 0)),
            pl.BlockSpec(w_g2.shape, lambda i, j: (0, 0)),
            pl.BlockSpec((1, w_g2.shape[1]), lambda i, j: (0, 0)),
        ],
        out_specs=pl.BlockSpec((tm, tn), lambda i, j: (i, j)),
        out_shape=jax.ShapeDtypeStruct((n, n_cols), F32),
        scratch_shapes=[pltpu.VMEM((tm, dm), BF16)],
        compiler_params=_params("arbitrary", "arbitrary"),
        name="in_proj",
    )(x, gain.reshape(1, dm), w_in, w_r, w_g, w_g2, b_g.reshape(1, -1))


def _s5_body(*refs, tt, seg, ns, chunk):
    if seg == SUBLANES:
        u_ref, wbu_ref, wc_ref, wg_ref, d_ref, bg_ref, tab_ref, z_ref, hr_ref, hi_ref, bu_ref, cr_ref, ci_ref = refs
    else:
        u_ref, wbu_ref, wc_ref, wg_ref, d_ref, bg_ref, tab_ref, h0r_ref, h0i_ref, z_ref, hr_ref, hi_ref, bu_ref = refs
    n_lvl = seg.bit_length() - 1
    last = SUBLANES - 1
    lane_group = 2 * LANES

    if seg == SUBLANES:
        @pl.when(pl.program_id(2) == 0)
        def _():
            cr_ref[...] = jnp.zeros_like(cr_ref)
            ci_ref[...] = jnp.zeros_like(ci_ref)

    def project_in(c):
        rows = slice(c * chunk, (c + 1) * chunk)
        bu = _dot(u_ref[rows, :].astype(BF16), wbu_ref[...])
        if seg != SUBLANES:
            seqs = slice(c * chunk // seg, (c + 1) * chunk // seg)
            a_r = tab_ref[2 * n_lvl, 0:1, :]
            a_i = tab_ref[2 * n_lvl + 1, 0:1, :]
            h0r = h0r_ref[seqs, :]
            h0i = h0i_ref[seqs, :]
            first = (lax.broadcasted_iota(jnp.int32, (chunk, chunk // seg), 0)
                     == seg * lax.broadcasted_iota(jnp.int32, (chunk, chunk // seg), 1))
            first_b = jnp.where(first, 1.0, 0.0).astype(BF16)
            bu = bu + jnp.concatenate([_dot_exact_rows(first_b, a_r * h0r - a_i * h0i),
                                       _dot_exact_rows(first_b, a_r * h0i + a_i * h0r)], axis=1)
        bu_ref[rows, :] = bu

    def scan(c, carry):
        carry = list(carry)
        for blk in range(chunk // SUBLANES):
            rows = slice(c * chunk + blk * SUBLANES, c * chunk + (blk + 1) * SUBLANES)
            for gi in range(ns // lane_group):
                cols = slice(gi * lane_group, (gi + 1) * lane_group)
                cols_im = slice(ns + gi * lane_group, ns + (gi + 1) * lane_group)
                xr = bu_ref[rows, cols]
                xi = bu_ref[rows, cols_im]
                for lvl in range(n_lvl):
                    k = 1 << lvl
                    ar = tab_ref[2 * lvl, :, cols]
                    ai = tab_ref[2 * lvl + 1, :, cols]
                    sr = pltpu.roll(xr, k, 0)
                    si = pltpu.roll(xi, k, 0)
                    xr, xi = xr + (ar * sr - ai * si), xi + (ar * si + ai * sr)
                if seg == SUBLANES:
                    pr = tab_ref[2 * n_lvl, :, cols]
                    pi = tab_ref[2 * n_lvl + 1, :, cols]
                    c_r, c_i = carry[gi]
                    hr = xr + (pr * c_r - pi * c_i)
                    hi = xi + (pr * c_i + pi * c_r)
                    carry[gi] = (jnp.broadcast_to(hr[last:last + 1], hr.shape),
                                 jnp.broadcast_to(hi[last:last + 1], hi.shape))
                else:
                    hr, hi = xr, xi
                    for e in range(SUBLANES // seg):
                        src = e * seg + seg - 1
                        dst = (c * chunk + blk * SUBLANES) // seg + e
                        hr_ref[dst:dst + 1, cols] = hr[src:src + 1]
                        hi_ref[dst:dst + 1, cols] = hi[src:src + 1]
                bu_ref[rows, cols] = hr
                bu_ref[rows, cols_im] = hi
        return carry

    def project_out(c):
        rows = slice(c * chunk, (c + 1) * chunk)
        h = bu_ref[rows, :]
        y = _dot(h.astype(BF16), wc_ref[...]) + d_ref[...] * u_ref[rows, :]
        z = _gelu(y)
        gate = jax.nn.sigmoid(_dot(z.astype(BF16), wg_ref[...]) + bg_ref[...])
        z_ref[rows, :] = z * gate

    n_chunks = tt // chunk
    n_groups = ns // lane_group
    group_cols = [slice(gi * lane_group, (gi + 1) * lane_group) for gi in range(n_groups)]
    carry = [(cr_ref[:, cols], ci_ref[:, cols]) for cols in group_cols] if seg == SUBLANES else [None] * n_groups
    project_in(0)
    for c in range(n_chunks):
        if c + 1 < n_chunks:
            project_in(c + 1)
        carry = scan(c, carry)
        project_out(c)
    if seg == SUBLANES:
        for cols, (c_r, c_i) in zip(group_cols, carry):
            cr_ref[:, cols] = c_r
            ci_ref[:, cols] = c_i
            hr_ref[:, cols] = c_r[0:1]
            hi_ref[:, cols] = c_i[0:1]


def _s5_mixer(proj3, tables, *, tt, seg, h0=None, chunk=128):
    w_bu, w_c, w_g, d_b, bg_b, tab = tables
    bsz, seq, _ = proj3.shape
    nb, cw, ns2 = w_bu.shape
    ns = ns2 // 2
    n_tab = tab.shape[1]
    body = functools.partial(_s5_body, tt=tt, seg=seg, ns=ns, chunk=min(chunk, tt))
    wspec = lambda shape: pl.BlockSpec((None,) + shape, lambda b, g, t: (g,) + (0,) * len(shape))
    in_specs = [
        pl.BlockSpec((None, tt, cw), lambda b, g, t: (b, t, g)),
        wspec((cw, ns2)), wspec((ns2, cw)), wspec((cw, cw)), wspec((1, cw)), wspec((1, cw)),
        wspec((n_tab, SUBLANES, ns)),
    ]
    args = [proj3, w_bu, w_c, w_g, d_b, bg_b, tab]
    z_shape = jax.ShapeDtypeStruct((bsz, seq, nb * cw), F32)
    z_spec = pl.BlockSpec((None, tt, cw), lambda b, g, t: (b, t, g))
    scratch = [pltpu.VMEM((tt, ns2), F32)]
    if seg == SUBLANES:
        st_shape = jax.ShapeDtypeStruct((bsz, nb, 1, ns), F32)
        st_spec = pl.BlockSpec((None, None, 1, ns), lambda b, g, t: (b, g, 0, 0))
        scratch += [pltpu.VMEM((SUBLANES, ns), F32), pltpu.VMEM((SUBLANES, ns), F32)]
    else:
        h0r, h0i = h0
        st_spec = pl.BlockSpec((None, tt // seg, ns), lambda b, g, t: (b, t, g))
        in_specs += [st_spec] * 2
        args += [h0r, h0i]
        st_shape = jax.ShapeDtypeStruct((bsz, seq // seg, nb * ns), F32)
    return pl.pallas_call(
        body,
        grid=(bsz, nb, seq // tt),
        in_specs=in_specs,
        out_specs=(z_spec, st_spec, st_spec),
        out_shape=(z_shape, st_shape, st_shape),
        scratch_shapes=scratch,
        compiler_params=_params("arbitrary", "arbitrary", "arbitrary"),
        name="s5_seg%d" % seg,
    )(*args)


def _split3_bf16(x):
    hi = x.astype(BF16)
    r1 = x - hi.astype(F32)
    mid = r1.astype(BF16)
    lo = (r1 - mid.astype(F32)).astype(BF16)
    return hi, mid, lo


def _gla_out(o, gain, r):
    return _rms(o, gain) * (r * jax.nn.sigmoid(r))


def _gla_seq_body(q_ref, k_ref, v_ref, la_ref, r_ref, gain_ref, y_ref, st_out_ref, st_ref, *, tt, dk):
    c = GLA_CHUNK

    @pl.when(pl.program_id(2) == 0)
    def _():
        st_ref[...] = jnp.zeros_like(st_ref)

    row = lax.broadcasted_iota(jnp.int32, (tt, tt), 0)
    col = lax.broadcasted_iota(jnp.int32, (tt, tt), 1)
    shift = c.bit_length() - 1
    tri = (col <= row) & ((row >> shift) == (col >> shift))
    tri_b = jnp.where(tri, 1.0, 0.0).astype(BF16)
    hi, mid, lo = _split3_bf16(la_ref[...])
    bcum = _dot(tri_b, hi) + _dot(tri_b, mid) + _dot(tri_b, lo)
    causal = lax.broadcasted_iota(jnp.int32, (c, c), 1) <= lax.broadcasted_iota(jnp.int32, (c, c), 0)
    scale = dk ** -0.5
    chunks = [slice(ci * c, (ci + 1) * c) for ci in range(tt // c)]
    qts, decays, o_intra, kvs = [], [], [], []
    for sl in chunks:
        bc = bcum[sl]
        bl = bc[c - 1:c]
        qt = ((q_ref[sl, :] * scale) * jnp.exp(bc)).astype(BF16)
        kc = k_ref[sl, :]
        kt = (kc * jnp.exp(-bc)).astype(BF16)
        ke = (kc * jnp.exp(bl - bc)).astype(BF16)
        vb = v_ref[sl, :].astype(BF16)
        s = lax.dot_general(qt, kt, _CONTRACT_LAST, preferred_element_type=F32)
        s = jnp.where(causal, s, 0.0)
        qts.append(qt)
        decays.append(jnp.exp(bl))
        o_intra.append(_dot(s.astype(BF16), vb))
        kvs.append(lax.dot_general(vb, ke, _CONTRACT_FIRST, preferred_element_type=F32))
    st = st_ref[...]
    for sl, qt, dec, oi, kv in zip(chunks, qts, decays, o_intra, kvs):
        o = oi + lax.dot_general(qt, st.astype(BF16), _CONTRACT_LAST, preferred_element_type=F32)
        st = st * dec + kv
        y_ref[sl, :] = _gla_out(o, gain_ref[...], r_ref[sl, :]).astype(y_ref.dtype)
    st_ref[...] = st
    st_out_ref[...] = st


def _gla_seq(proj3, gain, *, tt):
    bsz, seq, _ = proj3.shape
    hh = GLA_HEADS
    dv = gain.shape[0]
    dk = dv // 2
    col_q = 1024 // dk
    col_k = (1024 + hh * dk) // dk
    col_v = (1024 + 2 * hh * dk) // dv
    col_r = (1024 + 2 * hh * dk + hh * dv) // dv
    col_la = (1024 + 2 * hh * dk + 2 * hh * dv) // dk
    spec = lambda w, c0: pl.BlockSpec((None, tt, w), lambda b, h, t: (b, t, c0 + h))
    body = functools.partial(_gla_seq_body, tt=tt, dk=dk)
    return pl.pallas_call(
        body,
        grid=(bsz, hh, seq // tt),
        in_specs=[spec(dk, col_q), spec(dk, col_k), spec(dv, col_v), spec(dk, col_la), spec(dv, col_r),
                  pl.BlockSpec((1, dv), lambda b, h, t: (0, 0))],
        out_specs=(pl.BlockSpec((None, tt, dv), lambda b, h, t: (b, t, h)),
                   pl.BlockSpec((None, None, dv, dk), lambda b, h, t: (b, h, 0, 0))),
        out_shape=(jax.ShapeDtypeStruct((bsz, seq, hh * dv), BF16),
                   jax.ShapeDtypeStruct((bsz, hh, dv, dk), F32)),
        scratch_shapes=[pltpu.VMEM((dv, dk), F32)],
        compiler_params=_params("arbitrary", "arbitrary", "arbitrary"),
        name="gla_seq",
    )(proj3, proj3, proj3, proj3, proj3, gain.reshape(1, dv))


def _gla_step_body(q_ref, k_ref, v_ref, la_ref, r_ref, gain_ref, s_ref, y_ref, so_ref, *, nb, seg, dk, dv):
    hh = GLA_HEADS
    rows = SUBLANES
    per = rows // seg
    row = lax.broadcasted_iota(jnp.int32, (rows, rows), 0)
    col = lax.broadcasted_iota(jnp.int32, (rows, rows), 1)
    shift = seg.bit_length() - 1
    causal = (col <= row) & ((row >> shift) == (col >> shift))
    rid = lax.broadcasted_iota(jnp.int32, (rows, 1), 0)
    t_in = rid & (seg - 1)
    ones_b = jnp.ones((rows, dv), BF16)
    scale = dk ** -0.5

    def pair(i, carry):
        r0 = pl.multiple_of(i * rows, rows)
        prep = []
        for h in range(hh):
            ks = slice(h * dk, (h + 1) * dk)
            vs = slice(h * dv, (h + 1) * dv)
            bc = la_ref[pl.ds(r0, rows), ks]
            k = 1
            while k < seg:
                bc = bc + jnp.where(t_in >= k, pltpu.roll(bc, k, 0), 0.0)
                k *= 2
            bl = jnp.zeros_like(bc)
            for e in range(per):
                last = e * seg + seg - 1
                bl = jnp.where((rid >> shift) == e, bc[last:last + 1], bl)
            qt = ((q_ref[pl.ds(r0, rows), ks] * scale) * jnp.exp(bc)).astype(BF16)
            kc = k_ref[pl.ds(r0, rows), ks]
            kt = (kc * jnp.exp(-bc)).astype(BF16)
            ke = kc * jnp.exp(bl - bc)
            vb = v_ref[pl.ds(r0, rows), vs].astype(BF16)
            per_seq = []
            for e in range(per):
                last = e * seg + seg - 1
                dec = jnp.exp(bc[last:last + 1])
                d_hi, d_mid, d_lo = _split3_bf16(dec)
                d3 = jnp.where(rid == 0, d_hi.astype(F32),
                               jnp.where(rid == 1, d_mid.astype(F32),
                                         jnp.where(rid == 2, d_lo.astype(F32), 0.0))).astype(BF16)
                ke_e = jnp.where((rid >> shift) == e, ke, 0.0).astype(BF16)
                per_seq.append((d3, ke_e, s_ref[i * per + e, h]))
            prep.append((vs, qt, kt, vb, per_seq))
        scores = [lax.dot_general(qt, kt, _CONTRACT_LAST, preferred_element_type=F32) for _, qt, kt, _, _ in prep]
        inter, dec_cols, kvs = [], [], []
        for _, qt, _, vb, per_seq in prep:
            for d3, ke_e, st in per_seq:
                inter.append(_dot(qt, st.astype(BF16)))
                dec_cols.append(lax.dot_general(d3, ones_b, _CONTRACT_FIRST, preferred_element_type=F32))
                kvs.append(lax.dot_general(ke_e, vb, _CONTRACT_FIRST, preferred_element_type=F32))
        intra = [_dot(jnp.where(causal, s, 0.0).astype(BF16), vb) for s, (_, _, _, vb, _) in zip(scores, prep)]
        for h, (vs, _, _, _, per_seq) in enumerate(prep):
            o_inter = inter[h * per]
            for e in range(per):
                if e:
                    o_inter = jnp.where((rid >> shift) == e, inter[h * per + e], o_inter)
                so_ref[i * per + e, h] = dec_cols[h * per + e] * per_seq[e][2] + kvs[h * per + e]
            y_ref[pl.ds(r0, rows), vs] = _gla_out(intra[h] + o_inter, gain_ref[...], r_ref[pl.ds(r0, rows), vs])
        return carry

    lax.fori_loop(0, nb // per, pair, 0)


def _gla_step(proj, gain, state, *, seg, nb=8):
    n = proj.shape[0]
    bsz, hh, dk, dv = state.shape
    rows = nb * seg
    c_q = 1024 // (hh * dk)
    c_k = c_q + 1
    c_v = (1024 + 2 * hh * dk) // (hh * dv)
    c_r = c_v + 1
    c_la = (1024 + 2 * hh * dk + 2 * hh * dv) // (hh * dk)
    spec = lambda w, c0: pl.BlockSpec((rows, w), lambda i: (i, c0))
    body = functools.partial(_gla_step_body, nb=nb, seg=seg, dk=dk, dv=dv)
    return pl.pallas_call(
        body,
        grid=(bsz // nb,),
        in_specs=[spec(hh * dk, c_q), spec(hh * dk, c_k), spec(hh * dv, c_v), spec(hh * dk, c_la), spec(hh * dv, c_r),
                  pl.BlockSpec((1, dv), lambda i: (0, 0)),
                  pl.BlockSpec((nb, hh, dk, dv), lambda i: (i, 0, 0, 0))],
        out_specs=(pl.BlockSpec((rows, hh * dv), lambda i: (i, 0)),
                   pl.BlockSpec((nb, hh, dk, dv), lambda i: (i, 0, 0, 0))),
        out_shape=(jax.ShapeDtypeStruct((n, hh * dv), F32), jax.ShapeDtypeStruct(state.shape, F32)),
        compiler_params=_params("arbitrary"),
        name="gla_step",
    )(proj, proj, proj, proj, proj, gain.reshape(1, dv), state)


def _tiled_linear(name, prologue, row_ins, aux_ins, aux_specs, w, *, tm, tn, residual=None):
    n = row_ins[0].shape[0]
    k_dim, n_cols = w.shape
    nr, na = len(row_ins), len(aux_ins)
    has_res = residual is not None

    def body(*refs):
        row_refs = refs[:nr]
        aux_refs = refs[nr:nr + na]
        w_ref = refs[nr + na]
        o_ref, h_ref = refs[-2], refs[-1]

        @pl.when(pl.program_id(1) == 0)
        def _():
            prologue(row_refs, aux_refs, h_ref)

        acc = _dot(h_ref[...], w_ref[...].astype(BF16))
        if has_res:
            acc = acc + refs[nr + na + 1][...]
        o_ref[...] = acc

    in_specs = [pl.BlockSpec((tm, a.shape[1]), lambda i, j: (i, 0)) for a in row_ins]
    in_specs += list(aux_specs)
    in_specs.append(pl.BlockSpec((k_dim, tn), lambda i, j: (0, j)))
    args = list(row_ins) + list(aux_ins) + [w]
    if has_res:
        in_specs.append(pl.BlockSpec((tm, tn), lambda i, j: (i, j)))
        args.append(residual)
    return pl.pallas_call(
        body,
        grid=(n // tm, n_cols // tn),
        in_specs=in_specs,
        out_specs=pl.BlockSpec((tm, tn), lambda i, j: (i, j)),
        out_shape=jax.ShapeDtypeStruct((n, n_cols), F32),
        scratch_shapes=[pltpu.VMEM((tm, k_dim), BF16)],
        compiler_params=_params("arbitrary", "arbitrary"),
        name=name,
    )(*args)


def _whole(a):
    return pl.BlockSpec(a.shape, lambda i, j: (0,) * a.ndim)


def _norm_linear(name, x, gain, w, *, tm, tn=512):
    def prologue(row_refs, aux_refs, h_ref):
        _rms_rows_to_bf16(row_refs[0], aux_refs[0], h_ref, tm)
    g = gain.reshape(1, -1)
    return _tiled_linear(name, prologue, [x], [g], [_whole(g)], w, tm=tm, tn=tn)


def _cast_linear(name, x, w, residual, *, tm, tn=512):
    def prologue(row_refs, aux_refs, h_ref):
        def f(r0):
            h_ref[pl.ds(r0, 128), :] = row_refs[0][pl.ds(r0, 128), :].astype(BF16)
        _for_row_chunks(tm, 128, f)
    return _tiled_linear(name, prologue, [x], [], [], w, tm=tm, tn=tn, residual=residual)


def _out_proj(z_s5, y_gla, gain_s5, w_out, x, *, tm, tn=512):
    ws = z_s5.shape[1]

    def prologue(row_refs, aux_refs, h_ref):
        def f(r0):
            h_ref[pl.ds(r0, 128), 0:ws] = _rms(row_refs[0][pl.ds(r0, 128), :], aux_refs[0][...]).astype(BF16)
            h_ref[pl.ds(r0, 128), ws:] = row_refs[1][pl.ds(r0, 128), :].astype(BF16)
        _for_row_chunks(tm, 128, f)
    g = gain_s5.reshape(1, -1)
    return _tiled_linear("out_proj", prologue, [z_s5, y_gla], [g], [_whole(g)], w_out, tm=tm, tn=tn, residual=x)


def _attend(q, kb, vb, scale):
    s = lax.dot_general(q, kb, _CONTRACT_LAST, preferred_element_type=F32) * scale
    e = jnp.exp(s - jnp.max(s, axis=-1, keepdims=True))
    p = e / jnp.sum(e, axis=-1, keepdims=True)
    return _dot(p.astype(BF16), vb)


def _mix_xattn_seq_body(z_ref, y_ref, xres_ref, gs5_ref, gx_ref, wout_ref, wq_ref, wo_ref, mk_ref, mv_ref,
                        o_ref, x1_ref, h_ref, q_ref, *, tm, tn, nt, hd, chunk):
    j = pl.program_id(1)
    ws = z_ref.shape[1]
    scale = hd ** -0.5

    @pl.when(j == 0)
    def _():
        def f(r0):
            h_ref[pl.ds(r0, 128), 0:ws] = _rms(z_ref[pl.ds(r0, 128), :], gs5_ref[...]).astype(BF16)
            h_ref[pl.ds(r0, 128), ws:] = y_ref[pl.ds(r0, 128), :].astype(BF16)
        _for_row_chunks(tm, 128, f)

    @pl.when(j == nt)
    def _():
        _rms_rows_to_bf16(x1_ref, gx_ref, h_ref, tm)

    @pl.when(j == 2 * nt)
    def _():
        heads = [slice(h * hd, (h + 1) * hd) for h in range(X_HEADS)]

        def f(r0):
            scores = [lax.dot_general(q_ref[pl.ds(r0, chunk), hs], mk_ref[:, hs], _CONTRACT_LAST,
                                      preferred_element_type=F32) * scale for hs in heads]
            probs = []
            for s in scores:
                e = jnp.exp(s - jnp.max(s, axis=-1, keepdims=True))
                probs.append((e / jnp.sum(e, axis=-1, keepdims=True)).astype(BF16))
            for hs, p in zip(heads, probs):
                h_ref[pl.ds(r0, chunk), hs] = _dot(p, mv_ref[:, hs]).astype(BF16)
        _for_row_chunks(tm, chunk, f)

    for jj in range(nt):
        cols = slice(jj * tn, (jj + 1) * tn)

        @pl.when(j == jj)
        def _(cols=cols):
            x1_ref[:, cols] = _dot(h_ref[...], wout_ref[...]) + xres_ref[...]

        @pl.when(j == nt + jj)
        def _(cols=cols):
            q_ref[:, cols] = _dot(h_ref[...], wq_ref[...]).astype(BF16)

        @pl.when(j == 2 * nt + jj)
        def _(cols=cols):
            o_ref[...] = _dot(h_ref[...], wo_ref[...]) + x1_ref[:, cols]


def _mix_xattn_seq(z_s5, y_gla, x, gain_s5, gain_x, w_out, w_xq, w_xo, mk_b, mv_b, *, tm, seq, tn=512, chunk=256):
    n, dm = x.shape
    nt = dm // tn
    tiles_per_seq = seq // tm
    body = functools.partial(_mix_xattn_seq_body, tm=tm, tn=tn, nt=nt, hd=dm // X_HEADS, chunk=chunk)
    gs = gain_s5.reshape(1, -1)
    gx = gain_x.reshape(1, -1)
    phase = lambda p: (lambda i, j: (0, jnp.clip(j - p * nt, 0, nt - 1)))
    mspec = pl.BlockSpec((None,) + mk_b.shape[1:], lambda i, j: (i // tiles_per_seq, 0, 0))
    return pl.pallas_call(
        body,
        grid=(n // tm, 3 * nt),
        in_specs=[
            pl.BlockSpec((tm, z_s5.shape[1]), lambda i, j: (i, 0)),
            pl.BlockSpec((tm, y_gla.shape[1]), lambda i, j: (i, 0)),
            pl.BlockSpec((tm, tn), lambda i, j: (i, jnp.minimum(j, nt - 1))),
            _whole(gs), _whole(gx),
            pl.BlockSpec((dm, tn), phase(0)), pl.BlockSpec((dm, tn), phase(1)), pl.BlockSpec((dm, tn), phase(2)),
            mspec, mspec,
        ],
        out_specs=pl.BlockSpec((tm, tn), lambda i, j: (i, jnp.clip(j - 2 * nt, 0, nt - 1))),
        out_shape=jax.ShapeDtypeStruct((n, dm), F32),
        scratch_shapes=[pltpu.VMEM((tm, dm), F32), pltpu.VMEM((tm, dm), BF16), pltpu.VMEM((tm, dm), BF16)],
        compiler_params=_params("arbitrary", "arbitrary"),
        name="mix_xattn_seq",
    )(z_s5, y_gla, x, gs, gx, w_out, w_xq, w_xo, mk_b, mv_b)


def _cache_rows(c):
    b, m, h, hd = c.shape
    n_chunk = hd // LANES
    return c.reshape(b, m, h, n_chunk, LANES).transpose(0, 1, 3, 2, 4).reshape(b * m * n_chunk * h, LANES)


def _mem_proj_body(x_ref, gain_ref, w_ref, o_ref, ob_ref, h_ref, *, tm, hd):
    j = pl.program_id(1)

    @pl.when(j == 0)
    def _():
        _rms_rows_to_bf16(x_ref, gain_ref, h_ref, tm)

    y = _dot(h_ref[...], w_ref[...].astype(BF16))
    ob_ref[...] = y.astype(BF16)
    n_chunk = hd // LANES
    pitch = n_chunk * X_HEADS
    for ck in range(n_chunk):
        o_ref[pl.ds(ck * X_HEADS + j, tm, stride=pitch), :] = y[:, ck * LANES:(ck + 1) * LANES]


def _mem_proj(name, mem, gain, w, *, tm):
    b, m, dm = mem.shape
    n = b * m
    hd = dm // X_HEADS
    n_chunk = hd // LANES
    pitch = n_chunk * X_HEADS
    body = functools.partial(_mem_proj_body, tm=tm, hd=hd)
    g = gain.reshape(1, dm)
    rows, plain = pl.pallas_call(
        body,
        grid=(n // tm, X_HEADS),
        in_specs=[pl.BlockSpec((tm, dm), lambda i, j: (i, 0)), _whole(g), pl.BlockSpec((dm, hd), lambda i, j: (0, j))],
        out_specs=(pl.BlockSpec((tm * pitch, LANES), lambda i, j: (i, 0)),
                   pl.BlockSpec((tm, hd), lambda i, j: (i, j))),
        out_shape=(jax.ShapeDtypeStruct((n * pitch, LANES), F32), jax.ShapeDtypeStruct((n, dm), BF16)),
        scratch_shapes=[pltpu.VMEM((tm, dm), BF16)],
        compiler_params=_params("arbitrary", "arbitrary"),
        name=name,
    )(mem.reshape(n, dm), g, w)
    full = rows.reshape(b, m, n_chunk, X_HEADS, LANES).transpose(0, 1, 3, 2, 4).reshape(1, b, m, X_HEADS, hd)
    return full, plain.reshape(b, m, dm)


def _xattn_step_body(q_ref, k_ref, v_ref, o_ref, *, nb, seg, hd, n_mem):
    rows = SUBLANES
    per = rows // seg
    shift = seg.bit_length() - 1
    rid = lax.broadcasted_iota(jnp.int32, (rows, 1), 0)
    scale = hd ** -0.5
    n_chunk = hd // LANES
    pitch = n_chunk * X_HEADS

    def head_rows(ref, b, h):
        parts = [ref[pl.ds(b * n_mem * pitch + ck * X_HEADS + h, n_mem, stride=pitch), :] for ck in range(n_chunk)]
        return jnp.concatenate(parts, axis=1).astype(BF16)

    pairs = [(b, h) for b in range(nb) for h in range(X_HEADS)]
    scores = []
    for b, h in pairs:
        r0 = (b // per) * rows
        q = q_ref[r0:r0 + rows, h * hd:(h + 1) * hd].astype(BF16)
        scores.append(lax.dot_general(q, head_rows(k_ref, b, h), _CONTRACT_LAST, preferred_element_type=F32) * scale)
    probs = []
    for s in scores:
        e = jnp.exp(s - jnp.max(s, axis=-1, keepdims=True))
        probs.append((e / jnp.sum(e, axis=-1, keepdims=True)).astype(BF16))
    outs = {}
    for (b, h), p in zip(pairs, probs):
        outs[b, h] = _dot(p, head_rows(v_ref, b, h))
    for g in range(nb // per):
        for h in range(X_HEADS):
            o = outs[g * per, h]
            for e in range(1, per):
                o = jnp.where((rid >> shift) == e, outs[g * per + e, h], o)
            o_ref[g * rows:(g + 1) * rows, h * hd:(h + 1) * hd] = o


def _xattn_step(q, mem_k, mem_v, *, seg, nb=4):
    n, dm = q.shape
    bsz, m, _, hd = mem_k.shape
    body = functools.partial(_xattn_step_body, nb=nb, seg=seg, hd=hd, n_mem=m)
    rows_per_seq = m * dm // LANES
    mspec = pl.BlockSpec((nb * rows_per_seq, LANES), lambda i: (i, 0))
    return pl.pallas_call(
        body,
        grid=(bsz // nb,),
        in_specs=[pl.BlockSpec((nb * seg, dm), lambda i: (i, 0)), mspec, mspec],
        out_specs=pl.BlockSpec((nb * seg, dm), lambda i: (i, 0)),
        out_shape=jax.ShapeDtypeStruct((n, dm), F32),
        compiler_params=_params("arbitrary"),
        name="xattn_step",
    )(q, _cache_rows(mem_k), _cache_rows(mem_v))


def _ffn_finish(conv, gate, wd_ref, o_ref, gfin_ref, j, nj, tm):
    act = (_gelu(conv) * gate).astype(BF16)
    o_ref[...] += _dot(act, wd_ref[...].astype(BF16))

    @pl.when(j == nj - 1)
    def _():
        def f(r0):
            o_ref[pl.ds(r0, 128), :] = _rms(o_ref[pl.ds(r0, 128), :], gfin_ref[...])
        _for_row_chunks(tm, 128, f)


def _ffn_seq_body(x_ref, gain_ref, wa_ref, wg_ref, wd_ref, cw_ref, cb_ref, gfin_ref, o_ref, tail_ref, h_ref, a_ref,
                  *, tm, halo, tiles_per_seq, nj):
    i = pl.program_id(0)
    j = pl.program_id(1)

    @pl.when(j == 0)
    def _():
        @pl.when(i % tiles_per_seq == 0)
        def _():
            h_ref[0:halo, :] = jnp.zeros((halo, h_ref.shape[1]), BF16)

        @pl.when(i % tiles_per_seq != 0)
        def _():
            h_ref[0:halo, :] = h_ref[tm:tm + halo, :]

        _rms_rows_to_bf16(x_ref, gain_ref, h_ref, tm, row0=halo)
        o_ref[...] = x_ref[...]

    a_ref[...] = _dot(h_ref[...], wa_ref[...].astype(BF16))
    gate = _dot(h_ref[halo:halo + tm, :], wg_ref[...].astype(BF16))
    conv = cb_ref[...]
    for tap in range(CONV_W):
        conv = conv + a_ref[pl.ds(halo - (CONV_W - 1) + tap, tm), :] * cw_ref[tap:tap + 1, :]
    tail_ref[...] = a_ref[pl.ds(halo + tm - SUBLANES, SUBLANES), :]
    _ffn_finish(conv, gate, wd_ref, o_ref, gfin_ref, j, nj, tm)


def _ffn_seq(x, gain, w_up, conv_w, conv_b, w_down, gain_final, *, tm, seq, tf=512, halo=16):
    n, dm = x.shape
    dff = w_down.shape[0]
    nj = dff // tf
    body = functools.partial(_ffn_seq_body, tm=tm, halo=halo, tiles_per_seq=seq // tm, nj=nj)
    return pl.pallas_call(
        body,
        grid=(n // tm, nj),
        in_specs=[
            pl.BlockSpec((tm, dm), lambda i, j: (i, 0)),
            pl.BlockSpec((1, dm), lambda i, j: (0, 0)),
            pl.BlockSpec((dm, tf), lambda i, j: (0, j)),
            pl.BlockSpec((dm, tf), lambda i, j: (0, nj + j)),
            pl.BlockSpec((tf, dm), lambda i, j: (j, 0)),
            pl.BlockSpec((CONV_W, tf), lambda i, j: (0, j)),
            pl.BlockSpec((1, tf), lambda i, j: (0, j)),
            pl.BlockSpec((1, dm), lambda i, j: (0, 0)),
        ],
        out_specs=(pl.BlockSpec((tm, dm), lambda i, j: (i, 0)),
                   pl.BlockSpec((None, SUBLANES, tf), lambda i, j: (i, 0, j))),
        out_shape=(jax.ShapeDtypeStruct((n, dm), F32),
                   jax.ShapeDtypeStruct((n // tm, SUBLANES, dff), F32)),
        scratch_shapes=[pltpu.VMEM((halo + tm, dm), BF16), pltpu.VMEM((halo + tm, tf), F32)],
        compiler_params=_params("arbitrary", "arbitrary"),
        name="ffn_seq",
    )(x, gain.reshape(1, dm), w_up, w_up, w_down, conv_w, conv_b.reshape(1, dff), gain_final.reshape(1, dm))


def _dot_exact_rows(sel_b, x):
    hi, mid, lo = _split3_bf16(x)
    return _dot(sel_b, hi) + _dot(sel_b, mid) + _dot(sel_b, lo)


def _ffn_step_body(x_ref, gain_ref, wa_ref, wg_ref, wd_ref, cw_ref, cb_ref, gfin_ref, b0_ref, b1_ref,
                   o_ref, a_new_ref, h_ref, *, tm, seg, nj):
    j = pl.program_id(1)
    nseq = tm // seg
    shift = seg.bit_length() - 1

    @pl.when(j == 0)
    def _():
        _rms_rows_to_bf16(x_ref, gain_ref, h_ref, tm)
        o_ref[...] = x_ref[...]

    a = _dot(h_ref[...], wa_ref[...].astype(BF16))
    gate = _dot(h_ref[...], wg_ref[...].astype(BF16))
    t_in = lax.broadcasted_iota(jnp.int32, (tm, 1), 0) & (seg - 1)
    prev1 = jnp.where(t_in >= 1, pltpu.roll(a, 1, 0), 0.0)
    prev2 = jnp.where(t_in >= 2, pltpu.roll(a, 2, 0), 0.0)
    conv = cb_ref[...] + prev2 * cw_ref[0:1, :]
    conv = conv + prev1 * cw_ref[1:2, :]
    conv = conv + a * cw_ref[2:3, :]
    b0 = b0_ref[...]
    b1 = b1_ref[...]
    carried = jnp.concatenate([b0 * cw_ref[0:1, :] + b1 * cw_ref[1:2, :], b1 * cw_ref[0:1, :]], axis=0)
    r = lax.broadcasted_iota(jnp.int32, (tm, 2 * nseq), 0)
    c = lax.broadcasted_iota(jnp.int32, (tm, 2 * nseq), 1)
    spread = ((r >> shift) == (c & (nseq - 1))) & ((r & (seg - 1)) == (c >> (nseq.bit_length() - 1)))
    conv = conv + _dot_exact_rows(jnp.where(spread, 1.0, 0.0).astype(BF16), carried)
    r = lax.broadcasted_iota(jnp.int32, (2 * nseq, tm), 0)
    c = lax.broadcasted_iota(jnp.int32, (2 * nseq, tm), 1)
    pick = ((c >> shift) == (r & (nseq - 1))) & ((c & (seg - 1)) == seg - 2 + (r >> (nseq.bit_length() - 1)))
    picked = _dot_exact_rows(jnp.where(pick, 1.0, 0.0).astype(BF16), a)
    a_new_ref[0] = picked[0:nseq]
    a_new_ref[1] = picked[nseq:2 * nseq]
    _ffn_finish(conv, gate, wd_ref, o_ref, gfin_ref, j, nj, tm)


def _ffn_step(x, gain, w_up, conv_w, conv_b, w_down, gain_final, b0, b1, *, seg, tf=512):
    n, dm = x.shape
    dff = w_down.shape[0]
    nj = dff // tf
    tm = n
    nseq = n // seg
    body = functools.partial(_ffn_step_body, tm=tm, seg=seg, nj=nj)
    return pl.pallas_call(
        body,
        grid=(1, nj),
        in_specs=[
            pl.BlockSpec((tm, dm), lambda i, j: (i, 0)),
            pl.BlockSpec((1, dm), lambda i, j: (0, 0)),
            pl.BlockSpec((dm, tf), lambda i, j: (0, j)),
            pl.BlockSpec((dm, tf), lambda i, j: (0, nj + j)),
            pl.BlockSpec((tf, dm), lambda i, j: (j, 0)),
            pl.BlockSpec((CONV_W, tf), lambda i, j: (0, j)),
            pl.BlockSpec((1, tf), lambda i, j: (0, j)),
            pl.BlockSpec((1, dm), lambda i, j: (0, 0)),
            pl.BlockSpec((nseq, tf), lambda i, j: (0, j)),
            pl.BlockSpec((nseq, tf), lambda i, j: (0, j)),
        ],
        out_specs=(pl.BlockSpec((tm, dm), lambda i, j: (i, 0)),
                   pl.BlockSpec((CONV_W - 1, nseq, tf), lambda i, j: (0, 0, j))),
        out_shape=(jax.ShapeDtypeStruct((n, dm), F32), jax.ShapeDtypeStruct((CONV_W - 1, nseq, dff), F32)),
        scratch_shapes=[pltpu.VMEM((tm, dm), BF16)],
        compiler_params=_params("arbitrary", "arbitrary"),
        name="ffn_step",
    )(x, gain.reshape(1, dm), w_up, w_up, w_down, conv_w, conv_b.reshape(1, dff), gain_final.reshape(1, dm),
      b0, b1)


def kernel(x_prompt, x_sample, mem_prompt, cache_mem_k, cache_mem_v, state_s5_re, state_s5_im, state_gla, state_conv, norm_mix, w_in, s5_lambda_re, s5_lambda_im, s5_log_dt, s5_b_re, s5_b_im, s5_c_re, s5_c_im, s5_d, s5_w_glu, s5_b_glu, norm_s5_out, gla_w_g2, gla_b_g, norm_gla_out, w_out, norm_xattn, norm_mem, w_xq, w_xk, w_xv, w_xo, norm_ffn, w_up, conv_w, conv_b, w_down, norm_final):
    depth = w_in.shape[0]
    assert depth == 1
    bp, seq, dm = x_prompt.shape
    bs, seg, _ = x_sample.shape
    n_mem = mem_prompt.shape[1]
    dff = w_down.shape[1]
    l = 0
    s5_args = (s5_lambda_re[l], s5_lambda_im[l], s5_log_dt[l], s5_b_re[l], s5_b_im[l], s5_c_re[l], s5_c_im[l],
               s5_d[l], s5_w_glu[l], s5_b_glu[l])
    tm_p = min(1024, seq)
    n_s = bs * seg
    w_in, w_out, w_xq, w_xo, w_up, w_down = (w.astype(BF16) for w in (w_in, w_out, w_xq, w_xo, w_up, w_down))

    mk, mk_b = _mem_proj("mem_k", mem_prompt, norm_mem[l], w_xk[l], tm=min(1024, bp * n_mem))
    mv, mv_b = _mem_proj("mem_v", mem_prompt, norm_mem[l], w_xv[l], tm=min(1024, bp * n_mem))

    xp = x_prompt.reshape(bp * seq, dm)
    proj = _in_proj(xp, norm_mix[l], w_in[l], gla_w_g2[l], gla_b_g[l], tm=tm_p)
    proj3 = proj.reshape(bp, seq, -1)
    z_s5, hend_r, hend_i = _s5_mixer(proj3, _s5_tables(*s5_args, seg=SUBLANES), tt=min(512, seq), seg=SUBLANES)
    y_gla, st_gla = _gla_seq(proj3, norm_gla_out[l], tt=min(512, seq))
    x2 = _mix_xattn_seq(z_s5.reshape(bp * seq, -1), y_gla.reshape(bp * seq, -1), xp, norm_s5_out[l], norm_xattn[l],
                        w_out[l], w_xq[l], w_xo[l], mk_b, mv_b, tm=tm_p, seq=seq)
    yp, tails = _ffn_seq(x2, norm_ffn[l], w_up[l], conv_w[l], conv_b[l], w_down[l], norm_final, tm=tm_p, seq=seq)

    g_total = s5_lambda_re.shape[1]
    n_state = s5_lambda_re.shape[2]
    p_re = hend_r.reshape(1, bp, g_total, n_state)
    p_im = hend_i.reshape(1, bp, g_total, n_state)
    p_gla = jnp.swapaxes(st_gla, -1, -2)[None]
    tiles_per_seq = seq // tm_p
    p_conv = tails.reshape(bp, tiles_per_seq, SUBLANES, dff)[:, -1, SUBLANES - (CONV_W - 1):, :][None]

    xs = x_sample.reshape(n_s, dm)
    proj_s = _in_proj(xs, norm_mix[l], w_in[l], gla_w_g2[l], gla_b_g[l], tm=n_s)
    h0r = state_s5_re[l].reshape(1, bs, -1)
    h0i = state_s5_im[l].reshape(1, bs, -1)
    z_s5_s, hs_r, hs_i = _s5_mixer(proj_s[None], _s5_tables(*s5_args, seg=seg), tt=n_s, seg=seg, h0=(h0r, h0i))
    y_gla_s, s_gla = _gla_step(proj_s, norm_gla_out[l], state_gla[l], seg=seg)
    x1s = _out_proj(z_s5_s[0], y_gla_s, norm_s5_out[l], w_out[l], xs, tm=n_s)
    qs = _norm_linear("xattn_q_s", x1s, norm_xattn[l], w_xq[l], tm=n_s)
    att = _xattn_step(qs, cache_mem_k[l], cache_mem_v[l], seg=seg)
    x2s = _cast_linear("xattn_o_s", att, w_xo[l], x1s, tm=n_s)
    buf = state_conv[l]
    ys, a_new = _ffn_step(x2s, norm_ffn[l], w_up[l], conv_w[l], conv_b[l], w_down[l], norm_final,
                          buf[:, 0], buf[:, 1], seg=seg)

    s_re = hs_r.reshape(1, bs, g_total, n_state)
    s_im = hs_i.reshape(1, bs, g_total, n_state)
    s_conv = jnp.swapaxes(a_new, 0, 1)[None]

    return (yp.reshape(bp, seq, dm), ys.reshape(bs, seg, dm), mk, mv,
            p_re, p_im, p_gla, p_conv, s_re, s_im, s_gla[None], s_conv)
```

```python
import functools

import jax
import jax.numpy as jnp
from jax import lax
from jax.experimental import pallas as pl
from jax.experimental.pallas import tpu as pltpu

F32 = jnp.float32
BF16 = jnp.bfloat16

EPS = 1e-6
S5_GROUP = 16
S5_STATE = 64
S5_BLOCK_GROUPS = 16
GLA_HEADS = 4
GLA_RANK = 16
GLA_TAU = 16.0
GLA_CHUNK = 64
X_HEADS = 4
CONV_W = 3

SUBLANES = 8
LANES = 128
VMEM_LIMIT_BYTES = 56 * 1024 * 1024

_CONTRACT_LAST = (((1,), (1,)), ((), ()))
_CONTRACT_FIRST = (((0,), (0,)), ((), ()))


def _params(*semantics):
    return pltpu.CompilerParams(dimension_semantics=semantics, vmem_limit_bytes=VMEM_LIMIT_BYTES)


def _dot(a, b):
    return jnp.dot(a, b, preferred_element_type=F32)


def _for_row_chunks(n_rows, chunk, fn):
    def body(i, carry):
        fn(pl.multiple_of(i * chunk, chunk))
        return carry
    lax.fori_loop(0, n_rows // chunk, body, 0)


def _rms(x, gain):
    r = lax.rsqrt(jnp.mean(x * x, axis=-1, keepdims=True) + EPS)
    return (x * r) * gain


def _gelu(x):
    return 0.5 * x * (1.0 + lax.erf(x * (2.0 ** -0.5)))


def _rms_rows_to_bf16(x_ref, gain_ref, h_ref, n_rows, row0=0, chunk=128):
    def f(r0):
        h_ref[pl.ds(row0 + r0, chunk), :] = _rms(x_ref[pl.ds(r0, chunk), :], gain_ref[...]).astype(BF16)
    _for_row_chunks(n_rows, chunk, f)


def _s5_prep_body(lr_ref, li_ref, ldt_ref, br_ref, bi_ref, apr_ref, api_ref, bbr_ref, bbi_ref):
    lr = lr_ref[...]
    li = li_ref[...]
    dt = jnp.exp(ldt_ref[...])
    mag = jnp.exp(lr * dt)
    ar = mag * jnp.cos(li * dt)
    ai = mag * jnp.sin(li * dt)
    den = lr * lr + li * li
    pr = ar - 1.0
    qr = (pr * lr + ai * li) / den
    qi = (ai * lr - pr * li) / den
    for p in range(S5_GROUP):
        br = br_ref[p]
        bi = bi_ref[p]
        bbr_ref[p] = qr * br - qi * bi
        bbi_ref[p] = qr * bi + qi * br
    cr, ci = ar, ai
    apr_ref[0] = cr
    api_ref[0] = ci
    for k in range(1, SUBLANES):
        cr, ci = cr * ar - ci * ai, cr * ai + ci * ar
        apr_ref[k] = cr
        api_ref[k] = ci


def _s5_prep(lam_re, lam_im, log_dt, b_re, b_im):
    g, n = lam_re.shape
    out = jax.ShapeDtypeStruct((SUBLANES, g, n), F32)
    outb = jax.ShapeDtypeStruct((S5_GROUP, g, n), F32)
    return pl.pallas_call(
        _s5_prep_body, out_shape=(out, out, outb, outb), name="s5_prep",
    )(lam_re, lam_im, log_dt.reshape(g, 1), jnp.transpose(b_re, (2, 0, 1)), jnp.transpose(b_im, (2, 0, 1)))


def _s5_weights(lam_re, lam_im, log_dt, b_re, b_im, c_re, c_im, d, w_glu, b_glu):
    g, n = lam_re.shape
    p = S5_GROUP
    gb = S5_BLOCK_GROUPS
    nb = g // gb
    apr, api, bbr, bbi = _s5_prep(lam_re, lam_im, log_dt, b_re, b_im)
    eye = jnp.eye(gb, dtype=F32)

    def bu_block(bb):
        return jnp.einsum('pbgn,gh->bgphn', bb.reshape(p, nb, gb, n), eye).reshape(nb, gb * p, gb * n)

    def c_block(c):
        return jnp.einsum('bgpn,gh->bgnhp', c.reshape(nb, gb, p, n), eye).reshape(nb, gb * n, gb * p)

    w_bu = jnp.concatenate([bu_block(bbr), bu_block(bbi)], axis=-1).astype(BF16)
    w_c = jnp.concatenate([c_block(c_re), -c_block(c_im)], axis=1).astype(BF16)
    w_g = jnp.einsum('bgpq,gh->bgphq', w_glu.reshape(nb, gb, p, p), eye).reshape(nb, gb * p, gb * p).astype(BF16)
    d_b = d.reshape(nb, 1, gb * p)
    bg_b = b_glu.reshape(nb, 1, gb * p)
    pw_r = apr.reshape(SUBLANES, nb, gb * n)
    pw_i = api.reshape(SUBLANES, nb, gb * n)
    return (w_bu, w_c, w_g, d_b, bg_b), (pw_r, pw_i)


def _s5_scan_table(powers, steps, rows_per_step):
    pw_r, pw_i = powers
    t = (jnp.arange(SUBLANES) // rows_per_step) % steps

    def masked(k):
        m = (t >= k)[:, None, None]
        return jnp.where(m, pw_r[k - 1][None], 0.0), jnp.where(m, pw_i[k - 1][None], 0.0)

    tabs = []
    k = 1
    while k < steps:
        tabs.extend(masked(k))
        k *= 2
    tabs.append(pw_r[t])
    tabs.append(pw_i[t])
    whole = jnp.full((SUBLANES,), steps - 1)
    tabs.append(pw_r[whole])
    tabs.append(pw_i[whole])
    return jnp.transpose(jnp.stack(tabs), (2, 0, 1, 3))


def _in_proj_body(x_ref, gain_ref, w_ref, wr_ref, wg_ref, wg2_ref, bg_ref, o_ref, h_ref, *, tm, n_main, n_r):
    j = pl.program_id(1)

    @pl.when(j == 0)
    def _():
        _rms_rows_to_bf16(x_ref, gain_ref, h_ref, tm)

    @pl.when(j < n_main)
    def _():
        o_ref[...] = _dot(h_ref[...], w_ref[...].astype(BF16))

    @pl.when((j >= n_main) & (j < n_main + n_r))
    def _():
        o_ref[...] = _dot(h_ref[...], wr_ref[...].astype(BF16))

    @pl.when(j == n_main + n_r)
    def _():
        g_low = _dot(h_ref[...], wg_ref[...].astype(BF16))
        pre = _dot(g_low.astype(BF16), wg2_ref[...].astype(BF16)) + bg_ref[...]
        log_sig = jnp.minimum(pre, 0.0) - jnp.log1p(jnp.exp(-jnp.abs(pre)))
        o_ref[...] = log_sig * (1.0 / GLA_TAU)


def _in_proj(x, gain, w_in, w_g2, b_g, *, tm, tn=512):
    n, dm = x.shape
    mix = dm
    n_u_q_k_v = mix // 2 + 2 * (mix // 4) + mix // 2
    col_g = n_u_q_k_v
    col_r = n_u_q_k_v + GLA_RANK
    w_r = w_in[:, col_r:]
    w_g = w_in[:, col_g:col_r]
    n_main = n_u_q_k_v // tn
    n_r = w_r.shape[1] // tn
    n_cols = n_u_q_k_v + w_r.shape[1] + w_g2.shape[1]
    body = functools.partial(_in_proj_body, tm=tm, n_main=n_main, n_r=n_r)
    return pl.pallas_call(
        body,
        grid=(n // tm, n_main + n_r + 1),
        in_specs=[
            pl.BlockSpec((tm, dm), lambda i, j: (i, 0)),
            pl.BlockSpec((1, dm), lambda i, j: (0, 0)),
            pl.BlockSpec((dm, tn), lambda i, j: (0, jnp.minimum(j, n_main - 1))),
            pl.BlockSpec((dm, tn), lambda i, j: (0, jnp.clip(j - n_main, 0, n_r - 1))),
            pl.BlockSpec((dm, GLA_RANK), lambda i, j: (0, 0)),
            pl.BlockSpec(w_g2.shape, lambda i, j: (0, 0)),
            pl.BlockSpec((1, w_g2.shape[1]), lambda i, j: (0, 0)),
        ],
        out_specs=pl.BlockSpec((tm, tn), lambda i, j: (i, j)),
        out_shape=jax.ShapeDtypeStruct((n, n_cols), F32),
        scratch_shapes=[pltpu.VMEM((tm, dm), BF16)],
        compiler_params=_params("arbitrary", "arbitrary"),
        name="in_proj",
    )(x, gain.reshape(1, dm), w_in, w_r, w_g, w_g2, b_g.reshape(1, -1))


def _s5_body(*refs, tt, steps, nseq, ns, chunk):
    carried = nseq > 1
    if carried:
        (u_ref, wbu_ref, wc_ref, wg_ref, d_ref, bg_ref, tab_ref, z_ref, hr_ref, hi_ref,
         bu_ref, cr_ref, ci_ref, us_ref, zs_ref) = refs
    else:
        u_ref, wbu_ref, wc_ref, wg_ref, d_ref, bg_ref, tab_ref, h0r_ref, h0i_ref, z_ref, hr_ref, hi_ref, bu_ref = refs
    seg = steps
    n_lvl = steps.bit_length() - 1
    lane_group = 2 * LANES
    rid = lax.broadcasted_iota(jnp.int32, (SUBLANES, 1), 0)
    cw = u_ref.shape[-1]
    n_lt = cw // LANES

    if carried:
        @pl.when(pl.program_id(1) == 0)
        def _():
            cr_ref[...] = jnp.zeros_like(cr_ref)
            ci_ref[...] = jnp.zeros_like(ci_ref)

        per = chunk // nseq

    def chunk_u(c):
        if not carried:
            return u_ref[c * chunk:(c + 1) * chunk, :]
        for b in range(nseq):
            for lt in range(n_lt):
                us_ref[c * n_lt + lt, pl.ds(b, per, stride=nseq), :] = (
                    u_ref[b, c * per:(c + 1) * per, lt * LANES:(lt + 1) * LANES])
        return jnp.concatenate([us_ref[c * n_lt + lt] for lt in range(n_lt)], axis=1)

    def project_in(c):
        rows = slice(c * chunk, (c + 1) * chunk)
        bu = _dot(chunk_u(c).astype(BF16), wbu_ref[...])
        if not carried:
            seqs = slice(c * chunk // seg, (c + 1) * chunk // seg)
            a_r = tab_ref[2 * n_lvl, 0:1, :]
            a_i = tab_ref[2 * n_lvl + 1, 0:1, :]
            h0r = h0r_ref[seqs, :]
            h0i = h0i_ref[seqs, :]
            first = (lax.broadcasted_iota(jnp.int32, (chunk, chunk // seg), 0)
                     == seg * lax.broadcasted_iota(jnp.int32, (chunk, chunk // seg), 1))
            first_b = jnp.where(first, 1.0, 0.0).astype(BF16)
            bu = bu + jnp.concatenate([_dot_exact_rows(first_b, a_r * h0r - a_i * h0i),
                                       _dot_exact_rows(first_b, a_r * h0i + a_i * h0r)], axis=1)
        bu_ref[rows, :] = bu

    def scan(c, carry):
        carry = list(carry)
        for blk in range(chunk // SUBLANES):
            rows = slice(c * chunk + blk * SUBLANES, c * chunk + (blk + 1) * SUBLANES)
            for gi in range(ns // lane_group):
                cols = slice(gi * lane_group, (gi + 1) * lane_group)
                cols_im = slice(ns + gi * lane_group, ns + (gi + 1) * lane_group)
                xr = bu_ref[rows, cols]
                xi = bu_ref[rows, cols_im]
                for lvl in range(n_lvl):
                    k = 1 << lvl
                    ar = tab_ref[2 * lvl, :, cols]
                    ai = tab_ref[2 * lvl + 1, :, cols]
                    sr = pltpu.roll(xr, k * nseq, 0)
                    si = pltpu.roll(xi, k * nseq, 0)
                    xr, xi = xr + (ar * sr - ai * si), xi + (ar * si + ai * sr)
                if carried:
                    pr = tab_ref[2 * n_lvl, :, cols]
                    pi = tab_ref[2 * n_lvl + 1, :, cols]
                    c_r, c_i = carry[gi]
                    hr = xr + (pr * c_r - pi * c_i)
                    hi = xi + (pr * c_i + pi * c_r)
                    lr, li = xr, xi
                    half = SUBLANES // 2
                    while half >= nseq:
                        keep = (rid & (2 * half - 1)) >= half
                        lr = jnp.where(keep, lr, pltpu.roll(lr, SUBLANES - half, 0))
                        li = jnp.where(keep, li, pltpu.roll(li, SUBLANES - half, 0))
                        half //= 2
                    er = tab_ref[2 * n_lvl + 2, :, cols]
                    ei = tab_ref[2 * n_lvl + 3, :, cols]
                    carry[gi] = (lr + (er * c_r - ei * c_i), li + (er * c_i + ei * c_r))
                else:
                    hr, hi = xr, xi
                    for e in range(SUBLANES // seg):
                        src = e * seg + seg - 1
                        dst = (c * chunk + blk * SUBLANES) // seg + e
                        hr_ref[dst:dst + 1, cols] = hr[src:src + 1]
                        hi_ref[dst:dst + 1, cols] = hi[src:src + 1]
                bu_ref[rows, cols] = hr
                bu_ref[rows, cols_im] = hi
        return carry

    def project_out(c):
        rows = slice(c * chunk, (c + 1) * chunk)
        if carried:
            u = jnp.concatenate([us_ref[c * n_lt + lt] for lt in range(n_lt)], axis=1)
        else:
            u = u_ref[rows, :]
        y = _dot(bu_ref[rows, :].astype(BF16), wc_ref[...]) + d_ref[...] * u
        z = _gelu(y)
        z = z * jax.nn.sigmoid(_dot(z.astype(BF16), wg_ref[...]) + bg_ref[...])
        if carried:
            for lt in range(n_lt):
                zs_ref[c * n_lt + lt] = z[:, lt * LANES:(lt + 1) * LANES]
            for b in range(nseq):
                for lt in range(n_lt):
                    z_ref[b, c * per:(c + 1) * per, lt * LANES:(lt + 1) * LANES] = (
                        zs_ref[c * n_lt + lt, pl.ds(b, per, stride=nseq), :])
        else:
            z_ref[rows, :] = z

    n_chunks = tt // chunk
    n_groups = ns // lane_group
    group_cols = [slice(gi * lane_group, (gi + 1) * lane_group) for gi in range(n_groups)]
    carry = [(cr_ref[:, cols], ci_ref[:, cols]) for cols in group_cols] if carried else [None] * n_groups
    project_in(0)
    for c in range(n_chunks):
        if c + 1 < n_chunks:
            project_in(c + 1)
        carry = scan(c, carry)
        project_out(c)
    if carried:
        for cols, (c_r, c_i) in zip(group_cols, carry):
            cr_ref[:, cols] = c_r
            ci_ref[:, cols] = c_i
            hr_ref[:, cols] = c_r
            hi_ref[:, cols] = c_i


def _s5_mixer(proj3, weights, tab, *, tt, steps, h0=None, chunk=128):
    w_bu, w_c, w_g, d_b, bg_b = weights
    bsz, seq, _ = proj3.shape
    nb, cw, ns2 = w_bu.shape
    ns = ns2 // 2
    n_tab = tab.shape[1]
    carried = h0 is None
    nseq = bsz if carried else 1
    assert nseq * steps == SUBLANES or not carried
    rows = nseq * tt
    body = functools.partial(_s5_body, tt=rows, steps=steps, nseq=nseq, ns=ns, chunk=min(chunk, rows))
    wspec = lambda shape: pl.BlockSpec((None,) + shape, lambda g, t: (g,) + (0,) * len(shape))
    lead = bsz if carried else None
    uz_spec = pl.BlockSpec((lead, tt, cw), lambda g, t: (0, t, g))
    in_specs = [
        uz_spec,
        wspec((cw, ns2)), wspec((ns2, cw)), wspec((cw, cw)), wspec((1, cw)), wspec((1, cw)),
        wspec((n_tab, SUBLANES, ns)),
    ]
    args = [proj3, w_bu, w_c, w_g, d_b, bg_b, tab]
    z_shape = jax.ShapeDtypeStruct((bsz, seq, nb * cw), F32)
    scratch = [pltpu.VMEM((rows, ns2), F32)]
    if carried:
        st_shape = jax.ShapeDtypeStruct((nb, SUBLANES, ns), F32)
        st_spec = pl.BlockSpec((None, SUBLANES, ns), lambda g, t: (g, 0, 0))
        slabs = (rows // min(chunk, rows)) * (cw // LANES)
        scratch += [pltpu.VMEM((SUBLANES, ns), F32), pltpu.VMEM((SUBLANES, ns), F32),
                    pltpu.VMEM((slabs, min(chunk, rows), LANES), F32), pltpu.VMEM((slabs, min(chunk, rows), LANES), F32)]
    else:
        assert bsz == 1
        h0r, h0i = h0
        st_spec = pl.BlockSpec((None, tt // steps, ns), lambda g, t: (0, t, g))
        in_specs += [st_spec] * 2
        args += [h0r, h0i]
        st_shape = jax.ShapeDtypeStruct((bsz, seq // steps, nb * ns), F32)
    return pl.pallas_call(
        body,
        grid=(nb, seq // tt),
        in_specs=in_specs,
        out_specs=(uz_spec, st_spec, st_spec),
        out_shape=(z_shape, st_shape, st_shape),
        scratch_shapes=scratch,
        compiler_params=_params("arbitrary", "arbitrary"),
        name="s5_carried" if carried else "s5_segments",
    )(*args)


def _split3_bf16(x):
    hi = x.astype(BF16)
    r1 = x - hi.astype(F32)
    mid = r1.astype(BF16)
    lo = (r1 - mid.astype(F32)).astype(BF16)
    return hi, mid, lo


def _gla_out(o, gain, r):
    return _rms(o, gain) * (r * jax.nn.sigmoid(r))


def _gla_seq_body(q_ref, k_ref, v_ref, la_ref, r_ref, gain_ref, y_ref, st_out_ref, st_ref, *, tt, dk):
    c = GLA_CHUNK

    @pl.when(pl.program_id(2) == 0)
    def _():
        st_ref[...] = jnp.zeros_like(st_ref)

    row = lax.broadcasted_iota(jnp.int32, (tt, tt), 0)
    col = lax.broadcasted_iota(jnp.int32, (tt, tt), 1)
    shift = c.bit_length() - 1
    tri = (col <= row) & ((row >> shift) == (col >> shift))
    tri_b = jnp.where(tri, 1.0, 0.0).astype(BF16)
    hi, mid, lo = _split3_bf16(la_ref[...])
    bcum = _dot(tri_b, hi) + _dot(tri_b, mid) + _dot(tri_b, lo)
    causal = lax.broadcasted_iota(jnp.int32, (c, c), 1) <= lax.broadcasted_iota(jnp.int32, (c, c), 0)
    scale = dk ** -0.5
    chunks = [slice(ci * c, (ci + 1) * c) for ci in range(tt // c)]
    qts, decays, o_intra, kvs = [], [], [], []
    for sl in chunks:
        bc = bcum[sl]
        bl = bc[c - 1:c]
        qt = ((q_ref[sl, :] * scale) * jnp.exp(bc)).astype(BF16)
        kc = k_ref[sl, :]
        kt = (kc * jnp.exp(-bc)).astype(BF16)
        ke = (kc * jnp.exp(bl - bc)).astype(BF16)
        vb = v_ref[sl, :].astype(BF16)
        s = lax.dot_general(qt, kt, _CONTRACT_LAST, preferred_element_type=F32)
        s = jnp.where(causal, s, 0.0)
        qts.append(qt)
        decays.append(jnp.exp(bl))
        o_intra.append(_dot(s.astype(BF16), vb))
        kvs.append(lax.dot_general(vb, ke, _CONTRACT_FIRST, preferred_element_type=F32))
    st = st_ref[...]
    for sl, qt, dec, oi, kv in zip(chunks, qts, decays, o_intra, kvs):
        o = oi + lax.dot_general(qt, st.astype(BF16), _CONTRACT_LAST, preferred_element_type=F32)
        st = st * dec + kv
        y_ref[sl, :] = _gla_out(o, gain_ref[...], r_ref[sl, :]).astype(y_ref.dtype)
    st_ref[...] = st
    st_out_ref[...] = st


def _gla_seq(proj3, gain, *, tt):
    bsz, seq, _ = proj3.shape
    hh = GLA_HEADS
    dv = gain.shape[0]
    dk = dv // 2
    col_q = 1024 // dk
    col_k = (1024 + hh * dk) // dk
    col_v = (1024 + 2 * hh * dk) // dv
    col_r = (1024 + 2 * hh * dk + hh * dv) // dv
    col_la = (1024 + 2 * hh * dk + 2 * hh * dv) // dk
    spec = lambda w, c0: pl.BlockSpec((None, tt, w), lambda b, h, t: (b, t, c0 + h))
    body = functools.partial(_gla_seq_body, tt=tt, dk=dk)
    return pl.pallas_call(
        body,
        grid=(bsz, hh, seq // tt),
        in_specs=[spec(dk, col_q), spec(dk, col_k), spec(dv, col_v), spec(dk, col_la), spec(dv, col_r),
                  pl.BlockSpec((1, dv), lambda b, h, t: (0, 0))],
        out_specs=(pl.BlockSpec((None, tt, dv), lambda b, h, t: (b, t, h)),
                   pl.BlockSpec((None, None, dv, dk), lambda b, h, t: (b, h, 0, 0))),
        out_shape=(jax.ShapeDtypeStruct((bsz, seq, hh * dv), BF16),
                   jax.ShapeDtypeStruct((bsz, hh, dv, dk), F32)),
        scratch_shapes=[pltpu.VMEM((dv, dk), F32)],
        compiler_params=_params("arbitrary", "arbitrary", "arbitrary"),
        name="gla_seq",
    )(proj3, proj3, proj3, proj3, proj3, gain.reshape(1, dv))


def _gla_step_body(q_ref, k_ref, v_ref, la_ref, r_ref, gain_ref, s_ref, y_ref, so_ref, *, nb, seg, dk, dv):
    hh = GLA_HEADS
    rows = SUBLANES
    per = rows // seg
    row = lax.broadcasted_iota(jnp.int32, (rows, rows), 0)
    col = lax.broadcasted_iota(jnp.int32, (rows, rows), 1)
    shift = seg.bit_length() - 1
    causal = (col <= row) & ((row >> shift) == (col >> shift))
    rid = lax.broadcasted_iota(jnp.int32, (rows, 1), 0)
    t_in = rid & (seg - 1)
    ones_b = jnp.ones((rows, dv), BF16)
    scale = dk ** -0.5

    def pair(i, carry):
        r0 = pl.multiple_of(i * rows, rows)
        prep = []
        for h in range(hh):
            ks = slice(h * dk, (h + 1) * dk)
            vs = slice(h * dv, (h + 1) * dv)
            bc = la_ref[pl.ds(r0, rows), ks]
            k = 1
            while k < seg:
                bc = bc + jnp.where(t_in >= k, pltpu.roll(bc, k, 0), 0.0)
                k *= 2
            bl = jnp.zeros_like(bc)
            for e in range(per):
                last = e * seg + seg - 1
                bl = jnp.where((rid >> shift) == e, bc[last:last + 1], bl)
            qt = ((q_ref[pl.ds(r0, rows), ks] * scale) * jnp.exp(bc)).astype(BF16)
            kc = k_ref[pl.ds(r0, rows), ks]
            kt = (kc * jnp.exp(-bc)).astype(BF16)
            ke = kc * jnp.exp(bl - bc)
            vb = v_ref[pl.ds(r0, rows), vs].astype(BF16)
            per_seq = []
            for e in range(per):
                last = e * seg + seg - 1
                dec = jnp.exp(bc[last:last + 1])
                d_hi, d_mid, d_lo = _split3_bf16(dec)
                d3 = jnp.where(rid == 0, d_hi.astype(F32),
                               jnp.where(rid == 1, d_mid.astype(F32),
                                         jnp.where(rid == 2, d_lo.astype(F32), 0.0))).astype(BF16)
                ke_e = jnp.where((rid >> shift) == e, ke, 0.0).astype(BF16)
                per_seq.append((d3, ke_e, s_ref[i * per + e, h]))
            prep.append((vs, qt, kt, vb, per_seq))
        scores = [lax.dot_general(qt, kt, _CONTRACT_LAST, preferred_element_type=F32) for _, qt, kt, _, _ in prep]
        inter, dec_cols, kvs = [], [], []
        for _, qt, _, vb, per_seq in prep:
            for d3, ke_e, st in per_seq:
                inter.append(_dot(qt, st.astype(BF16)))
                dec_cols.append(lax.dot_general(d3, ones_b, _CONTRACT_FIRST, preferred_element_type=F32))
                kvs.append(lax.dot_general(ke_e, vb, _CONTRACT_FIRST, preferred_element_type=F32))
        intra = [_dot(jnp.where(causal, s, 0.0).astype(BF16), vb) for s, (_, _, _, vb, _) in zip(scores, prep)]
        for h, (vs, _, _, _, per_seq) in enumerate(prep):
            o_inter = inter[h * per]
            for e in range(per):
                if e:
                    o_inter = jnp.where((rid >> shift) == e, inter[h * per + e], o_inter)
                so_ref[i * per + e, h] = dec_cols[h * per + e] * per_seq[e][2] + kvs[h * per + e]
            y_ref[pl.ds(r0, rows), vs] = _gla_out(intra[h] + o_inter, gain_ref[...], r_ref[pl.ds(r0, rows), vs])
        return carry

    lax.fori_loop(0, nb // per, pair, 0)


def _gla_step(proj, gain, state, *, seg, nb=8):
    n = proj.shape[0]
    bsz, hh, dk, dv = state.shape
    rows = nb * seg
    c_q = 1024 // (hh * dk)
    c_k = c_q + 1
    c_v = (1024 + 2 * hh * dk) // (hh * dv)
    c_r = c_v + 1
    c_la = (1024 + 2 * hh * dk + 2 * hh * dv) // (hh * dk)
    spec = lambda w, c0: pl.BlockSpec((rows, w), lambda i: (i, c0))
    body = functools.partial(_gla_step_body, nb=nb, seg=seg, dk=dk, dv=dv)
    return pl.pallas_call(
        body,
        grid=(bsz // nb,),
        in_specs=[spec(hh * dk, c_q), spec(hh * dk, c_k), spec(hh * dv, c_v), spec(hh * dk, c_la), spec(hh * dv, c_r),
                  pl.BlockSpec((1, dv), lambda i: (0, 0)),
                  pl.BlockSpec((nb, hh, dk, dv), lambda i: (i, 0, 0, 0))],
        out_specs=(pl.BlockSpec((rows, hh * dv), lambda i: (i, 0)),
                   pl.BlockSpec((nb, hh, dk, dv), lambda i: (i, 0, 0, 0))),
        out_shape=(jax.ShapeDtypeStruct((n, hh * dv), F32), jax.ShapeDtypeStruct(state.shape, F32)),
        compiler_params=_params("arbitrary"),
        name="gla_step",
    )(proj, proj, proj, proj, proj, gain.reshape(1, dv), state)


def _tiled_linear(name, prologue, row_ins, aux_ins, aux_specs, w, *, tm, tn, residual=None):
    n = row_ins[0].shape[0]
    k_dim, n_cols = w.shape
    nr, na = len(row_ins), len(aux_ins)
    has_res = residual is not None

    def body(*refs):
        row_refs = refs[:nr]
        aux_refs = refs[nr:nr + na]
        w_ref = refs[nr + na]
        o_ref, h_ref = refs[-2], refs[-1]

        @pl.when(pl.program_id(1) == 0)
        def _():
            prologue(row_refs, aux_refs, h_ref)

        acc = _dot(h_ref[...], w_ref[...].astype(BF16))
        if has_res:
            acc = acc + refs[nr + na + 1][...]
        o_ref[...] = acc

    in_specs = [pl.BlockSpec((tm, a.shape[1]), lambda i, j: (i, 0)) for a in row_ins]
    in_specs += list(aux_specs)
    in_specs.append(pl.BlockSpec((k_dim, tn), lambda i, j: (0, j)))
    args = list(row_ins) + list(aux_ins) + [w]
    if has_res:
        in_specs.append(pl.BlockSpec((tm, tn), lambda i, j: (i, j)))
        args.append(residual)
    return pl.pallas_call(
        body,
        grid=(n // tm, n_cols // tn),
        in_specs=in_specs,
        out_specs=pl.BlockSpec((tm, tn), lambda i, j: (i, j)),
        out_shape=jax.ShapeDtypeStruct((n, n_cols), F32),
        scratch_shapes=[pltpu.VMEM((tm, k_dim), BF16)],
        compiler_params=_params("arbitrary", "arbitrary"),
        name=name,
    )(*args)


def _whole(a):
    return pl.BlockSpec(a.shape, lambda i, j: (0,) * a.ndim)


def _norm_linear(name, x, gain, w, *, tm, tn=512):
    def prologue(row_refs, aux_refs, h_ref):
        _rms_rows_to_bf16(row_refs[0], aux_refs[0], h_ref, tm)
    g = gain.reshape(1, -1)
    return _tiled_linear(name, prologue, [x], [g], [_whole(g)], w, tm=tm, tn=tn)


def _cast_linear(name, x, w, residual, *, tm, tn=512):
    def prologue(row_refs, aux_refs, h_ref):
        def f(r0):
            h_ref[pl.ds(r0, 128), :] = row_refs[0][pl.ds(r0, 128), :].astype(BF16)
        _for_row_chunks(tm, 128, f)
    return _tiled_linear(name, prologue, [x], [], [], w, tm=tm, tn=tn, residual=residual)


def _out_proj(z_s5, y_gla, gain_s5, w_out, x, *, tm, tn=512):
    ws = z_s5.shape[1]

    def prologue(row_refs, aux_refs, h_ref):
        def f(r0):
            h_ref[pl.ds(r0, 128), 0:ws] = _rms(row_refs[0][pl.ds(r0, 128), :], aux_refs[0][...]).astype(BF16)
            h_ref[pl.ds(r0, 128), ws:] = row_refs[1][pl.ds(r0, 128), :].astype(BF16)
        _for_row_chunks(tm, 128, f)
    g = gain_s5.reshape(1, -1)
    return _tiled_linear("out_proj", prologue, [z_s5, y_gla], [g], [_whole(g)], w_out, tm=tm, tn=tn, residual=x)


def _attend(q, kb, vb, scale):
    s = lax.dot_general(q, kb, _CONTRACT_LAST, preferred_element_type=F32) * scale
    e = jnp.exp(s - jnp.max(s, axis=-1, keepdims=True))
    p = e / jnp.sum(e, axis=-1, keepdims=True)
    return _dot(p.astype(BF16), vb)


def _mix_xattn_seq_body(z_ref, y_ref, xres_ref, gs5_ref, gx_ref, wout_ref, wq_ref, wo_ref, mk_ref, mv_ref,
                        o_ref, x1_ref, h_ref, q_ref, *, tm, tn, nt, hd, chunk):
    j = pl.program_id(1)
    ws = z_ref.shape[1]
    scale = hd ** -0.5

    @pl.when(j == 0)
    def _():
        def f(r0):
            h_ref[pl.ds(r0, 128), 0:ws] = _rms(z_ref[pl.ds(r0, 128), :], gs5_ref[...]).astype(BF16)
            h_ref[pl.ds(r0, 128), ws:] = y_ref[pl.ds(r0, 128), :].astype(BF16)
        _for_row_chunks(tm, 128, f)

    @pl.when(j == nt)
    def _():
        _rms_rows_to_bf16(x1_ref, gx_ref, h_ref, tm)

    @pl.when(j == 2 * nt)
    def _():
        heads = [slice(h * hd, (h + 1) * hd) for h in range(X_HEADS)]

        def f(r0):
            scores = [lax.dot_general(q_ref[pl.ds(r0, chunk), hs], mk_ref[:, hs], _CONTRACT_LAST,
                                      preferred_element_type=F32) * scale for hs in heads]
            probs = []
            for s in scores:
                e = jnp.exp(s - jnp.max(s, axis=-1, keepdims=True))
                probs.append((e / jnp.sum(e, axis=-1, keepdims=True)).astype(BF16))
            for hs, p in zip(heads, probs):
                h_ref[pl.ds(r0, chunk), hs] = _dot(p, mv_ref[:, hs]).astype(BF16)
        _for_row_chunks(tm, chunk, f)

    for jj in range(nt):
        cols = slice(jj * tn, (jj + 1) * tn)

        @pl.when(j == jj)
        def _(cols=cols):
            x1_ref[:, cols] = _dot(h_ref[...], wout_ref[...]) + xres_ref[...]

        @pl.when(j == nt + jj)
        def _(cols=cols):
            q_ref[:, cols] = _dot(h_ref[...], wq_ref[...]).astype(BF16)

        @pl.when(j == 2 * nt + jj)
        def _(cols=cols):
            o_ref[...] = _dot(h_ref[...], wo_ref[...]) + x1_ref[:, cols]


def _mix_xattn_seq(z_s5, y_gla, x, gain_s5, gain_x, w_out, w_xq, w_xo, mk_b, mv_b, *, tm, seq, tn=512, chunk=256):
    n, dm = x.shape
    nt = dm // tn
    tiles_per_seq = seq // tm
    body = functools.partial(_mix_xattn_seq_body, tm=tm, tn=tn, nt=nt, hd=dm // X_HEADS, chunk=chunk)
    gs = gain_s5.reshape(1, -1)
    gx = gain_x.reshape(1, -1)
    phase = lambda p: (lambda i, j: (0, jnp.clip(j - p * nt, 0, nt - 1)))
    mspec = pl.BlockSpec((None,) + mk_b.shape[1:], lambda i, j: (i // tiles_per_seq, 0, 0))
    return pl.pallas_call(
        body,
        grid=(n // tm, 3 * nt),
        in_specs=[
            pl.BlockSpec((tm, z_s5.shape[1]), lambda i, j: (i, 0)),
            pl.BlockSpec((tm, y_gla.shape[1]), lambda i, j: (i, 0)),
            pl.BlockSpec((tm, tn), lambda i, j: (i, jnp.minimum(j, nt - 1))),
            _whole(gs), _whole(gx),
            pl.BlockSpec((dm, tn), phase(0)), pl.BlockSpec((dm, tn), phase(1)), pl.BlockSpec((dm, tn), phase(2)),
            mspec, mspec,
        ],
        out_specs=pl.BlockSpec((tm, tn), lambda i, j: (i, jnp.clip(j - 2 * nt, 0, nt - 1))),
        out_shape=jax.ShapeDtypeStruct((n, dm), F32),
        scratch_shapes=[pltpu.VMEM((tm, dm), F32), pltpu.VMEM((tm, dm), BF16), pltpu.VMEM((tm, dm), BF16)],
        compiler_params=_params("arbitrary", "arbitrary"),
        name="mix_xattn_seq",
    )(z_s5, y_gla, x, gs, gx, w_out, w_xq, w_xo, mk_b, mv_b)


def _cache_rows(c):
    b, m, h, hd = c.shape
    n_chunk = hd // LANES
    return c.reshape(b, m, h, n_chunk, LANES).transpose(0, 1, 3, 2, 4).reshape(b * m * n_chunk * h, LANES)


def _mem_proj_body(x_ref, gain_ref, w_ref, o_ref, ob_ref, h_ref, *, tm, hd):
    j = pl.program_id(1)

    @pl.when(j == 0)
    def _():
        _rms_rows_to_bf16(x_ref, gain_ref, h_ref, tm)

    y = _dot(h_ref[...], w_ref[...].astype(BF16))
    ob_ref[...] = y.astype(BF16)
    n_chunk = hd // LANES
    pitch = n_chunk * X_HEADS
    for ck in range(n_chunk):
        o_ref[pl.ds(ck * X_HEADS + j, tm, stride=pitch), :] = y[:, ck * LANES:(ck + 1) * LANES]


def _mem_proj(name, mem, gain, w, *, tm):
    b, m, dm = mem.shape
    n = b * m
    hd = dm // X_HEADS
    n_chunk = hd // LANES
    pitch = n_chunk * X_HEADS
    body = functools.partial(_mem_proj_body, tm=tm, hd=hd)
    g = gain.reshape(1, dm)
    rows, plain = pl.pallas_call(
        body,
        grid=(n // tm, X_HEADS),
        in_specs=[pl.BlockSpec((tm, dm), lambda i, j: (i, 0)), _whole(g), pl.BlockSpec((dm, hd), lambda i, j: (0, j))],
        out_specs=(pl.BlockSpec((tm * pitch, LANES), lambda i, j: (i, 0)),
                   pl.BlockSpec((tm, hd), lambda i, j: (i, j))),
        out_shape=(jax.ShapeDtypeStruct((n * pitch, LANES), F32), jax.ShapeDtypeStruct((n, dm), BF16)),
        scratch_shapes=[pltpu.VMEM((tm, dm), BF16)],
        compiler_params=_params("arbitrary", "arbitrary"),
        name=name,
    )(mem.reshape(n, dm), g, w)
    full = rows.reshape(b, m, n_chunk, X_HEADS, LANES).transpose(0, 1, 3, 2, 4).reshape(1, b, m, X_HEADS, hd)
    return full, plain.reshape(b, m, dm)


def _xattn_step_body(q_ref, k_ref, v_ref, o_ref, *, nb, seg, hd, n_mem):
    rows = SUBLANES
    per = rows // seg
    shift = seg.bit_length() - 1
    rid = lax.broadcasted_iota(jnp.int32, (rows, 1), 0)
    scale = hd ** -0.5
    n_chunk = hd // LANES
    pitch = n_chunk * X_HEADS

    def head_rows(ref, b, h):
        parts = [ref[pl.ds(b * n_mem * pitch + ck * X_HEADS + h, n_mem, stride=pitch), :] for ck in range(n_chunk)]
        return jnp.concatenate(parts, axis=1).astype(BF16)

    pairs = [(b, h) for b in range(nb) for h in range(X_HEADS)]
    scores = []
    for b, h in pairs:
        r0 = (b // per) * rows
        q = q_ref[r0:r0 + rows, h * hd:(h + 1) * hd].astype(BF16)
        scores.append(lax.dot_general(q, head_rows(k_ref, b, h), _CONTRACT_LAST, preferred_element_type=F32) * scale)
    probs = []
    for s in scores:
        e = jnp.exp(s - jnp.max(s, axis=-1, keepdims=True))
        probs.append((e / jnp.sum(e, axis=-1, keepdims=True)).astype(BF16))
    outs = {}
    for (b, h), p in zip(pairs, probs):
        outs[b, h] = _dot(p, head_rows(v_ref, b, h))
    for g in range(nb // per):
        for h in range(X_HEADS):
            o = outs[g * per, h]
            for e in range(1, per):
                o = jnp.where((rid >> shift) == e, outs[g * per + e, h], o)
            o_ref[g * rows:(g + 1) * rows, h * hd:(h + 1) * hd] = o


def _xattn_step(q, mem_k, mem_v, *, seg, nb=4):
    n, dm = q.shape
    bsz, m, _, hd = mem_k.shape
    body = functools.partial(_xattn_step_body, nb=nb, seg=seg, hd=hd, n_mem=m)
    rows_per_seq = m * dm // LANES
    mspec = pl.BlockSpec((nb * rows_per_seq, LANES), lambda i: (i, 0))
    return pl.pallas_call(
        body,
        grid=(bsz // nb,),
        in_specs=[pl.BlockSpec((nb * seg, dm), lambda i: (i, 0)), mspec, mspec],
        out_specs=pl.BlockSpec((nb * seg, dm), lambda i: (i, 0)),
        out_shape=jax.ShapeDtypeStruct((n, dm), F32),
        compiler_params=_params("arbitrary"),
        name="xattn_step",
    )(q, _cache_rows(mem_k), _cache_rows(mem_v))


def _ffn_finish(conv, gate, wd_ref, o_ref, gfin_ref, j, nj, tm):
    act = (_gelu(conv) * gate).astype(BF16)
    o_ref[...] += _dot(act, wd_ref[...].astype(BF16))

    @pl.when(j == nj - 1)
    def _():
        def f(r0):
            o_ref[pl.ds(r0, 128), :] = _rms(o_ref[pl.ds(r0, 128), :], gfin_ref[...])
        _for_row_chunks(tm, 128, f)


def _ffn_seq_body(x_ref, gain_ref, wa_ref, wg_ref, wd_ref, cw_ref, cb_ref, gfin_ref, o_ref, tail_ref, h_ref, a_ref,
                  *, tm, halo, tiles_per_seq, nj):
    i = pl.program_id(0)
    j = pl.program_id(1)

    @pl.when(j == 0)
    def _():
        @pl.when(i % tiles_per_seq == 0)
        def _():
            h_ref[0:halo, :] = jnp.zeros((halo, h_ref.shape[1]), BF16)

        @pl.when(i % tiles_per_seq != 0)
        def _():
            h_ref[0:halo, :] = h_ref[tm:tm + halo, :]

        _rms_rows_to_bf16(x_ref, gain_ref, h_ref, tm, row0=halo)
        o_ref[...] = x_ref[...]

    a_ref[...] = _dot(h_ref[...], wa_ref[...].astype(BF16))
    gate = _dot(h_ref[halo:halo + tm, :], wg_ref[...].astype(BF16))
    conv = cb_ref[...]
    for tap in range(CONV_W):
        conv = conv + a_ref[pl.ds(halo - (CONV_W - 1) + tap, tm), :] * cw_ref[tap:tap + 1, :]
    tail_ref[...] = a_ref[pl.ds(halo + tm - SUBLANES, SUBLANES), :]
    _ffn_finish(conv, gate, wd_ref, o_ref, gfin_ref, j, nj, tm)


def _ffn_seq(x, gain, w_up, conv_w, conv_b, w_down, gain_final, *, tm, seq, tf=512, halo=16):
    n, dm = x.shape
    dff = w_down.shape[0]
    nj = dff // tf
    body = functools.partial(_ffn_seq_body, tm=tm, halo=halo, tiles_per_seq=seq // tm, nj=nj)
    return pl.pallas_call(
        body,
        grid=(n // tm, nj),
        in_specs=[
            pl.BlockSpec((tm, dm), lambda i, j: (i, 0)),
            pl.BlockSpec((1, dm), lambda i, j: (0, 0)),
            pl.BlockSpec((dm, tf), lambda i, j: (0, j)),
            pl.BlockSpec((dm, tf), lambda i, j: (0, nj + j)),
            pl.BlockSpec((tf, dm), lambda i, j: (j, 0)),
            pl.BlockSpec((CONV_W, tf), lambda i, j: (0, j)),
            pl.BlockSpec((1, tf), lambda i, j: (0, j)),
            pl.BlockSpec((1, dm), lambda i, j: (0, 0)),
        ],
        out_specs=(pl.BlockSpec((tm, dm), lambda i, j: (i, 0)),
                   pl.BlockSpec((None, SUBLANES, tf), lambda i, j: (i, 0, j))),
        out_shape=(jax.ShapeDtypeStruct((n, dm), F32),
                   jax.ShapeDtypeStruct((n // tm, SUBLANES, dff), F32)),
        scratch_shapes=[pltpu.VMEM((halo + tm, dm), BF16), pltpu.VMEM((halo + tm, tf), F32)],
        compiler_params=_params("arbitrary", "arbitrary"),
        name="ffn_seq",
    )(x, gain.reshape(1, dm), w_up, w_up, w_down, conv_w, conv_b.reshape(1, dff), gain_final.reshape(1, dm))


def _dot_exact_rows(sel_b, x):
    hi, mid, lo = _split3_bf16(x)
    return _dot(sel_b, hi) + _dot(sel_b, mid) + _dot(sel_b, lo)


def _ffn_step_body(x_ref, gain_ref, wa_ref, wg_ref, wd_ref, cw_ref, cb_ref, gfin_ref, b0_ref, b1_ref,
                   o_ref, a_new_ref, h_ref, *, tm, seg, nj):
    j = pl.program_id(1)
    nseq = tm // seg
    shift = seg.bit_length() - 1

    @pl.when(j == 0)
    def _():
        _rms_rows_to_bf16(x_ref, gain_ref, h_ref, tm)
        o_ref[...] = x_ref[...]

    a = _dot(h_ref[...], wa_ref[...].astype(BF16))
    gate = _dot(h_ref[...], wg_ref[...].astype(BF16))
    t_in = lax.broadcasted_iota(jnp.int32, (tm, 1), 0) & (seg - 1)
    prev1 = jnp.where(t_in >= 1, pltpu.roll(a, 1, 0), 0.0)
    prev2 = jnp.where(t_in >= 2, pltpu.roll(a, 2, 0), 0.0)
    conv = cb_ref[...] + prev2 * cw_ref[0:1, :]
    conv = conv + prev1 * cw_ref[1:2, :]
    conv = conv + a * cw_ref[2:3, :]
    b0 = b0_ref[...]
    b1 = b1_ref[...]
    carried = jnp.concatenate([b0 * cw_ref[0:1, :] + b1 * cw_ref[1:2, :], b1 * cw_ref[0:1, :]], axis=0)
    r = lax.broadcasted_iota(jnp.int32, (tm, 2 * nseq), 0)
    c = lax.broadcasted_iota(jnp.int32, (tm, 2 * nseq), 1)
    spread = ((r >> shift) == (c & (nseq - 1))) & ((r & (seg - 1)) == (c >> (nseq.bit_length() - 1)))
    conv = conv + _dot_exact_rows(jnp.where(spread, 1.0, 0.0).astype(BF16), carried)
    r = lax.broadcasted_iota(jnp.int32, (2 * nseq, tm), 0)
    c = lax.broadcasted_iota(jnp.int32, (2 * nseq, tm), 1)
    pick = ((c >> shift) == (r & (nseq - 1))) & ((c & (seg - 1)) == seg - 2 + (r >> (nseq.bit_length() - 1)))
    picked = _dot_exact_rows(jnp.where(pick, 1.0, 0.0).astype(BF16), a)
    a_new_ref[0] = picked[0:nseq]
    a_new_ref[1] = picked[nseq:2 * nseq]
    _ffn_finish(conv, gate, wd_ref, o_ref, gfin_ref, j, nj, tm)


def _ffn_step(x, gain, w_up, conv_w, conv_b, w_down, gain_final, b0, b1, *, seg, tf=512):
    n, dm = x.shape
    dff = w_down.shape[0]
    nj = dff // tf
    tm = n
    nseq = n // seg
    body = functools.partial(_ffn_step_body, tm=tm, seg=seg, nj=nj)
    return pl.pallas_call(
        body,
        grid=(1, nj),
        in_specs=[
            pl.BlockSpec((tm, dm), lambda i, j: (i, 0)),
            pl.BlockSpec((1, dm), lambda i, j: (0, 0)),
            pl.BlockSpec((dm, tf), lambda i, j: (0, j)),
            pl.BlockSpec((dm, tf), lambda i, j: (0, nj + j)),
            pl.BlockSpec((tf, dm), lambda i, j: (j, 0)),
            pl.BlockSpec((CONV_W, tf), lambda i, j: (0, j)),
            pl.BlockSpec((1, tf), lambda i, j: (0, j)),
            pl.BlockSpec((1, dm), lambda i, j: (0, 0)),
            pl.BlockSpec((nseq, tf), lambda i, j: (0, j)),
            pl.BlockSpec((nseq, tf), lambda i, j: (0, j)),
        ],
        out_specs=(pl.BlockSpec((tm, dm), lambda i, j: (i, 0)),
                   pl.BlockSpec((CONV_W - 1, nseq, tf), lambda i, j: (0, 0, j))),
        out_shape=(jax.ShapeDtypeStruct((n, dm), F32), jax.ShapeDtypeStruct((CONV_W - 1, nseq, dff), F32)),
        scratch_shapes=[pltpu.VMEM((tm, dm), BF16)],
        compiler_params=_params("arbitrary", "arbitrary"),
        name="ffn_step",
    )(x, gain.reshape(1, dm), w_up, w_up, w_down, conv_w, conv_b.reshape(1, dff), gain_final.reshape(1, dm),
      b0, b1)


def kernel(x_prompt, x_sample, mem_prompt, cache_mem_k, cache_mem_v, state_s5_re, state_s5_im, state_gla, state_conv, norm_mix, w_in, s5_lambda_re, s5_lambda_im, s5_log_dt, s5_b_re, s5_b_im, s5_c_re, s5_c_im, s5_d, s5_w_glu, s5_b_glu, norm_s5_out, gla_w_g2, gla_b_g, norm_gla_out, w_out, norm_xattn, norm_mem, w_xq, w_xk, w_xv, w_xo, norm_ffn, w_up, conv_w, conv_b, w_down, norm_final):
    depth = w_in.shape[0]
    assert depth == 1
    bp, seq, dm = x_prompt.shape
    bs, seg, _ = x_sample.shape
    n_mem = mem_prompt.shape[1]
    dff = w_down.shape[1]
    l = 0
    s5_args = (s5_lambda_re[l], s5_lambda_im[l], s5_log_dt[l], s5_b_re[l], s5_b_im[l], s5_c_re[l], s5_c_im[l],
               s5_d[l], s5_w_glu[l], s5_b_glu[l])
    tm_p = min(1024, seq)
    n_s = bs * seg
    w_in, w_out, w_xq, w_xo, w_up, w_down = (w.astype(BF16) for w in (w_in, w_out, w_xq, w_xo, w_up, w_down))

    mk, mk_b = _mem_proj("mem_k", mem_prompt, norm_mem[l], w_xk[l], tm=min(1024, bp * n_mem))
    mv, mv_b = _mem_proj("mem_v", mem_prompt, norm_mem[l], w_xv[l], tm=min(1024, bp * n_mem))

    xp = x_prompt.reshape(bp * seq, dm)
    proj = _in_proj(xp, norm_mix[l], w_in[l], gla_w_g2[l], gla_b_g[l], tm=tm_p)
    proj3 = proj.reshape(bp, seq, -1)
    s5_w, s5_powers = _s5_weights(*s5_args)
    steps_p = SUBLANES // bp
    z_s5, hend_r, hend_i = _s5_mixer(proj3, s5_w, _s5_scan_table(s5_powers, steps_p, bp),
                                     tt=min(512 // bp, seq), steps=steps_p)
    y_gla, st_gla = _gla_seq(proj3, norm_gla_out[l], tt=min(512, seq))
    x2 = _mix_xattn_seq(z_s5.reshape(bp * seq, -1), y_gla.reshape(bp * seq, -1), xp, norm_s5_out[l], norm_xattn[l],
                        w_out[l], w_xq[l], w_xo[l], mk_b, mv_b, tm=tm_p, seq=seq)
    yp, tails = _ffn_seq(x2, norm_ffn[l], w_up[l], conv_w[l], conv_b[l], w_down[l], norm_final, tm=tm_p, seq=seq)

    g_total = s5_lambda_re.shape[1]
    n_state = s5_lambda_re.shape[2]
    p_re = jnp.swapaxes(hend_r[:, SUBLANES - bp:], 0, 1).reshape(1, bp, g_total, n_state)
    p_im = jnp.swapaxes(hend_i[:, SUBLANES - bp:], 0, 1).reshape(1, bp, g_total, n_state)
    p_gla = jnp.swapaxes(st_gla, -1, -2)[None]
    tiles_per_seq = seq // tm_p
    p_conv = tails.reshape(bp, tiles_per_seq, SUBLANES, dff)[:, -1, SUBLANES - (CONV_W - 1):, :][None]

    xs = x_sample.reshape(n_s, dm)
    proj_s = _in_proj(xs, norm_mix[l], w_in[l], gla_w_g2[l], gla_b_g[l], tm=n_s)
    h0r = state_s5_re[l].reshape(1, bs, -1)
    h0i = state_s5_im[l].reshape(1, bs, -1)
    z_s5_s, hs_r, hs_i = _s5_mixer(proj_s[None], s5_w, _s5_scan_table(s5_powers, seg, 1), tt=n_s, steps=seg,
                                   h0=(h0r, h0i))
    y_gla_s, s_gla = _gla_step(proj_s, norm_gla_out[l], state_gla[l], seg=seg)
    x1s = _out_proj(z_s5_s[0], y_gla_s, norm_s5_out[l], w_out[l], xs, tm=n_s)
    qs = _norm_linear("xattn_q_s", x1s, norm_xattn[l], w_xq[l], tm=n_s)
    att = _xattn_step(qs, cache_mem_k[l], cache_mem_v[l], seg=seg)
    x2s = _cast_linear("xattn_o_s", att, w_xo[l], x1s, tm=n_s)
    buf = state_conv[l]
    ys, a_new = _ffn_step(x2s, norm_ffn[l], w_up[l], conv_w[l], conv_b[l], w_down[l], norm_final,
                          buf[:, 0], buf[:, 1], seg=seg)

    s_re = hs_r.reshape(1, bs, g_total, n_state)
    s_im = hs_i.reshape(1, bs, g_total, n_state)
    s_conv = jnp.swapaxes(a_new, 0, 1)[None]

    return (yp.reshape(bp, seq, dm), ys.reshape(bs, seg, dm), mk, mv,
            p_re, p_im, p_gla, p_conv, s_re, s_im, s_gla[None], s_conv)
```

```python
import functools

import jax
import jax.numpy as jnp
from jax import lax
from jax.experimental import pallas as pl
from jax.experimental.pallas import tpu as pltpu

F32 = jnp.float32
BF16 = jnp.bfloat16

EPS = 1e-6
S5_GROUP = 16
S5_STATE = 64
S5_BLOCK_GROUPS = 16
GLA_HEADS = 4
GLA_RANK = 16
GLA_TAU = 16.0
GLA_CHUNK = 64
X_HEADS = 4
CONV_W = 3

SUBLANES = 8
LANES = 128
VMEM_LIMIT_BYTES = 56 * 1024 * 1024

_CONTRACT_LAST = (((1,), (1,)), ((), ()))
_CONTRACT_FIRST = (((0,), (0,)), ((), ()))


def _params(*semantics):
    return pltpu.CompilerParams(dimension_semantics=semantics, vmem_limit_bytes=VMEM_LIMIT_BYTES)


def _dot(a, b):
    return jnp.dot(a, b, preferred_element_type=F32)


def _for_row_chunks(n_rows, chunk, fn):
    def body(i, carry):
        fn(pl.multiple_of(i * chunk, chunk))
        return carry
    lax.fori_loop(0, n_rows // chunk, body, 0)


def _rms(x, gain):
    r = lax.rsqrt(jnp.mean(x * x, axis=-1, keepdims=True) + EPS)
    return (x * r) * gain


def _gelu(x):
    return 0.5 * x * (1.0 + lax.erf(x * (2.0 ** -0.5)))


def _rms_rows_to_bf16(x_ref, gain_ref, h_ref, n_rows, row0=0, chunk=128):
    def f(r0):
        h_ref[pl.ds(row0 + r0, chunk), :] = _rms(x_ref[pl.ds(r0, chunk), :], gain_ref[...]).astype(BF16)
    _for_row_chunks(n_rows, chunk, f)


def _s5_prep_body(lr_ref, li_ref, ldt_ref, br_ref, bi_ref, apr_ref, api_ref, bbr_ref, bbi_ref):
    lr = lr_ref[...]
    li = li_ref[...]
    dt = jnp.exp(ldt_ref[...])
    mag = jnp.exp(lr * dt)
    ar = mag * jnp.cos(li * dt)
    ai = mag * jnp.sin(li * dt)
    den = lr * lr + li * li
    pr = ar - 1.0
    qr = (pr * lr + ai * li) / den
    qi = (ai * lr - pr * li) / den
    for p in range(S5_GROUP):
        br = br_ref[p]
        bi = bi_ref[p]
        bbr_ref[p] = qr * br - qi * bi
        bbi_ref[p] = qr * bi + qi * br
    cr, ci = ar, ai
    apr_ref[0] = cr
    api_ref[0] = ci
    for k in range(1, SUBLANES):
        cr, ci = cr * ar - ci * ai, cr * ai + ci * ar
        apr_ref[k] = cr
        api_ref[k] = ci


def _s5_prep(lam_re, lam_im, log_dt, b_re, b_im):
    g, n = lam_re.shape
    out = jax.ShapeDtypeStruct((SUBLANES, g, n), F32)
    outb = jax.ShapeDtypeStruct((S5_GROUP, g, n), F32)
    return pl.pallas_call(
        _s5_prep_body, out_shape=(out, out, outb, outb), name="s5_prep",
    )(lam_re, lam_im, log_dt.reshape(g, 1), jnp.transpose(b_re, (2, 0, 1)), jnp.transpose(b_im, (2, 0, 1)))


def _s5_weights(lam_re, lam_im, log_dt, b_re, b_im, c_re, c_im, d, w_glu, b_glu):
    g, n = lam_re.shape
    p = S5_GROUP
    gb = S5_BLOCK_GROUPS
    nb = g // gb
    apr, api, bbr, bbi = _s5_prep(lam_re, lam_im, log_dt, b_re, b_im)
    eye = jnp.eye(gb, dtype=F32)

    def bu_block(bb):
        return jnp.einsum('pbgn,gh->bgphn', bb.reshape(p, nb, gb, n), eye).reshape(nb, gb * p, gb * n)

    def c_block(c):
        return jnp.einsum('bgpn,gh->bgnhp', c.reshape(nb, gb, p, n), eye).reshape(nb, gb * n, gb * p)

    w_bu = jnp.concatenate([bu_block(bbr), bu_block(bbi)], axis=-1).astype(BF16)
    w_c = jnp.concatenate([c_block(c_re), -c_block(c_im)], axis=1).astype(BF16)
    w_g = jnp.einsum('bgpq,gh->bgphq', w_glu.reshape(nb, gb, p, p), eye).reshape(nb, gb * p, gb * p).astype(BF16)
    d_b = d.reshape(nb, 1, gb * p)
    bg_b = b_glu.reshape(nb, 1, gb * p)
    pw_r = apr.reshape(SUBLANES, nb, gb * n)
    pw_i = api.reshape(SUBLANES, nb, gb * n)
    return (w_bu, w_c, w_g, d_b, bg_b), (pw_r, pw_i)


def _s5_scan_table(powers, steps, rows_per_step):
    pw_r, pw_i = powers
    t = (jnp.arange(SUBLANES) // rows_per_step) % steps

    def masked(k):
        m = (t >= k)[:, None, None]
        return jnp.where(m, pw_r[k - 1][None], 0.0), jnp.where(m, pw_i[k - 1][None], 0.0)

    tabs = []
    k = 1
    while k < steps:
        tabs.extend(masked(k))
        k *= 2
    tabs.append(pw_r[t])
    tabs.append(pw_i[t])
    whole = jnp.full((SUBLANES,), steps - 1)
    tabs.append(pw_r[whole])
    tabs.append(pw_i[whole])
    return jnp.transpose(jnp.stack(tabs), (2, 0, 1, 3))


def _in_proj_body(x_ref, gain_ref, w_ref, wr_ref, wg_ref, wg2_ref, bg_ref, o_ref, h_ref, *, tm, n_main, n_r):
    j = pl.program_id(1)

    @pl.when(j == 0)
    def _():
        _rms_rows_to_bf16(x_ref, gain_ref, h_ref, tm)

    @pl.when(j < n_main)
    def _():
        o_ref[...] = _dot(h_ref[...], w_ref[...].astype(BF16))

    @pl.when((j >= n_main) & (j < n_main + n_r))
    def _():
        o_ref[...] = _dot(h_ref[...], wr_ref[...].astype(BF16))

    @pl.when(j == n_main + n_r)
    def _():
        g_low = _dot(h_ref[...], wg_ref[...].astype(BF16))
        pre = _dot(g_low.astype(BF16), wg2_ref[...].astype(BF16)) + bg_ref[...]
        log_sig = jnp.minimum(pre, 0.0) - jnp.log1p(jnp.exp(-jnp.abs(pre)))
        o_ref[...] = log_sig * (1.0 / GLA_TAU)


def _in_proj(x, gain, w_in, w_g2, b_g, *, tm, tn=512):
    n, dm = x.shape
    mix = dm
    n_u_q_k_v = mix // 2 + 2 * (mix // 4) + mix // 2
    col_g = n_u_q_k_v
    col_r = n_u_q_k_v + GLA_RANK
    w_r = w_in[:, col_r:]
    w_g = w_in[:, col_g:col_r]
    n_main = n_u_q_k_v // tn
    n_r = w_r.shape[1] // tn
    n_cols = n_u_q_k_v + w_r.shape[1] + w_g2.shape[1]
    body = functools.partial(_in_proj_body, tm=tm, n_main=n_main, n_r=n_r)
    return pl.pallas_call(
        body,
        grid=(n // tm, n_main + n_r + 1),
        in_specs=[
            pl.BlockSpec((tm, dm), lambda i, j: (i, 0)),
            pl.BlockSpec((1, dm), lambda i, j: (0, 0)),
            pl.BlockSpec((dm, tn), lambda i, j: (0, jnp.minimum(j, n_main - 1))),
            pl.BlockSpec((dm, tn), lambda i, j: (0, jnp.clip(j - n_main, 0, n_r - 1))),
            pl.BlockSpec((dm, GLA_RANK), lambda i, j: (0, 0)),
            pl.BlockSpec(w_g2.shape, lambda i, j: (0, 0)),
            pl.BlockSpec((1, w_g2.shape[1]), lambda i, j: (0, 0)),
        ],
        out_specs=pl.BlockSpec((tm, tn), lambda i, j: (i, j)),
        out_shape=jax.ShapeDtypeStruct((n, n_cols), F32),
        scratch_shapes=[pltpu.VMEM((tm, dm), BF16)],
        compiler_params=_params("arbitrary", "arbitrary"),
        name="in_proj",
    )(x, gain.reshape(1, dm), w_in, w_r, w_g, w_g2, b_g.reshape(1, -1))


def _s5_body(*refs, tt, steps, nseq, ns, chunk):
    carried = nseq > 1
    if carried:
        (u_ref, wbu_ref, wc_ref, wg_ref, d_ref, bg_ref, tab_ref, z_ref, hr_ref, hi_ref,
         bu_ref, cr_ref, ci_ref, us_ref, zs_ref) = refs
    else:
        u_ref, wbu_ref, wc_ref, wg_ref, d_ref, bg_ref, tab_ref, h0r_ref, h0i_ref, z_ref, hr_ref, hi_ref, bu_ref = refs
    seg = steps
    n_lvl = steps.bit_length() - 1
    lane_group = 2 * LANES
    rid = lax.broadcasted_iota(jnp.int32, (SUBLANES, 1), 0)
    cw = u_ref.shape[-1]
    n_lt = cw // LANES

    if carried:
        @pl.when(pl.program_id(1) == 0)
        def _():
            cr_ref[...] = jnp.zeros_like(cr_ref)
            ci_ref[...] = jnp.zeros_like(ci_ref)

        per = chunk // nseq

    def chunk_u(c):
        if not carried:
            return u_ref[c * chunk:(c + 1) * chunk, :]
        for b in range(nseq):
            for lt in range(n_lt):
                us_ref[c * n_lt + lt, pl.ds(b, per, stride=nseq), :] = (
                    u_ref[b, c * per:(c + 1) * per, lt * LANES:(lt + 1) * LANES])
        return jnp.concatenate([us_ref[c * n_lt + lt] for lt in range(n_lt)], axis=1)

    def project_in(c):
        rows = slice(c * chunk, (c + 1) * chunk)
        bu = _dot(chunk_u(c).astype(BF16), wbu_ref[...])
        if not carried:
            seqs = slice(c * chunk // seg, (c + 1) * chunk // seg)
            a_r = tab_ref[2 * n_lvl, 0:1, :]
            a_i = tab_ref[2 * n_lvl + 1, 0:1, :]
            h0r = h0r_ref[seqs, :]
            h0i = h0i_ref[seqs, :]
            first = (lax.broadcasted_iota(jnp.int32, (chunk, chunk // seg), 0)
                     == seg * lax.broadcasted_iota(jnp.int32, (chunk, chunk // seg), 1))
            first_b = jnp.where(first, 1.0, 0.0).astype(BF16)
            bu = bu + jnp.concatenate([_dot_exact_rows(first_b, a_r * h0r - a_i * h0i),
                                       _dot_exact_rows(first_b, a_r * h0i + a_i * h0r)], axis=1)
        bu_ref[rows, :] = bu

    def scan(c, carry):
        carry = list(carry)
        for blk in range(chunk // SUBLANES):
            rows = slice(c * chunk + blk * SUBLANES, c * chunk + (blk + 1) * SUBLANES)
            for gi in range(ns // lane_group):
                cols = slice(gi * lane_group, (gi + 1) * lane_group)
                cols_im = slice(ns + gi * lane_group, ns + (gi + 1) * lane_group)
                xr = bu_ref[rows, cols]
                xi = bu_ref[rows, cols_im]
                for lvl in range(n_lvl):
                    k = 1 << lvl
                    ar = tab_ref[2 * lvl, :, cols]
                    ai = tab_ref[2 * lvl + 1, :, cols]
                    sr = pltpu.roll(xr, k * nseq, 0)
                    si = pltpu.roll(xi, k * nseq, 0)
                    xr, xi = xr + (ar * sr - ai * si), xi + (ar * si + ai * sr)
                if carried:
                    pr = tab_ref[2 * n_lvl, :, cols]
                    pi = tab_ref[2 * n_lvl + 1, :, cols]
                    c_r, c_i = carry[gi]
                    hr = xr + (pr * c_r - pi * c_i)
                    hi = xi + (pr * c_i + pi * c_r)
                    lr, li = xr, xi
                    half = SUBLANES // 2
                    while half >= nseq:
                        keep = (rid & (2 * half - 1)) >= half
                        lr = jnp.where(keep, lr, pltpu.roll(lr, SUBLANES - half, 0))
                        li = jnp.where(keep, li, pltpu.roll(li, SUBLANES - half, 0))
                        half //= 2
                    er = tab_ref[2 * n_lvl + 2, :, cols]
                    ei = tab_ref[2 * n_lvl + 3, :, cols]
                    carry[gi] = (lr + (er * c_r - ei * c_i), li + (er * c_i + ei * c_r))
                else:
                    hr, hi = xr, xi
                    for e in range(SUBLANES // seg):
                        src = e * seg + seg - 1
                        dst = (c * chunk + blk * SUBLANES) // seg + e
                        hr_ref[dst:dst + 1, cols] = hr[src:src + 1]
                        hi_ref[dst:dst + 1, cols] = hi[src:src + 1]
                bu_ref[rows, cols] = hr
                bu_ref[rows, cols_im] = hi
        return carry

    def project_out(c):
        rows = slice(c * chunk, (c + 1) * chunk)
        if carried:
            u = jnp.concatenate([us_ref[c * n_lt + lt] for lt in range(n_lt)], axis=1)
        else:
            u = u_ref[rows, :]
        y = _dot(bu_ref[rows, :].astype(BF16), wc_ref[...]) + d_ref[...] * u
        z = _gelu(y)
        z = z * jax.nn.sigmoid(_dot(z.astype(BF16), wg_ref[...]) + bg_ref[...])
        if carried:
            for lt in range(n_lt):
                zs_ref[c * n_lt + lt] = z[:, lt * LANES:(lt + 1) * LANES]
            for b in range(nseq):
                for lt in range(n_lt):
                    z_ref[b, c * per:(c + 1) * per, lt * LANES:(lt + 1) * LANES] = (
                        zs_ref[c * n_lt + lt, pl.ds(b, per, stride=nseq), :])
        else:
            z_ref[rows, :] = z

    n_chunks = tt // chunk
    n_groups = ns // lane_group
    group_cols = [slice(gi * lane_group, (gi + 1) * lane_group) for gi in range(n_groups)]
    carry = [(cr_ref[:, cols], ci_ref[:, cols]) for cols in group_cols] if carried else [None] * n_groups
    project_in(0)
    for c in range(n_chunks):
        if c + 1 < n_chunks:
            project_in(c + 1)
        carry = scan(c, carry)
        project_out(c)
    if carried:
        for cols, (c_r, c_i) in zip(group_cols, carry):
            cr_ref[:, cols] = c_r
            ci_ref[:, cols] = c_i
            hr_ref[:, cols] = c_r
            hi_ref[:, cols] = c_i


def _s5_mixer(proj3, weights, tab, *, tt, steps, h0=None, chunk=128):
    w_bu, w_c, w_g, d_b, bg_b = weights
    bsz, seq, _ = proj3.shape
    nb, cw, ns2 = w_bu.shape
    ns = ns2 // 2
    n_tab = tab.shape[1]
    carried = h0 is None
    nseq = bsz if carried else 1
    assert nseq * steps == SUBLANES or not carried
    rows = nseq * tt
    body = functools.partial(_s5_body, tt=rows, steps=steps, nseq=nseq, ns=ns, chunk=min(chunk, rows))
    wspec = lambda shape: pl.BlockSpec((None,) + shape, lambda g, t: (g,) + (0,) * len(shape))
    lead = bsz if carried else None
    uz_spec = pl.BlockSpec((lead, tt, cw), lambda g, t: (0, t, g))
    in_specs = [
        uz_spec,
        wspec((cw, ns2)), wspec((ns2, cw)), wspec((cw, cw)), wspec((1, cw)), wspec((1, cw)),
        wspec((n_tab, SUBLANES, ns)),
    ]
    args = [proj3, w_bu, w_c, w_g, d_b, bg_b, tab]
    z_shape = jax.ShapeDtypeStruct((bsz, seq, nb * cw), F32)
    scratch = [pltpu.VMEM((rows, ns2), F32)]
    if carried:
        st_shape = jax.ShapeDtypeStruct((nb, SUBLANES, ns), F32)
        st_spec = pl.BlockSpec((None, SUBLANES, ns), lambda g, t: (g, 0, 0))
        slabs = (rows // min(chunk, rows)) * (cw // LANES)
        scratch += [pltpu.VMEM((SUBLANES, ns), F32), pltpu.VMEM((SUBLANES, ns), F32),
                    pltpu.VMEM((slabs, min(chunk, rows), LANES), F32), pltpu.VMEM((slabs, min(chunk, rows), LANES), F32)]
    else:
        assert bsz == 1
        h0r, h0i = h0
        st_spec = pl.BlockSpec((None, tt // steps, ns), lambda g, t: (0, t, g))
        in_specs += [st_spec] * 2
        args += [h0r, h0i]
        st_shape = jax.ShapeDtypeStruct((bsz, seq // steps, nb * ns), F32)
    return pl.pallas_call(
        body,
        grid=(nb, seq // tt),
        in_specs=in_specs,
        out_specs=(uz_spec, st_spec, st_spec),
        out_shape=(z_shape, st_shape, st_shape),
        scratch_shapes=scratch,
        compiler_params=_params("arbitrary", "arbitrary"),
        name="s5_carried" if carried else "s5_segments",
    )(*args)


def _split3_bf16(x):
    hi = x.astype(BF16)
    r1 = x - hi.astype(F32)
    mid = r1.astype(BF16)
    lo = (r1 - mid.astype(F32)).astype(BF16)
    return hi, mid, lo


def _gla_out(o, gain, r):
    return _rms(o, gain) * (r * jax.nn.sigmoid(r))


def _gla_seq_body(q_ref, k_ref, v_ref, la_ref, r_ref, gain_ref, y_ref, st_out_ref, st_ref, *, tt, dk, dv):
    c = GLA_CHUNK
    hh = GLA_HEADS

    @pl.when(pl.program_id(1) == 0)
    def _():
        st_ref[...] = jnp.zeros_like(st_ref)

    row = lax.broadcasted_iota(jnp.int32, (tt, tt), 0)
    col = lax.broadcasted_iota(jnp.int32, (tt, tt), 1)
    shift = c.bit_length() - 1
    tri = (col <= row) & ((row >> shift) == (col >> shift))
    tri_b = jnp.where(tri, 1.0, 0.0).astype(BF16)
    hi, mid, lo = _split3_bf16(la_ref[...])
    bcum = _dot(tri_b, hi) + _dot(tri_b, mid) + _dot(tri_b, lo)
    causal = lax.broadcasted_iota(jnp.int32, (c, c), 1) <= lax.broadcasted_iota(jnp.int32, (c, c), 0)
    scale = dk ** -0.5
    chunks = [slice(ci * c, (ci + 1) * c) for ci in range(tt // c)]
    work = []
    for h in range(hh):
        ks = slice(h * dk, (h + 1) * dk)
        vs = slice(h * dv, (h + 1) * dv)
        qts, decays, o_intra, kvs = [], [], [], []
        for sl in chunks:
            bc = bcum[sl, ks]
            bl = bc[c - 1:c]
            qt = ((q_ref[sl, ks] * scale) * jnp.exp(bc)).astype(BF16)
            kc = k_ref[sl, ks]
            kt = (kc * jnp.exp(-bc)).astype(BF16)
            ke = (kc * jnp.exp(bl - bc)).astype(BF16)
            vb = v_ref[sl, vs].astype(BF16)
            s = lax.dot_general(qt, kt, _CONTRACT_LAST, preferred_element_type=F32)
            s = jnp.where(causal, s, 0.0)
            qts.append(qt)
            decays.append(jnp.exp(bl))
            o_intra.append(_dot(s.astype(BF16), vb))
            kvs.append(lax.dot_general(vb, ke, _CONTRACT_FIRST, preferred_element_type=F32))
        work.append((vs, qts, decays, o_intra, kvs))
    for h, (vs, qts, decays, o_intra, kvs) in enumerate(work):
        st = st_ref[h]
        for sl, qt, dec, oi, kv in zip(chunks, qts, decays, o_intra, kvs):
            o = oi + lax.dot_general(qt, st.astype(BF16), _CONTRACT_LAST, preferred_element_type=F32)
            st = st * dec + kv
            y_ref[sl, vs] = _gla_out(o, gain_ref[...], r_ref[sl, vs]).astype(y_ref.dtype)
        st_ref[h] = st
        st_out_ref[h] = st


def _gla_seq(proj3, gain, *, tt):
    bsz, seq, _ = proj3.shape
    hh = GLA_HEADS
    dv = gain.shape[0]
    dk = dv // 2
    wk, wv = hh * dk, hh * dv
    col_q = 1024 // wk
    col_k = col_q + 1
    col_v = (1024 + 2 * wk) // wv
    col_r = col_v + 1
    col_la = (1024 + 2 * wk + 2 * wv) // wk
    spec = lambda w, c0: pl.BlockSpec((None, tt, w), lambda b, t: (b, t, c0))
    body = functools.partial(_gla_seq_body, tt=tt, dk=dk, dv=dv)
    return pl.pallas_call(
        body,
        grid=(bsz, seq // tt),
        in_specs=[spec(wk, col_q), spec(wk, col_k), spec(wv, col_v), spec(wk, col_la), spec(wv, col_r),
                  pl.BlockSpec((1, dv), lambda b, t: (0, 0))],
        out_specs=(pl.BlockSpec((None, tt, wv), lambda b, t: (b, t, 0)),
                   pl.BlockSpec((None, hh, dv, dk), lambda b, t: (b, 0, 0, 0))),
        out_shape=(jax.ShapeDtypeStruct((bsz, seq, wv), BF16),
                   jax.ShapeDtypeStruct((bsz, hh, dv, dk), F32)),
        scratch_shapes=[pltpu.VMEM((hh, dv, dk), F32)],
        compiler_params=_params("arbitrary", "arbitrary"),
        name="gla_seq",
    )(proj3, proj3, proj3, proj3, proj3, gain.reshape(1, dv))


def _gla_step_body(q_ref, k_ref, v_ref, la_ref, r_ref, gain_ref, s_ref, y_ref, so_ref, *, nb, seg, dk, dv):
    hh = GLA_HEADS
    rows = SUBLANES
    per = rows // seg
    row = lax.broadcasted_iota(jnp.int32, (rows, rows), 0)
    col = lax.broadcasted_iota(jnp.int32, (rows, rows), 1)
    shift = seg.bit_length() - 1
    causal = (col <= row) & ((row >> shift) == (col >> shift))
    rid = lax.broadcasted_iota(jnp.int32, (rows, 1), 0)
    t_in = rid & (seg - 1)
    ones_b = jnp.ones((rows, dv), BF16)
    scale = dk ** -0.5

    def pair(i, carry):
        r0 = pl.multiple_of(i * rows, rows)
        prep = []
        for h in range(hh):
            ks = slice(h * dk, (h + 1) * dk)
            vs = slice(h * dv, (h + 1) * dv)
            bc = la_ref[pl.ds(r0, rows), ks]
            k = 1
            while k < seg:
                bc = bc + jnp.where(t_in >= k, pltpu.roll(bc, k, 0), 0.0)
                k *= 2
            bl = jnp.zeros_like(bc)
            for e in range(per):
                last = e * seg + seg - 1
                bl = jnp.where((rid >> shift) == e, bc[last:last + 1], bl)
            qt = ((q_ref[pl.ds(r0, rows), ks] * scale) * jnp.exp(bc)).astype(BF16)
            kc = k_ref[pl.ds(r0, rows), ks]
            kt = (kc * jnp.exp(-bc)).astype(BF16)
            ke = kc * jnp.exp(bl - bc)
            vb = v_ref[pl.ds(r0, rows), vs].astype(BF16)
            per_seq = []
            for e in range(per):
                last = e * seg + seg - 1
                dec = jnp.exp(bc[last:last + 1])
                d_hi, d_mid, d_lo = _split3_bf16(dec)
                d3 = jnp.where(rid == 0, d_hi.astype(F32),
                               jnp.where(rid == 1, d_mid.astype(F32),
                                         jnp.where(rid == 2, d_lo.astype(F32), 0.0))).astype(BF16)
                ke_e = jnp.where((rid >> shift) == e, ke, 0.0).astype(BF16)
                per_seq.append((d3, ke_e, s_ref[i * per + e, h]))
            prep.append((vs, qt, kt, vb, per_seq))
        scores = [lax.dot_general(qt, kt, _CONTRACT_LAST, preferred_element_type=F32) for _, qt, kt, _, _ in prep]
        inter, dec_cols, kvs = [], [], []
        for _, qt, _, vb, per_seq in prep:
            for d3, ke_e, st in per_seq:
                inter.append(_dot(qt, st.astype(BF16)))
                dec_cols.append(lax.dot_general(d3, ones_b, _CONTRACT_FIRST, preferred_element_type=F32))
                kvs.append(lax.dot_general(ke_e, vb, _CONTRACT_FIRST, preferred_element_type=F32))
        intra = [_dot(jnp.where(causal, s, 0.0).astype(BF16), vb) for s, (_, _, _, vb, _) in zip(scores, prep)]
        for h, (vs, _, _, _, per_seq) in enumerate(prep):
            o_inter = inter[h * per]
            for e in range(per):
                if e:
                    o_inter = jnp.where((rid >> shift) == e, inter[h * per + e], o_inter)
                so_ref[i * per + e, h] = dec_cols[h * per + e] * per_seq[e][2] + kvs[h * per + e]
            y_ref[pl.ds(r0, rows), vs] = _gla_out(intra[h] + o_inter, gain_ref[...], r_ref[pl.ds(r0, rows), vs])
        return carry

    lax.fori_loop(0, nb // per, pair, 0)


def _gla_step(proj, gain, state, *, seg, nb=8):
    n = proj.shape[0]
    bsz, hh, dk, dv = state.shape
    rows = nb * seg
    c_q = 1024 // (hh * dk)
    c_k = c_q + 1
    c_v = (1024 + 2 * hh * dk) // (hh * dv)
    c_r = c_v + 1
    c_la = (1024 + 2 * hh * dk + 2 * hh * dv) // (hh * dk)
    spec = lambda w, c0: pl.BlockSpec((rows, w), lambda i: (i, c0))
    body = functools.partial(_gla_step_body, nb=nb, seg=seg, dk=dk, dv=dv)
    return pl.pallas_call(
        body,
        grid=(bsz // nb,),
        in_specs=[spec(hh * dk, c_q), spec(hh * dk, c_k), spec(hh * dv, c_v), spec(hh * dk, c_la), spec(hh * dv, c_r),
                  pl.BlockSpec((1, dv), lambda i: (0, 0)),
                  pl.BlockSpec((nb, hh, dk, dv), lambda i: (i, 0, 0, 0))],
        out_specs=(pl.BlockSpec((rows, hh * dv), lambda i: (i, 0)),
                   pl.BlockSpec((nb, hh, dk, dv), lambda i: (i, 0, 0, 0))),
        out_shape=(jax.ShapeDtypeStruct((n, hh * dv), F32), jax.ShapeDtypeStruct(state.shape, F32)),
        compiler_params=_params("arbitrary"),
        name="gla_step",
    )(proj, proj, proj, proj, proj, gain.reshape(1, dv), state)


def _tiled_linear(name, prologue, row_ins, aux_ins, aux_specs, w, *, tm, tn, residual=None):
    n = row_ins[0].shape[0]
    k_dim, n_cols = w.shape
    nr, na = len(row_ins), len(aux_ins)
    has_res = residual is not None

    def body(*refs):
        row_refs = refs[:nr]
        aux_refs = refs[nr:nr + na]
        w_ref = refs[nr + na]
        o_ref, h_ref = refs[-2], refs[-1]

        @pl.when(pl.program_id(1) == 0)
        def _():
            prologue(row_refs, aux_refs, h_ref)

        acc = _dot(h_ref[...], w_ref[...].astype(BF16))
        if has_res:
            acc = acc + refs[nr + na + 1][...]
        o_ref[...] = acc

    in_specs = [pl.BlockSpec((tm, a.shape[1]), lambda i, j: (i, 0)) for a in row_ins]
    in_specs += list(aux_specs)
    in_specs.append(pl.BlockSpec((k_dim, tn), lambda i, j: (0, j)))
    args = list(row_ins) + list(aux_ins) + [w]
    if has_res:
        in_specs.append(pl.BlockSpec((tm, tn), lambda i, j: (i, j)))
        args.append(residual)
    return pl.pallas_call(
        body,
        grid=(n // tm, n_cols // tn),
        in_specs=in_specs,
        out_specs=pl.BlockSpec((tm, tn), lambda i, j: (i, j)),
        out_shape=jax.ShapeDtypeStruct((n, n_cols), F32),
        scratch_shapes=[pltpu.VMEM((tm, k_dim), BF16)],
        compiler_params=_params("arbitrary", "arbitrary"),
        name=name,
    )(*args)


def _whole(a):
    return pl.BlockSpec(a.shape, lambda i, j: (0,) * a.ndim)


def _norm_linear(name, x, gain, w, *, tm, tn=512):
    def prologue(row_refs, aux_refs, h_ref):
        _rms_rows_to_bf16(row_refs[0], aux_refs[0], h_ref, tm)
    g = gain.reshape(1, -1)
    return _tiled_linear(name, prologue, [x], [g], [_whole(g)], w, tm=tm, tn=tn)


def _cast_linear(name, x, w, residual, *, tm, tn=512):
    def prologue(row_refs, aux_refs, h_ref):
        def f(r0):
            h_ref[pl.ds(r0, 128), :] = row_refs[0][pl.ds(r0, 128), :].astype(BF16)
        _for_row_chunks(tm, 128, f)
    return _tiled_linear(name, prologue, [x], [], [], w, tm=tm, tn=tn, residual=residual)


def _out_proj(z_s5, y_gla, gain_s5, w_out, x, *, tm, tn=512):
    ws = z_s5.shape[1]

    def prologue(row_refs, aux_refs, h_ref):
        def f(r0):
            h_ref[pl.ds(r0, 128), 0:ws] = _rms(row_refs[0][pl.ds(r0, 128), :], aux_refs[0][...]).astype(BF16)
            h_ref[pl.ds(r0, 128), ws:] = row_refs[1][pl.ds(r0, 128), :].astype(BF16)
        _for_row_chunks(tm, 128, f)
    g = gain_s5.reshape(1, -1)
    return _tiled_linear("out_proj", prologue, [z_s5, y_gla], [g], [_whole(g)], w_out, tm=tm, tn=tn, residual=x)


def _attend(q, kb, vb, scale):
    s = lax.dot_general(q, kb, _CONTRACT_LAST, preferred_element_type=F32) * scale
    e = jnp.exp(s - jnp.max(s, axis=-1, keepdims=True))
    p = e / jnp.sum(e, axis=-1, keepdims=True)
    return _dot(p.astype(BF16), vb)


def _mix_xattn_seq_body(z_ref, y_ref, xres_ref, gs5_ref, gx_ref, wout_ref, wq_ref, wo_ref, mk_ref, mv_ref,
                        o_ref, x1_ref, h_ref, q_ref, *, tm, tn, nt, hd, chunk):
    j = pl.program_id(1)
    ws = z_ref.shape[1]
    scale = hd ** -0.5

    @pl.when(j == 0)
    def _():
        def f(r0):
            h_ref[pl.ds(r0, 128), 0:ws] = _rms(z_ref[pl.ds(r0, 128), :], gs5_ref[...]).astype(BF16)
            h_ref[pl.ds(r0, 128), ws:] = y_ref[pl.ds(r0, 128), :].astype(BF16)
        _for_row_chunks(tm, 128, f)

    @pl.when(j == nt)
    def _():
        _rms_rows_to_bf16(x1_ref, gx_ref, h_ref, tm)

    @pl.when(j == 2 * nt)
    def _():
        heads = [slice(h * hd, (h + 1) * hd) for h in range(X_HEADS)]

        def f(r0):
            scores = [lax.dot_general(q_ref[pl.ds(r0, chunk), hs], mk_ref[:, hs], _CONTRACT_LAST,
                                      preferred_element_type=F32) * scale for hs in heads]
            probs = []
            for s in scores:
                e = jnp.exp(s - jnp.max(s, axis=-1, keepdims=True))
                probs.append((e / jnp.sum(e, axis=-1, keepdims=True)).astype(BF16))
            for hs, p in zip(heads, probs):
                h_ref[pl.ds(r0, chunk), hs] = _dot(p, mv_ref[:, hs]).astype(BF16)
        _for_row_chunks(tm, chunk, f)

    for jj in range(nt):
        cols = slice(jj * tn, (jj + 1) * tn)

        @pl.when(j == jj)
        def _(cols=cols):
            x1_ref[:, cols] = _dot(h_ref[...], wout_ref[...]) + xres_ref[...]

        @pl.when(j == nt + jj)
        def _(cols=cols):
            q_ref[:, cols] = _dot(h_ref[...], wq_ref[...]).astype(BF16)

        @pl.when(j == 2 * nt + jj)
        def _(cols=cols):
            o_ref[...] = _dot(h_ref[...], wo_ref[...]) + x1_ref[:, cols]


def _mix_xattn_seq(z_s5, y_gla, x, gain_s5, gain_x, w_out, w_xq, w_xo, mk_b, mv_b, *, tm, seq, tn=512, chunk=256):
    n, dm = x.shape
    nt = dm // tn
    tiles_per_seq = seq // tm
    body = functools.partial(_mix_xattn_seq_body, tm=tm, tn=tn, nt=nt, hd=dm // X_HEADS, chunk=chunk)
    gs = gain_s5.reshape(1, -1)
    gx = gain_x.reshape(1, -1)
    phase = lambda p: (lambda i, j: (0, jnp.clip(j - p * nt, 0, nt - 1)))
    mspec = pl.BlockSpec((None,) + mk_b.shape[1:], lambda i, j: (i // tiles_per_seq, 0, 0))
    return pl.pallas_call(
        body,
        grid=(n // tm, 3 * nt),
        in_specs=[
            pl.BlockSpec((tm, z_s5.shape[1]), lambda i, j: (i, 0)),
            pl.BlockSpec((tm, y_gla.shape[1]), lambda i, j: (i, 0)),
            pl.BlockSpec((tm, tn), lambda i, j: (i, jnp.minimum(j, nt - 1))),
            _whole(gs), _whole(gx),
            pl.BlockSpec((dm, tn), phase(0)), pl.BlockSpec((dm, tn), phase(1)), pl.BlockSpec((dm, tn), phase(2)),
            mspec, mspec,
        ],
        out_specs=pl.BlockSpec((tm, tn), lambda i, j: (i, jnp.clip(j - 2 * nt, 0, nt - 1))),
        out_shape=jax.ShapeDtypeStruct((n, dm), F32),
        scratch_shapes=[pltpu.VMEM((tm, dm), F32), pltpu.VMEM((tm, dm), BF16), pltpu.VMEM((tm, dm), BF16)],
        compiler_params=_params("arbitrary", "arbitrary"),
        name="mix_xattn_seq",
    )(z_s5, y_gla, x, gs, gx, w_out, w_xq, w_xo, mk_b, mv_b)


def _cache_rows(c):
    b, m, h, hd = c.shape
    n_chunk = hd // LANES
    return c.reshape(b, m, h, n_chunk, LANES).transpose(0, 1, 3, 2, 4).reshape(b * m * n_chunk * h, LANES)


def _mem_proj_body(x_ref, gain_ref, w_ref, o_ref, ob_ref, h_ref, *, tm, hd):
    j = pl.program_id(1)

    @pl.when(j == 0)
    def _():
        _rms_rows_to_bf16(x_ref, gain_ref, h_ref, tm)

    y = _dot(h_ref[...], w_ref[...].astype(BF16))
    ob_ref[...] = y.astype(BF16)
    n_chunk = hd // LANES
    pitch = n_chunk * X_HEADS
    for ck in range(n_chunk):
        o_ref[pl.ds(ck * X_HEADS + j, tm, stride=pitch), :] = y[:, ck * LANES:(ck + 1) * LANES]


def _mem_proj(name, mem, gain, w, *, tm):
    b, m, dm = mem.shape
    n = b * m
    hd = dm // X_HEADS
    n_chunk = hd // LANES
    pitch = n_chunk * X_HEADS
    body = functools.partial(_mem_proj_body, tm=tm, hd=hd)
    g = gain.reshape(1, dm)
    rows, plain = pl.pallas_call(
        body,
        grid=(n // tm, X_HEADS),
        in_specs=[pl.BlockSpec((tm, dm), lambda i, j: (i, 0)), _whole(g), pl.BlockSpec((dm, hd), lambda i, j: (0, j))],
        out_specs=(pl.BlockSpec((tm * pitch, LANES), lambda i, j: (i, 0)),
                   pl.BlockSpec((tm, hd), lambda i, j: (i, j))),
        out_shape=(jax.ShapeDtypeStruct((n * pitch, LANES), F32), jax.ShapeDtypeStruct((n, dm), BF16)),
        scratch_shapes=[pltpu.VMEM((tm, dm), BF16)],
        compiler_params=_params("arbitrary", "arbitrary"),
        name=name,
    )(mem.reshape(n, dm), g, w)
    full = rows.reshape(b, m, n_chunk, X_HEADS, LANES).transpose(0, 1, 3, 2, 4).reshape(1, b, m, X_HEADS, hd)
    return full, plain.reshape(b, m, dm)


def _xattn_step_body(q_ref, k_ref, v_ref, o_ref, *, nb, seg, hd, n_mem):
    rows = SUBLANES
    per = rows // seg
    shift = seg.bit_length() - 1
    rid = lax.broadcasted_iota(jnp.int32, (rows, 1), 0)
    scale = hd ** -0.5
    n_chunk = hd // LANES
    pitch = n_chunk * X_HEADS

    def head_rows(ref, b, h):
        parts = [ref[pl.ds(b * n_mem * pitch + ck * X_HEADS + h, n_mem, stride=pitch), :] for ck in range(n_chunk)]
        return jnp.concatenate(parts, axis=1).astype(BF16)

    pairs = [(b, h) for b in range(nb) for h in range(X_HEADS)]
    scores = []
    for b, h in pairs:
        r0 = (b // per) * rows
        q = q_ref[r0:r0 + rows, h * hd:(h + 1) * hd].astype(BF16)
        scores.append(lax.dot_general(q, head_rows(k_ref, b, h), _CONTRACT_LAST, preferred_element_type=F32) * scale)
    probs = []
    for s in scores:
        e = jnp.exp(s - jnp.max(s, axis=-1, keepdims=True))
        probs.append((e / jnp.sum(e, axis=-1, keepdims=True)).astype(BF16))
    outs = {}
    for (b, h), p in zip(pairs, probs):
        outs[b, h] = _dot(p, head_rows(v_ref, b, h))
    for g in range(nb // per):
        for h in range(X_HEADS):
            o = outs[g * per, h]
            for e in range(1, per):
                o = jnp.where((rid >> shift) == e, outs[g * per + e, h], o)
            o_ref[g * rows:(g + 1) * rows, h * hd:(h + 1) * hd] = o


def _xattn_step(q, mem_k, mem_v, *, seg, nb=4):
    n, dm = q.shape
    bsz, m, _, hd = mem_k.shape
    body = functools.partial(_xattn_step_body, nb=nb, seg=seg, hd=hd, n_mem=m)
    rows_per_seq = m * dm // LANES
    mspec = pl.BlockSpec((nb * rows_per_seq, LANES), lambda i: (i, 0))
    return pl.pallas_call(
        body,
        grid=(bsz // nb,),
        in_specs=[pl.BlockSpec((nb * seg, dm), lambda i: (i, 0)), mspec, mspec],
        out_specs=pl.BlockSpec((nb * seg, dm), lambda i: (i, 0)),
        out_shape=jax.ShapeDtypeStruct((n, dm), F32),
        compiler_params=_params("arbitrary"),
        name="xattn_step",
    )(q, _cache_rows(mem_k), _cache_rows(mem_v))


def _ffn_finish(conv, gate, wd_ref, o_ref, gfin_ref, j, nj, tm):
    act = (_gelu(conv) * gate).astype(BF16)
    o_ref[...] += _dot(act, wd_ref[...].astype(BF16))

    @pl.when(j == nj - 1)
    def _():
        def f(r0):
            o_ref[pl.ds(r0, 128), :] = _rms(o_ref[pl.ds(r0, 128), :], gfin_ref[...])
        _for_row_chunks(tm, 128, f)


def _ffn_seq_body(x_ref, gain_ref, wa_ref, wg_ref, wd_ref, cw_ref, cb_ref, gfin_ref, o_ref, tail_ref, h_ref, a_ref,
                  *, tm, halo, tiles_per_seq, nj):
    i = pl.program_id(0)
    j = pl.program_id(1)

    @pl.when(j == 0)
    def _():
        @pl.when(i % tiles_per_seq == 0)
        def _():
            h_ref[0:halo, :] = jnp.zeros((halo, h_ref.shape[1]), BF16)

        @pl.when(i % tiles_per_seq != 0)
        def _():
            h_ref[0:halo, :] = h_ref[tm:tm + halo, :]

        _rms_rows_to_bf16(x_ref, gain_ref, h_ref, tm, row0=halo)
        o_ref[...] = x_ref[...]

    a_ref[...] = _dot(h_ref[...], wa_ref[...].astype(BF16))
    gate = _dot(h_ref[halo:halo + tm, :], wg_ref[...].astype(BF16))
    conv = cb_ref[...]
    for tap in range(CONV_W):
        conv = conv + a_ref[pl.ds(halo - (CONV_W - 1) + tap, tm), :] * cw_ref[tap:tap + 1, :]
    tail_ref[...] = a_ref[pl.ds(halo + tm - SUBLANES, SUBLANES), :]
    _ffn_finish(conv, gate, wd_ref, o_ref, gfin_ref, j, nj, tm)


def _ffn_seq(x, gain, w_up, conv_w, conv_b, w_down, gain_final, *, tm, seq, tf=512, halo=16):
    n, dm = x.shape
    dff = w_down.shape[0]
    nj = dff // tf
    body = functools.partial(_ffn_seq_body, tm=tm, halo=halo, tiles_per_seq=seq // tm, nj=nj)
    return pl.pallas_call(
        body,
        grid=(n // tm, nj),
        in_specs=[
            pl.BlockSpec((tm, dm), lambda i, j: (i, 0)),
            pl.BlockSpec((1, dm), lambda i, j: (0, 0)),
            pl.BlockSpec((dm, tf), lambda i, j: (0, j)),
            pl.BlockSpec((dm, tf), lambda i, j: (0, nj + j)),
            pl.BlockSpec((tf, dm), lambda i, j: (j, 0)),
            pl.BlockSpec((CONV_W, tf), lambda i, j: (0, j)),
            pl.BlockSpec((1, tf), lambda i, j: (0, j)),
            pl.BlockSpec((1, dm), lambda i, j: (0, 0)),
        ],
        out_specs=(pl.BlockSpec((tm, dm), lambda i, j: (i, 0)),
                   pl.BlockSpec((None, SUBLANES, tf), lambda i, j: (i, 0, j))),
        out_shape=(jax.ShapeDtypeStruct((n, dm), F32),
                   jax.ShapeDtypeStruct((n // tm, SUBLANES, dff), F32)),
        scratch_shapes=[pltpu.VMEM((halo + tm, dm), BF16), pltpu.VMEM((halo + tm, tf), F32)],
        compiler_params=_params("arbitrary", "arbitrary"),
        name="ffn_seq",
    )(x, gain.reshape(1, dm), w_up, w_up, w_down, conv_w, conv_b.reshape(1, dff), gain_final.reshape(1, dm))


def _dot_exact_rows(sel_b, x):
    hi, mid, lo = _split3_bf16(x)
    return _dot(sel_b, hi) + _dot(sel_b, mid) + _dot(sel_b, lo)


def _ffn_step_body(x_ref, gain_ref, wa_ref, wg_ref, wd_ref, cw_ref, cb_ref, gfin_ref, b0_ref, b1_ref,
                   o_ref, a_new_ref, h_ref, *, tm, seg, nj):
    j = pl.program_id(1)
    nseq = tm // seg
    shift = seg.bit_length() - 1

    @pl.when(j == 0)
    def _():
        _rms_rows_to_bf16(x_ref, gain_ref, h_ref, tm)
        o_ref[...] = x_ref[...]

    a = _dot(h_ref[...], wa_ref[...].astype(BF16))
    gate = _dot(h_ref[...], wg_ref[...].astype(BF16))
    t_in = lax.broadcasted_iota(jnp.int32, (tm, 1), 0) & (seg - 1)
    prev1 = jnp.where(t_in >= 1, pltpu.roll(a, 1, 0), 0.0)
    prev2 = jnp.where(t_in >= 2, pltpu.roll(a, 2, 0), 0.0)
    conv = cb_ref[...] + prev2 * cw_ref[0:1, :]
    conv = conv + prev1 * cw_ref[1:2, :]
    conv = conv + a * cw_ref[2:3, :]
    b0 = b0_ref[...]
    b1 = b1_ref[...]
    carried = jnp.concatenate([b0 * cw_ref[0:1, :] + b1 * cw_ref[1:2, :], b1 * cw_ref[0:1, :]], axis=0)
    r = lax.broadcasted_iota(jnp.int32, (tm, 2 * nseq), 0)
    c = lax.broadcasted_iota(jnp.int32, (tm, 2 * nseq), 1)
    spread = ((r >> shift) == (c & (nseq - 1))) & ((r & (seg - 1)) == (c >> (nseq.bit_length() - 1)))
    conv = conv + _dot_exact_rows(jnp.where(spread, 1.0, 0.0).astype(BF16), carried)
    r = lax.broadcasted_iota(jnp.int32, (2 * nseq, tm), 0)
    c = lax.broadcasted_iota(jnp.int32, (2 * nseq, tm), 1)
    pick = ((c >> shift) == (r & (nseq - 1))) & ((c & (seg - 1)) == seg - 2 + (r >> (nseq.bit_length() - 1)))
    picked = _dot_exact_rows(jnp.where(pick, 1.0, 0.0).astype(BF16), a)
    a_new_ref[0] = picked[0:nseq]
    a_new_ref[1] = picked[nseq:2 * nseq]
    _ffn_finish(conv, gate, wd_ref, o_ref, gfin_ref, j, nj, tm)


def _ffn_step(x, gain, w_up, conv_w, conv_b, w_down, gain_final, b0, b1, *, seg, tf=512):
    n, dm = x.shape
    dff = w_down.shape[0]
    nj = dff // tf
    tm = n
    nseq = n // seg
    body = functools.partial(_ffn_step_body, tm=tm, seg=seg, nj=nj)
    return pl.pallas_call(
        body,
        grid=(1, nj),
        in_specs=[
            pl.BlockSpec((tm, dm), lambda i, j: (i, 0)),
            pl.BlockSpec((1, dm), lambda i, j: (0, 0)),
            pl.BlockSpec((dm, tf), lambda i, j: (0, j)),
            pl.BlockSpec((dm, tf), lambda i, j: (0, nj + j)),
            pl.BlockSpec((tf, dm), lambda i, j: (j, 0)),
            pl.BlockSpec((CONV_W, tf), lambda i, j: (0, j)),
            pl.BlockSpec((1, tf), lambda i, j: (0, j)),
            pl.BlockSpec((1, dm), lambda i, j: (0, 0)),
            pl.BlockSpec((nseq, tf), lambda i, j: (0, j)),
            pl.BlockSpec((nseq, tf), lambda i, j: (0, j)),
        ],
        out_specs=(pl.BlockSpec((tm, dm), lambda i, j: (i, 0)),
                   pl.BlockSpec((CONV_W - 1, nseq, tf), lambda i, j: (0, 0, j))),
        out_shape=(jax.ShapeDtypeStruct((n, dm), F32), jax.ShapeDtypeStruct((CONV_W - 1, nseq, dff), F32)),
        scratch_shapes=[pltpu.VMEM((tm, dm), BF16)],
        compiler_params=_params("arbitrary", "arbitrary"),
        name="ffn_step",
    )(x, gain.reshape(1, dm), w_up, w_up, w_down, conv_w, conv_b.reshape(1, dff), gain_final.reshape(1, dm),
      b0, b1)


def kernel(x_prompt, x_sample, mem_prompt, cache_mem_k, cache_mem_v, state_s5_re, state_s5_im, state_gla, state_conv, norm_mix, w_in, s5_lambda_re, s5_lambda_im, s5_log_dt, s5_b_re, s5_b_im, s5_c_re, s5_c_im, s5_d, s5_w_glu, s5_b_glu, norm_s5_out, gla_w_g2, gla_b_g, norm_gla_out, w_out, norm_xattn, norm_mem, w_xq, w_xk, w_xv, w_xo, norm_ffn, w_up, conv_w, conv_b, w_down, norm_final):
    depth = w_in.shape[0]
    assert depth == 1
    bp, seq, dm = x_prompt.shape
    bs, seg, _ = x_sample.shape
    n_mem = mem_prompt.shape[1]
    dff = w_down.shape[1]
    l = 0
    s5_args = (s5_lambda_re[l], s5_lambda_im[l], s5_log_dt[l], s5_b_re[l], s5_b_im[l], s5_c_re[l], s5_c_im[l],
               s5_d[l], s5_w_glu[l], s5_b_glu[l])
    tm_p = min(1024, seq)
    n_s = bs * seg
    w_in, w_out, w_xq, w_xo, w_up, w_down = (w.astype(BF16) for w in (w_in, w_out, w_xq, w_xo, w_up, w_down))

    mk, mk_b = _mem_proj("mem_k", mem_prompt, norm_mem[l], w_xk[l], tm=min(1024, bp * n_mem))
    mv, mv_b = _mem_proj("mem_v", mem_prompt, norm_mem[l], w_xv[l], tm=min(1024, bp * n_mem))

    xp = x_prompt.reshape(bp * seq, dm)
    proj = _in_proj(xp, norm_mix[l], w_in[l], gla_w_g2[l], gla_b_g[l], tm=tm_p)
    proj3 = proj.reshape(bp, seq, -1)
    s5_w, s5_powers = _s5_weights(*s5_args)
    steps_p = SUBLANES // bp
    z_s5, hend_r, hend_i = _s5_mixer(proj3, s5_w, _s5_scan_table(s5_powers, steps_p, bp),
                                     tt=min(512 // bp, seq), steps=steps_p)
    y_gla, st_gla = _gla_seq(proj3, norm_gla_out[l], tt=min(512, seq))
    x2 = _mix_xattn_seq(z_s5.reshape(bp * seq, -1), y_gla.reshape(bp * seq, -1), xp, norm_s5_out[l], norm_xattn[l],
                        w_out[l], w_xq[l], w_xo[l], mk_b, mv_b, tm=tm_p, seq=seq)
    yp, tails = _ffn_seq(x2, norm_ffn[l], w_up[l], conv_w[l], conv_b[l], w_down[l], norm_final, tm=tm_p, seq=seq)

    g_total = s5_lambda_re.shape[1]
    n_state = s5_lambda_re.shape[2]
    p_re = jnp.swapaxes(hend_r[:, SUBLANES - bp:], 0, 1).reshape(1, bp, g_total, n_state)
    p_im = jnp.swapaxes(hend_i[:, SUBLANES - bp:], 0, 1).reshape(1, bp, g_total, n_state)
    p_gla = jnp.swapaxes(st_gla, -1, -2)[None]
    tiles_per_seq = seq // tm_p
    p_conv = tails.reshape(bp, tiles_per_seq, SUBLANES, dff)[:, -1, SUBLANES - (CONV_W - 1):, :][None]

    xs = x_sample.reshape(n_s, dm)
    proj_s = _in_proj(xs, norm_mix[l], w_in[l], gla_w_g2[l], gla_b_g[l], tm=n_s)
    h0r = state_s5_re[l].reshape(1, bs, -1)
    h0i = state_s5_im[l].reshape(1, bs, -1)
    z_s5_s, hs_r, hs_i = _s5_mixer(proj_s[None], s5_w, _s5_scan_table(s5_powers, seg, 1), tt=n_s, steps=seg,
                                   h0=(h0r, h0i))
    y_gla_s, s_gla = _gla_step(proj_s, norm_gla_out[l], state_gla[l], seg=seg)
    x1s = _out_proj(z_s5_s[0], y_gla_s, norm_s5_out[l], w_out[l], xs, tm=n_s)
    qs = _norm_linear("xattn_q_s", x1s, norm_xattn[l], w_xq[l], tm=n_s)
    att = _xattn_step(qs, cache_mem_k[l], cache_mem_v[l], seg=seg)
    x2s = _cast_linear("xattn_o_s", att, w_xo[l], x1s, tm=n_s)
    buf = state_conv[l]
    ys, a_new = _ffn_step(x2s, norm_ffn[l], w_up[l], conv_w[l], conv_b[l], w_down[l], norm_final,
                          buf[:, 0], buf[:, 1], seg=seg)

    s_re = hs_r.reshape(1, bs, g_total, n_state)
    s_im = hs_i.reshape(1, bs, g_total, n_state)
    s_conv = jnp.swapaxes(a_new, 0, 1)[None]

    return (yp.reshape(bp, seq, dm), ys.reshape(bs, seg, dm), mk, mv,
            p_re, p_im, p_gla, p_conv, s_re, s_im, s_gla[None], s_conv)
```

```python
import functools

import jax
import jax.numpy as jnp
from jax import lax
from jax.experimental import pallas as pl
from jax.experimental.pallas import tpu as pltpu

F32 = jnp.float32
BF16 = jnp.bfloat16

EPS = 1e-6
S5_GROUP = 16
S5_BLOCK_GROUPS = 16
GLA_HEADS = 4
GLA_RANK = 16
GLA_TAU = 16.0
GLA_CHUNK = 64
X_HEADS = 4
CONV_W = 3

SUBLANES = 8
LANES = 128
VMEM_LIMIT_BYTES = 56 * 1024 * 1024

_CONTRACT_LAST = (((1,), (1,)), ((), ()))
_CONTRACT_FIRST = (((0,), (0,)), ((), ()))


def _params(*semantics):
    return pltpu.CompilerParams(dimension_semantics=semantics, vmem_limit_bytes=VMEM_LIMIT_BYTES)


def _dot(a, b):
    return jnp.dot(a, b, preferred_element_type=F32)


def _for_row_chunks(n_rows, chunk, fn):
    def body(i, carry):
        fn(pl.multiple_of(i * chunk, chunk))
        return carry
    lax.fori_loop(0, n_rows // chunk, body, 0)


def _rms(x, gain):
    r = lax.rsqrt(jnp.mean(x * x, axis=-1, keepdims=True) + EPS)
    return (x * r) * gain


def _gelu(x):
    return 0.5 * x * (1.0 + lax.erf(x * (2.0 ** -0.5)))


def _rms_rows_to_bf16(x_ref, gain_ref, h_ref, n_rows, row0=0, chunk=128):
    def f(r0):
        h_ref[pl.ds(row0 + r0, chunk), :] = _rms(x_ref[pl.ds(r0, chunk), :], gain_ref[...]).astype(BF16)
    _for_row_chunks(n_rows, chunk, f)


def _s5_prep_body(lr_ref, li_ref, ldt_ref, br_ref, bi_ref, apr_ref, api_ref, bbr_ref, bbi_ref):
    lr = lr_ref[...]
    li = li_ref[...]
    dt = jnp.exp(ldt_ref[...])
    mag = jnp.exp(lr * dt)
    ar = mag * jnp.cos(li * dt)
    ai = mag * jnp.sin(li * dt)
    den = lr * lr + li * li
    pr = ar - 1.0
    qr = (pr * lr + ai * li) / den
    qi = (ai * lr - pr * li) / den
    for p in range(S5_GROUP):
        br = br_ref[p]
        bi = bi_ref[p]
        bbr_ref[p] = qr * br - qi * bi
        bbi_ref[p] = qr * bi + qi * br
    cr, ci = ar, ai
    apr_ref[0] = cr
    api_ref[0] = ci
    for k in range(1, SUBLANES):
        cr, ci = cr * ar - ci * ai, cr * ai + ci * ar
        apr_ref[k] = cr
        api_ref[k] = ci


def _s5_prep(lam_re, lam_im, log_dt, b_re, b_im):
    g, n = lam_re.shape
    out = jax.ShapeDtypeStruct((SUBLANES, g, n), F32)
    outb = jax.ShapeDtypeStruct((S5_GROUP, g, n), F32)
    return pl.pallas_call(
        _s5_prep_body, out_shape=(out, out, outb, outb), name="s5_prep",
    )(lam_re, lam_im, log_dt.reshape(g, 1), jnp.transpose(b_re, (2, 0, 1)), jnp.transpose(b_im, (2, 0, 1)))


def _s5_weights(lam_re, lam_im, log_dt, b_re, b_im, c_re, c_im, d, w_glu, b_glu):
    g, n = lam_re.shape
    p = S5_GROUP
    gb = S5_BLOCK_GROUPS
    nb = g // gb
    apr, api, bbr, bbi = _s5_prep(lam_re, lam_im, log_dt, b_re, b_im)

    def same_group(rows, row_w, cols, col_w):
        return (jnp.arange(rows)[:, None] // row_w) == (jnp.arange(cols)[None, :] // col_w)

    def tile_cols(x, reps):
        w = x.shape[-1]
        sel = (jnp.arange(reps * w)[None, :] % w == jnp.arange(w)[:, None]).astype(BF16)
        return jnp.einsum('brw,wj->brj', x, sel, preferred_element_type=F32)

    def tile_rows(x, reps):
        h = x.shape[1]
        sel = (jnp.arange(reps * h)[:, None] % h == jnp.arange(h)[None, :]).astype(BF16)
        return jnp.einsum('rh,bhc->brc', sel, x, preferred_element_type=F32)

    def bu_block(bb):
        per_block = jnp.transpose(bb.reshape(p, nb, gb * n), (1, 0, 2)).astype(BF16)
        return jnp.where(same_group(gb * p, p, gb * n, n), tile_rows(per_block, gb), 0.0)

    def c_block(c):
        per_block = jnp.transpose(c.reshape(nb, gb, p, n), (0, 1, 3, 2)).reshape(nb, gb * n, p).astype(BF16)
        return jnp.where(same_group(gb * n, n, gb * p, p), tile_cols(per_block, gb), 0.0)

    w_bu = jnp.concatenate([bu_block(bbr), bu_block(bbi)], axis=-1).astype(BF16)
    w_c = jnp.concatenate([c_block(c_re), -c_block(c_im)], axis=1).astype(BF16)
    w_g = jnp.where(same_group(gb * p, p, gb * p, p),
                    tile_cols(w_glu.reshape(nb, gb * p, p).astype(BF16), gb), 0.0).astype(BF16)
    d_b = d.reshape(nb, 1, gb * p)
    bg_b = b_glu.reshape(nb, 1, gb * p)
    pw_r = apr.reshape(SUBLANES, nb, gb * n)
    pw_i = api.reshape(SUBLANES, nb, gb * n)
    return (w_bu, w_c, w_g, d_b, bg_b), (pw_r, pw_i)


def _s5_scan_table(powers, steps, rows_per_step):
    pw_r, pw_i = powers
    t = (jnp.arange(SUBLANES) // rows_per_step) % steps

    def masked(k):
        m = (t >= k)[:, None, None]
        return jnp.where(m, pw_r[k - 1][None], 0.0), jnp.where(m, pw_i[k - 1][None], 0.0)

    tabs = []
    k = 1
    while k < steps:
        tabs.extend(masked(k))
        k *= 2
    tabs.append(pw_r[t])
    tabs.append(pw_i[t])
    whole = jnp.full((SUBLANES,), steps - 1)
    tabs.append(pw_r[whole])
    tabs.append(pw_i[whole])
    return jnp.transpose(jnp.stack(tabs), (2, 0, 1, 3))


def _in_proj_body(x_ref, gain_ref, w_ref, wr_ref, wg_ref, wg2_ref, bg_ref, o_ref, h_ref, *, tm, n_main, n_r):
    j = pl.program_id(1)

    @pl.when(j == 0)
    def _():
        _rms_rows_to_bf16(x_ref, gain_ref, h_ref, tm)

    @pl.when(j < n_main)
    def _():
        o_ref[...] = _dot(h_ref[...], w_ref[...].astype(BF16))

    @pl.when((j >= n_main) & (j < n_main + n_r))
    def _():
        o_ref[...] = _dot(h_ref[...], wr_ref[...].astype(BF16))

    @pl.when(j == n_main + n_r)
    def _():
        g_low = _dot(h_ref[...], wg_ref[...].astype(BF16))
        pre = _dot(g_low.astype(BF16), wg2_ref[...].astype(BF16)) + bg_ref[...]
        log_sig = jnp.minimum(pre, 0.0) - jnp.log1p(jnp.exp(-jnp.abs(pre)))
        o_ref[...] = log_sig * (1.0 / GLA_TAU)


def _in_proj(x, gain, w_in, w_g2, b_g, *, tm, tn=512):
    n, dm = x.shape
    mix = dm
    n_u_q_k_v = mix // 2 + 2 * (mix // 4) + mix // 2
    col_g = n_u_q_k_v
    col_r = n_u_q_k_v + GLA_RANK
    w_r = w_in[:, col_r:]
    w_g = w_in[:, col_g:col_r]
    n_main = n_u_q_k_v // tn
    n_r = w_r.shape[1] // tn
    n_cols = n_u_q_k_v + w_r.shape[1] + w_g2.shape[1]
    body = functools.partial(_in_proj_body, tm=tm, n_main=n_main, n_r=n_r)
    return pl.pallas_call(
        body,
        grid=(n // tm, n_main + n_r + 1),
        in_specs=[
            pl.BlockSpec((tm, dm), lambda i, j: (i, 0)),
            pl.BlockSpec((1, dm), lambda i, j: (0, 0)),
            pl.BlockSpec((dm, tn), lambda i, j: (0, jnp.minimum(j, n_main - 1))),
            pl.BlockSpec((dm, tn), lambda i, j: (0, jnp.clip(j - n_main, 0, n_r - 1))),
            pl.BlockSpec((dm, GLA_RANK), lambda i, j: (0, 0)),
            pl.BlockSpec(w_g2.shape, lambda i, j: (0, 0)),
            pl.BlockSpec((1, w_g2.shape[1]), lambda i, j: (0, 0)),
        ],
        out_specs=pl.BlockSpec((tm, tn), lambda i, j: (i, j)),
        out_shape=jax.ShapeDtypeStruct((n, n_cols), F32),
        scratch_shapes=[pltpu.VMEM((tm, dm), BF16)],
        compiler_params=_params("arbitrary", "arbitrary"),
        name="in_proj",
    )(x, gain.reshape(1, dm), w_in, w_r, w_g, w_g2, b_g.reshape(1, -1))


def _s5_body(*refs, tt, steps, nseq, ns, chunk):
    carried = nseq > 1
    if carried:
        (u_ref, wbu_ref, wc_ref, wg_ref, d_ref, bg_ref, tab_ref, z_ref, hr_ref, hi_ref,
         bu_ref, cr_ref, ci_ref, us_ref, zs_ref) = refs
    else:
        u_ref, wbu_ref, wc_ref, wg_ref, d_ref, bg_ref, tab_ref, h0r_ref, h0i_ref, z_ref, hr_ref, hi_ref, bu_ref = refs
    seg = steps
    n_lvl = steps.bit_length() - 1
    lane_group = 2 * LANES
    rid = lax.broadcasted_iota(jnp.int32, (SUBLANES, 1), 0)
    cw = u_ref.shape[-1]
    n_lt = cw // LANES

    if carried:
        @pl.when(pl.program_id(1) == 0)
        def _():
            cr_ref[...] = jnp.zeros_like(cr_ref)
            ci_ref[...] = jnp.zeros_like(ci_ref)

        per = chunk // nseq

    def chunk_u(c):
        if not carried:
            return u_ref[c * chunk:(c + 1) * chunk, :]
        for b in range(nseq):
            for lt in range(n_lt):
                us_ref[c * n_lt + lt, pl.ds(b, per, stride=nseq), :] = (
                    u_ref[b, c * per:(c + 1) * per, lt * LANES:(lt + 1) * LANES])
        return jnp.concatenate([us_ref[c * n_lt + lt] for lt in range(n_lt)], axis=1)

    def project_in(c):
        rows = slice(c * chunk, (c + 1) * chunk)
        bu = _dot(chunk_u(c).astype(BF16), wbu_ref[...])
        if not carried:
            seqs = slice(c * chunk // seg, (c + 1) * chunk // seg)
            a_r = tab_ref[2 * n_lvl, 0:1, :]
            a_i = tab_ref[2 * n_lvl + 1, 0:1, :]
            h0r = h0r_ref[seqs, :]
            h0i = h0i_ref[seqs, :]
            first = (lax.broadcasted_iota(jnp.int32, (chunk, chunk // seg), 0)
                     == seg * lax.broadcasted_iota(jnp.int32, (chunk, chunk // seg), 1))
            first_b = jnp.where(first, 1.0, 0.0).astype(BF16)
            bu = bu + jnp.concatenate([_dot_exact_rows(first_b, a_r * h0r - a_i * h0i),
                                       _dot_exact_rows(first_b, a_r * h0i + a_i * h0r)], axis=1)
        bu_ref[rows, :] = bu

    def scan(c, carry):
        carry = list(carry)
        for blk in range(chunk // SUBLANES):
            rows = slice(c * chunk + blk * SUBLANES, c * chunk + (blk + 1) * SUBLANES)
            for gi in range(ns // lane_group):
                cols = slice(gi * lane_group, (gi + 1) * lane_group)
                cols_im = slice(ns + gi * lane_group, ns + (gi + 1) * lane_group)
                xr = bu_ref[rows, cols]
                xi = bu_ref[rows, cols_im]
                for lvl in range(n_lvl):
                    k = 1 << lvl
                    ar = tab_ref[2 * lvl, :, cols]
                    ai = tab_ref[2 * lvl + 1, :, cols]
                    sr = pltpu.roll(xr, k * nseq, 0)
                    si = pltpu.roll(xi, k * nseq, 0)
                    xr, xi = xr + (ar * sr - ai * si), xi + (ar * si + ai * sr)
                if carried:
                    pr = tab_ref[2 * n_lvl, :, cols]
                    pi = tab_ref[2 * n_lvl + 1, :, cols]
                    c_r, c_i = carry[gi]
                    hr = xr + (pr * c_r - pi * c_i)
                    hi = xi + (pr * c_i + pi * c_r)
                    lr, li = xr, xi
                    half = SUBLANES // 2
                    while half >= nseq:
                        keep = (rid & (2 * half - 1)) >= half
                        lr = jnp.where(keep, lr, pltpu.roll(lr, SUBLANES - half, 0))
                        li = jnp.where(keep, li, pltpu.roll(li, SUBLANES - half, 0))
                        half //= 2
                    er = tab_ref[2 * n_lvl + 2, :, cols]
                    ei = tab_ref[2 * n_lvl + 3, :, cols]
                    carry[gi] = (lr + (er * c_r - ei * c_i), li + (er * c_i + ei * c_r))
                else:
                    hr, hi = xr, xi
                    for e in range(SUBLANES // seg):
                        src = e * seg + seg - 1
                        dst = (c * chunk + blk * SUBLANES) // seg + e
                        hr_ref[dst:dst + 1, cols] = hr[src:src + 1]
                        hi_ref[dst:dst + 1, cols] = hi[src:src + 1]
                bu_ref[rows, cols] = hr
                bu_ref[rows, cols_im] = hi
        return carry

    def project_out(c):
        rows = slice(c * chunk, (c + 1) * chunk)
        if carried:
            u = jnp.concatenate([us_ref[c * n_lt + lt] for lt in range(n_lt)], axis=1)
        else:
            u = u_ref[rows, :]
        y = _dot(bu_ref[rows, :].astype(BF16), wc_ref[...]) + d_ref[...] * u
        z = _gelu(y)
        z = z * jax.nn.sigmoid(_dot(z.astype(BF16), wg_ref[...]) + bg_ref[...])
        if carried:
            for lt in range(n_lt):
                zs_ref[c * n_lt + lt] = z[:, lt * LANES:(lt + 1) * LANES]
            for b in range(nseq):
                for lt in range(n_lt):
                    z_ref[b, c * per:(c + 1) * per, lt * LANES:(lt + 1) * LANES] = (
                        zs_ref[c * n_lt + lt, pl.ds(b, per, stride=nseq), :])
        else:
            z_ref[rows, :] = z

    n_chunks = tt // chunk
    n_groups = ns // lane_group
    group_cols = [slice(gi * lane_group, (gi + 1) * lane_group) for gi in range(n_groups)]
    carry = [(cr_ref[:, cols], ci_ref[:, cols]) for cols in group_cols] if carried else [None] * n_groups
    project_in(0)
    for c in range(n_chunks):
        if c + 1 < n_chunks:
            project_in(c + 1)
        carry = scan(c, carry)
        project_out(c)
    if carried:
        for cols, (c_r, c_i) in zip(group_cols, carry):
            cr_ref[:, cols] = c_r
            ci_ref[:, cols] = c_i
            hr_ref[:, cols] = c_r
            hi_ref[:, cols] = c_i


def _s5_mixer(proj3, weights, tab, *, tt, steps, h0=None, chunk=128):
    w_bu, w_c, w_g, d_b, bg_b = weights
    bsz, seq, _ = proj3.shape
    nb, cw, ns2 = w_bu.shape
    ns = ns2 // 2
    n_tab = tab.shape[1]
    carried = h0 is None
    nseq = bsz if carried else 1
    assert nseq * steps == SUBLANES or not carried
    rows = nseq * tt
    body = functools.partial(_s5_body, tt=rows, steps=steps, nseq=nseq, ns=ns, chunk=min(chunk, rows))
    wspec = lambda shape: pl.BlockSpec((None,) + shape, lambda g, t: (g,) + (0,) * len(shape))
    lead = bsz if carried else None
    uz_spec = pl.BlockSpec((lead, tt, cw), lambda g, t: (0, t, g))
    in_specs = [
        uz_spec,
        wspec((cw, ns2)), wspec((ns2, cw)), wspec((cw, cw)), wspec((1, cw)), wspec((1, cw)),
        wspec((n_tab, SUBLANES, ns)),
    ]
    args = [proj3, w_bu, w_c, w_g, d_b, bg_b, tab]
    z_shape = jax.ShapeDtypeStruct((bsz, seq, nb * cw), F32)
    scratch = [pltpu.VMEM((rows, ns2), F32)]
    if carried:
        st_shape = jax.ShapeDtypeStruct((nb, SUBLANES, ns), F32)
        st_spec = pl.BlockSpec((None, SUBLANES, ns), lambda g, t: (g, 0, 0))
        slabs = (rows // min(chunk, rows)) * (cw // LANES)
        scratch += [pltpu.VMEM((SUBLANES, ns), F32), pltpu.VMEM((SUBLANES, ns), F32),
                    pltpu.VMEM((slabs, min(chunk, rows), LANES), F32), pltpu.VMEM((slabs, min(chunk, rows), LANES), F32)]
    else:
        assert bsz == 1
        h0r, h0i = h0
        st_spec = pl.BlockSpec((None, tt // steps, ns), lambda g, t: (0, t, g))
        in_specs += [st_spec] * 2
        args += [h0r, h0i]
        st_shape = jax.ShapeDtypeStruct((bsz, seq // steps, nb * ns), F32)
    return pl.pallas_call(
        body,
        grid=(nb, seq // tt),
        in_specs=in_specs,
        out_specs=(uz_spec, st_spec, st_spec),
        out_shape=(z_shape, st_shape, st_shape),
        scratch_shapes=scratch,
        compiler_params=_params("arbitrary", "arbitrary"),
        name="s5_carried" if carried else "s5_segments",
    )(*args)


def _split3_bf16(x):
    hi = x.astype(BF16)
    r1 = x - hi.astype(F32)
    mid = r1.astype(BF16)
    lo = (r1 - mid.astype(F32)).astype(BF16)
    return hi, mid, lo


def _gla_out(o, gain, r):
    return _rms(o, gain) * (r * jax.nn.sigmoid(r))


def _gla_seq_body(q_ref, k_ref, v_ref, la_ref, r_ref, gain_ref, y_ref, st_out_ref, st_ref, *, tt, dk, dv):
    c = GLA_CHUNK
    hh = GLA_HEADS

    @pl.when(pl.program_id(1) == 0)
    def _():
        st_ref[...] = jnp.zeros_like(st_ref)

    row = lax.broadcasted_iota(jnp.int32, (tt, tt), 0)
    col = lax.broadcasted_iota(jnp.int32, (tt, tt), 1)
    shift = c.bit_length() - 1
    tri = (col <= row) & ((row >> shift) == (col >> shift))
    tri_b = jnp.where(tri, 1.0, 0.0).astype(BF16)
    hi, mid, lo = _split3_bf16(la_ref[...])
    bcum = _dot(tri_b, hi) + _dot(tri_b, mid) + _dot(tri_b, lo)
    causal = lax.broadcasted_iota(jnp.int32, (c, c), 1) <= lax.broadcasted_iota(jnp.int32, (c, c), 0)
    scale = dk ** -0.5
    chunks = [slice(ci * c, (ci + 1) * c) for ci in range(tt // c)]
    work = []
    for h in range(hh):
        ks = slice(h * dk, (h + 1) * dk)
        vs = slice(h * dv, (h + 1) * dv)
        qts, decays, o_intra, kvs = [], [], [], []
        for sl in chunks:
            bc = bcum[sl, ks]
            bl = bc[c - 1:c]
            qt = ((q_ref[sl, ks] * scale) * jnp.exp(bc)).astype(BF16)
            kc = k_ref[sl, ks]
            kt = (kc * jnp.exp(-bc)).astype(BF16)
            ke = (kc * jnp.exp(bl - bc)).astype(BF16)
            vb = v_ref[sl, vs].astype(BF16)
            s = lax.dot_general(qt, kt, _CONTRACT_LAST, preferred_element_type=F32)
            s = jnp.where(causal, s, 0.0)
            qts.append(qt)
            decays.append(jnp.exp(bl))
            o_intra.append(_dot(s.astype(BF16), vb))
            kvs.append(lax.dot_general(vb, ke, _CONTRACT_FIRST, preferred_element_type=F32))
        work.append((vs, qts, decays, o_intra, kvs))
    for h, (vs, qts, decays, o_intra, kvs) in enumerate(work):
        st = st_ref[h]
        for sl, qt, dec, oi, kv in zip(chunks, qts, decays, o_intra, kvs):
            o = oi + lax.dot_general(qt, st.astype(BF16), _CONTRACT_LAST, preferred_element_type=F32)
            st = st * dec + kv
            y_ref[sl, vs] = _gla_out(o, gain_ref[...], r_ref[sl, vs]).astype(y_ref.dtype)
        st_ref[h] = st
        st_out_ref[h] = st


def _gla_seq(proj3, gain, *, tt):
    bsz, seq, _ = proj3.shape
    hh = GLA_HEADS
    dv = gain.shape[0]
    dk = dv // 2
    wk, wv = hh * dk, hh * dv
    col_q = 1024 // wk
    col_k = col_q + 1
    col_v = (1024 + 2 * wk) // wv
    col_r = col_v + 1
    col_la = (1024 + 2 * wk + 2 * wv) // wk
    spec = lambda w, c0: pl.BlockSpec((None, tt, w), lambda b, t: (b, t, c0))
    body = functools.partial(_gla_seq_body, tt=tt, dk=dk, dv=dv)
    return pl.pallas_call(
        body,
        grid=(bsz, seq // tt),
        in_specs=[spec(wk, col_q), spec(wk, col_k), spec(wv, col_v), spec(wk, col_la), spec(wv, col_r),
                  pl.BlockSpec((1, dv), lambda b, t: (0, 0))],
        out_specs=(pl.BlockSpec((None, tt, wv), lambda b, t: (b, t, 0)),
                   pl.BlockSpec((None, hh, dv, dk), lambda b, t: (b, 0, 0, 0))),
        out_shape=(jax.ShapeDtypeStruct((bsz, seq, wv), BF16),
                   jax.ShapeDtypeStruct((bsz, hh, dv, dk), F32)),
        scratch_shapes=[pltpu.VMEM((hh, dv, dk), F32)],
        compiler_params=_params("arbitrary", "arbitrary"),
        name="gla_seq",
    )(proj3, proj3, proj3, proj3, proj3, gain.reshape(1, dv))


def _gla_step_body(q_ref, k_ref, v_ref, la_ref, r_ref, gain_ref, s_ref, y_ref, so_ref, *, nb, seg, dk, dv):
    hh = GLA_HEADS
    rows = SUBLANES
    per = rows // seg
    row = lax.broadcasted_iota(jnp.int32, (rows, rows), 0)
    col = lax.broadcasted_iota(jnp.int32, (rows, rows), 1)
    shift = seg.bit_length() - 1
    causal = (col <= row) & ((row >> shift) == (col >> shift))
    rid = lax.broadcasted_iota(jnp.int32, (rows, 1), 0)
    t_in = rid & (seg - 1)
    ones_b = jnp.ones((rows, dv), BF16)
    scale = dk ** -0.5

    def pair(i, carry):
        r0 = pl.multiple_of(i * rows, rows)
        prep = []
        for h in range(hh):
            ks = slice(h * dk, (h + 1) * dk)
            vs = slice(h * dv, (h + 1) * dv)
            bc = la_ref[pl.ds(r0, rows), ks]
            k = 1
            while k < seg:
                bc = bc + jnp.where(t_in >= k, pltpu.roll(bc, k, 0), 0.0)
                k *= 2
            bl = jnp.zeros_like(bc)
            for e in range(per):
                last = e * seg + seg - 1
                bl = jnp.where((rid >> shift) == e, bc[last:last + 1], bl)
            qt = ((q_ref[pl.ds(r0, rows), ks] * scale) * jnp.exp(bc)).astype(BF16)
            kc = k_ref[pl.ds(r0, rows), ks]
            kt = (kc * jnp.exp(-bc)).astype(BF16)
            ke = kc * jnp.exp(bl - bc)
            vb = v_ref[pl.ds(r0, rows), vs].astype(BF16)
            per_seq = []
            for e in range(per):
                last = e * seg + seg - 1
                dec = jnp.exp(bc[last:last + 1])
                d_hi, d_mid, d_lo = _split3_bf16(dec)
                d3 = jnp.where(rid == 0, d_hi.astype(F32),
                               jnp.where(rid == 1, d_mid.astype(F32),
                                         jnp.where(rid == 2, d_lo.astype(F32), 0.0))).astype(BF16)
                ke_e = jnp.where((rid >> shift) == e, ke, 0.0).astype(BF16)
                per_seq.append((d3, ke_e, s_ref[i * per + e, h]))
            prep.append((vs, qt, kt, vb, per_seq))
        scores = [lax.dot_general(qt, kt, _CONTRACT_LAST, preferred_element_type=F32) for _, qt, kt, _, _ in prep]
        inter, dec_cols, kvs = [], [], []
        for _, qt, _, vb, per_seq in prep:
            for d3, ke_e, st in per_seq:
                inter.append(_dot(qt, st.astype(BF16)))
                dec_cols.append(lax.dot_general(d3, ones_b, _CONTRACT_FIRST, preferred_element_type=F32))
                kvs.append(lax.dot_general(ke_e, vb, _CONTRACT_FIRST, preferred_element_type=F32))
        intra = [_dot(jnp.where(causal, s, 0.0).astype(BF16), vb) for s, (_, _, _, vb, _) in zip(scores, prep)]
        for h, (vs, _, _, _, per_seq) in enumerate(prep):
            o_inter = inter[h * per]
            for e in range(per):
                if e:
                    o_inter = jnp.where((rid >> shift) == e, inter[h * per + e], o_inter)
                so_ref[i * per + e, h] = dec_cols[h * per + e] * per_seq[e][2] + kvs[h * per + e]
            y_ref[pl.ds(r0, rows), vs] = _gla_out(intra[h] + o_inter, gain_ref[...], r_ref[pl.ds(r0, rows), vs])
        return carry

    lax.fori_loop(0, nb // per, pair, 0)


def _gla_step(proj, gain, state, *, seg, nb=8):
    n = proj.shape[0]
    bsz, hh, dk, dv = state.shape
    rows = nb * seg
    c_q = 1024 // (hh * dk)
    c_k = c_q + 1
    c_v = (1024 + 2 * hh * dk) // (hh * dv)
    c_r = c_v + 1
    c_la = (1024 + 2 * hh * dk + 2 * hh * dv) // (hh * dk)
    spec = lambda w, c0: pl.BlockSpec((rows, w), lambda i: (i, c0))
    body = functools.partial(_gla_step_body, nb=nb, seg=seg, dk=dk, dv=dv)
    return pl.pallas_call(
        body,
        grid=(bsz // nb,),
        in_specs=[spec(hh * dk, c_q), spec(hh * dk, c_k), spec(hh * dv, c_v), spec(hh * dk, c_la), spec(hh * dv, c_r),
                  pl.BlockSpec((1, dv), lambda i: (0, 0)),
                  pl.BlockSpec((nb, hh, dk, dv), lambda i: (i, 0, 0, 0))],
        out_specs=(pl.BlockSpec((rows, hh * dv), lambda i: (i, 0)),
                   pl.BlockSpec((nb, hh, dk, dv), lambda i: (i, 0, 0, 0))),
        out_shape=(jax.ShapeDtypeStruct((n, hh * dv), F32), jax.ShapeDtypeStruct(state.shape, F32)),
        compiler_params=_params("arbitrary"),
        name="gla_step",
    )(proj, proj, proj, proj, proj, gain.reshape(1, dv), state)


def _tiled_linear(name, prologue, row_ins, aux_ins, aux_specs, w, *, tm, tn, residual=None):
    n = row_ins[0].shape[0]
    k_dim, n_cols = w.shape
    nr, na = len(row_ins), len(aux_ins)
    has_res = residual is not None

    def body(*refs):
        row_refs = refs[:nr]
        aux_refs = refs[nr:nr + na]
        w_ref = refs[nr + na]
        o_ref, h_ref = refs[-2], refs[-1]

        @pl.when(pl.program_id(1) == 0)
        def _():
            prologue(row_refs, aux_refs, h_ref)

        acc = _dot(h_ref[...], w_ref[...].astype(BF16))
        if has_res:
            acc = acc + refs[nr + na + 1][...]
        o_ref[...] = acc

    in_specs = [pl.BlockSpec((tm, a.shape[1]), lambda i, j: (i, 0)) for a in row_ins]
    in_specs += list(aux_specs)
    in_specs.append(pl.BlockSpec((k_dim, tn), lambda i, j: (0, j)))
    args = list(row_ins) + list(aux_ins) + [w]
    if has_res:
        in_specs.append(pl.BlockSpec((tm, tn), lambda i, j: (i, j)))
        args.append(residual)
    return pl.pallas_call(
        body,
        grid=(n // tm, n_cols // tn),
        in_specs=in_specs,
        out_specs=pl.BlockSpec((tm, tn), lambda i, j: (i, j)),
        out_shape=jax.ShapeDtypeStruct((n, n_cols), F32),
        scratch_shapes=[pltpu.VMEM((tm, k_dim), BF16)],
        compiler_params=_params("arbitrary", "arbitrary"),
        name=name,
    )(*args)


def _whole(a):
    return pl.BlockSpec(a.shape, lambda i, j: (0,) * a.ndim)


def _norm_linear(name, x, gain, w, *, tm, tn=512):
    def prologue(row_refs, aux_refs, h_ref):
        _rms_rows_to_bf16(row_refs[0], aux_refs[0], h_ref, tm)
    g = gain.reshape(1, -1)
    return _tiled_linear(name, prologue, [x], [g], [_whole(g)], w, tm=tm, tn=tn)


def _cast_linear(name, x, w, residual, *, tm, tn=512):
    def prologue(row_refs, aux_refs, h_ref):
        def f(r0):
            h_ref[pl.ds(r0, 128), :] = row_refs[0][pl.ds(r0, 128), :].astype(BF16)
        _for_row_chunks(tm, 128, f)
    return _tiled_linear(name, prologue, [x], [], [], w, tm=tm, tn=tn, residual=residual)


def _out_proj(z_s5, y_gla, gain_s5, w_out, x, *, tm, tn=512):
    ws = z_s5.shape[1]

    def prologue(row_refs, aux_refs, h_ref):
        def f(r0):
            h_ref[pl.ds(r0, 128), 0:ws] = _rms(row_refs[0][pl.ds(r0, 128), :], aux_refs[0][...]).astype(BF16)
            h_ref[pl.ds(r0, 128), ws:] = row_refs[1][pl.ds(r0, 128), :].astype(BF16)
        _for_row_chunks(tm, 128, f)
    g = gain_s5.reshape(1, -1)
    return _tiled_linear("out_proj", prologue, [z_s5, y_gla], [g], [_whole(g)], w_out, tm=tm, tn=tn, residual=x)


def _mix_xattn_seq_body(z_ref, y_ref, xres_ref, gs5_ref, gx_ref, wout_ref, wq_ref, wo_ref, mk_ref, mv_ref,
                        o_ref, x1_ref, h_ref, q_ref, *, tm, tn, nt, hd, chunk):
    j = pl.program_id(1)
    ws = z_ref.shape[1]
    scale = hd ** -0.5

    @pl.when(j == 0)
    def _():
        def f(r0):
            h_ref[pl.ds(r0, 128), 0:ws] = _rms(z_ref[pl.ds(r0, 128), :], gs5_ref[...]).astype(BF16)
            h_ref[pl.ds(r0, 128), ws:] = y_ref[pl.ds(r0, 128), :].astype(BF16)
        _for_row_chunks(tm, 128, f)

    @pl.when(j == nt)
    def _():
        _rms_rows_to_bf16(x1_ref, gx_ref, h_ref, tm)

    @pl.when(j == 2 * nt)
    def _():
        heads = [slice(h * hd, (h + 1) * hd) for h in range(X_HEADS)]

        def f(r0):
            scores = [lax.dot_general(q_ref[pl.ds(r0, chunk), hs], mk_ref[:, hs], _CONTRACT_LAST,
                                      preferred_element_type=F32) * scale for hs in heads]
            probs = []
            for s in scores:
                e = jnp.exp(s - jnp.max(s, axis=-1, keepdims=True))
                probs.append((e / jnp.sum(e, axis=-1, keepdims=True)).astype(BF16))
            for hs, p in zip(heads, probs):
                h_ref[pl.ds(r0, chunk), hs] = _dot(p, mv_ref[:, hs]).astype(BF16)
        _for_row_chunks(tm, chunk, f)

    for jj in range(nt):
        cols = slice(jj * tn, (jj + 1) * tn)

        @pl.when(j == jj)
        def _(cols=cols):
            x1_ref[:, cols] = _dot(h_ref[...], wout_ref[...]) + xres_ref[...]

        @pl.when(j == nt + jj)
        def _(cols=cols):
            q_ref[:, cols] = _dot(h_ref[...], wq_ref[...]).astype(BF16)

        @pl.when(j == 2 * nt + jj)
        def _(cols=cols):
            o_ref[...] = _dot(h_ref[...], wo_ref[...]) + x1_ref[:, cols]


def _mix_xattn_seq(z_s5, y_gla, x, gain_s5, gain_x, w_out, w_xq, w_xo, mk_b, mv_b, *, tm, seq, tn=512, chunk=256):
    n, dm = x.shape
    nt = dm // tn
    tiles_per_seq = seq // tm
    body = functools.partial(_mix_xattn_seq_body, tm=tm, tn=tn, nt=nt, hd=dm // X_HEADS, chunk=chunk)
    gs = gain_s5.reshape(1, -1)
    gx = gain_x.reshape(1, -1)
    phase = lambda p: (lambda i, j: (0, jnp.clip(j - p * nt, 0, nt - 1)))
    mspec = pl.BlockSpec((None,) + mk_b.shape[1:], lambda i, j: (i // tiles_per_seq, 0, 0))
    return pl.pallas_call(
        body,
        grid=(n // tm, 3 * nt),
        in_specs=[
            pl.BlockSpec((tm, z_s5.shape[1]), lambda i, j: (i, 0)),
            pl.BlockSpec((tm, y_gla.shape[1]), lambda i, j: (i, 0)),
            pl.BlockSpec((tm, tn), lambda i, j: (i, jnp.minimum(j, nt - 1))),
            _whole(gs), _whole(gx),
            pl.BlockSpec((dm, tn), phase(0)), pl.BlockSpec((dm, tn), phase(1)), pl.BlockSpec((dm, tn), phase(2)),
            mspec, mspec,
        ],
        out_specs=pl.BlockSpec((tm, tn), lambda i, j: (i, jnp.clip(j - 2 * nt, 0, nt - 1))),
        out_shape=jax.ShapeDtypeStruct((n, dm), F32),
        scratch_shapes=[pltpu.VMEM((tm, dm), F32), pltpu.VMEM((tm, dm), BF16), pltpu.VMEM((tm, dm), BF16)],
        compiler_params=_params("arbitrary", "arbitrary"),
        name="mix_xattn_seq",
    )(z_s5, y_gla, x, gs, gx, w_out, w_xq, w_xo, mk_b, mv_b)


def _cache_rows(c):
    b, m, h, hd = c.shape
    n_chunk = hd // LANES
    return c.reshape(b, m, h, n_chunk, LANES).transpose(0, 1, 3, 2, 4).reshape(b * m * n_chunk * h, LANES)


def _mem_proj_body(x_ref, gain_ref, w_ref, o_ref, ob_ref, h_ref, *, tm, hd):
    j = pl.program_id(1)

    @pl.when(j == 0)
    def _():
        _rms_rows_to_bf16(x_ref, gain_ref, h_ref, tm)

    y = _dot(h_ref[...], w_ref[...].astype(BF16))
    ob_ref[...] = y.astype(BF16)
    n_chunk = hd // LANES
    pitch = n_chunk * X_HEADS
    for ck in range(n_chunk):
        o_ref[pl.ds(ck * X_HEADS + j, tm, stride=pitch), :] = y[:, ck * LANES:(ck + 1) * LANES]


def _mem_proj(name, mem, gain, w, *, tm):
    b, m, dm = mem.shape
    n = b * m
    hd = dm // X_HEADS
    n_chunk = hd // LANES
    pitch = n_chunk * X_HEADS
    body = functools.partial(_mem_proj_body, tm=tm, hd=hd)
    g = gain.reshape(1, dm)
    rows, plain = pl.pallas_call(
        body,
        grid=(n // tm, X_HEADS),
        in_specs=[pl.BlockSpec((tm, dm), lambda i, j: (i, 0)), _whole(g), pl.BlockSpec((dm, hd), lambda i, j: (0, j))],
        out_specs=(pl.BlockSpec((tm * pitch, LANES), lambda i, j: (i, 0)),
                   pl.BlockSpec((tm, hd), lambda i, j: (i, j))),
        out_shape=(jax.ShapeDtypeStruct((n * pitch, LANES), F32), jax.ShapeDtypeStruct((n, dm), BF16)),
        scratch_shapes=[pltpu.VMEM((tm, dm), BF16)],
        compiler_params=_params("arbitrary", "arbitrary"),
        name=name,
    )(mem.reshape(n, dm), g, w)
    full = rows.reshape(b, m, n_chunk, X_HEADS, LANES).transpose(0, 1, 3, 2, 4).reshape(1, b, m, X_HEADS, hd)
    return full, plain.reshape(b, m, dm)


def _xattn_step_body(q_ref, k_ref, v_ref, o_ref, *, nb, seg, hd, n_mem):
    rows = SUBLANES
    per = rows // seg
    shift = seg.bit_length() - 1
    rid = lax.broadcasted_iota(jnp.int32, (rows, 1), 0)
    scale = hd ** -0.5
    n_chunk = hd // LANES
    pitch = n_chunk * X_HEADS

    def head_rows(ref, b, h):
        parts = [ref[pl.ds(b * n_mem * pitch + ck * X_HEADS + h, n_mem, stride=pitch), :] for ck in range(n_chunk)]
        return jnp.concatenate(parts, axis=1).astype(BF16)

    pairs = [(b, h) for b in range(nb) for h in range(X_HEADS)]
    scores = []
    for b, h in pairs:
        r0 = (b // per) * rows
        q = q_ref[r0:r0 + rows, h * hd:(h + 1) * hd].astype(BF16)
        scores.append(lax.dot_general(q, head_rows(k_ref, b, h), _CONTRACT_LAST, preferred_element_type=F32) * scale)
    probs = []
    for s in scores:
        e = jnp.exp(s - jnp.max(s, axis=-1, keepdims=True))
        probs.append((e / jnp.sum(e, axis=-1, keepdims=True)).astype(BF16))
    outs = {}
    for (b, h), p in zip(pairs, probs):
        outs[b, h] = _dot(p, head_rows(v_ref, b, h))
    for g in range(nb // per):
        for h in range(X_HEADS):
            o = outs[g * per, h]
            for e in range(1, per):
                o = jnp.where((rid >> shift) == e, outs[g * per + e, h], o)
            o_ref[g * rows:(g + 1) * rows, h * hd:(h + 1) * hd] = o


def _xattn_step(q, mem_k, mem_v, *, seg, nb=4):
    n, dm = q.shape
    bsz, m, _, hd = mem_k.shape
    body = functools.partial(_xattn_step_body, nb=nb, seg=seg, hd=hd, n_mem=m)
    rows_per_seq = m * dm // LANES
    mspec = pl.BlockSpec((nb * rows_per_seq, LANES), lambda i: (i, 0))
    return pl.pallas_call(
        body,
        grid=(bsz // nb,),
        in_specs=[pl.BlockSpec((nb * seg, dm), lambda i: (i, 0)), mspec, mspec],
        out_specs=pl.BlockSpec((nb * seg, dm), lambda i: (i, 0)),
        out_shape=jax.ShapeDtypeStruct((n, dm), F32),
        compiler_params=_params("arbitrary"),
        name="xattn_step",
    )(q, _cache_rows(mem_k), _cache_rows(mem_v))


def _ffn_finish(conv, gate, wd_ref, o_ref, gfin_ref, j, nj, tm):
    act = (_gelu(conv) * gate).astype(BF16)
    o_ref[...] += _dot(act, wd_ref[...].astype(BF16))

    @pl.when(j == nj - 1)
    def _():
        def f(r0):
            o_ref[pl.ds(r0, 128), :] = _rms(o_ref[pl.ds(r0, 128), :], gfin_ref[...])
        _for_row_chunks(tm, 128, f)


def _ffn_seq_body(x_ref, gain_ref, wa_ref, wg_ref, wd_ref, cw_ref, cb_ref, gfin_ref, o_ref, tail_ref, h_ref, a_ref,
                  *, tm, halo, tiles_per_seq, nj):
    i = pl.program_id(0)
    j = pl.program_id(1)

    @pl.when(j == 0)
    def _():
        @pl.when(i % tiles_per_seq == 0)
        def _():
            h_ref[0:halo, :] = jnp.zeros((halo, h_ref.shape[1]), BF16)

        @pl.when(i % tiles_per_seq != 0)
        def _():
            h_ref[0:halo, :] = h_ref[tm:tm + halo, :]

        _rms_rows_to_bf16(x_ref, gain_ref, h_ref, tm, row0=halo)
        o_ref[...] = x_ref[...]

    a_ref[...] = _dot(h_ref[...], wa_ref[...].astype(BF16))
    gate = _dot(h_ref[halo:halo + tm, :], wg_ref[...].astype(BF16))
    conv = cb_ref[...]
    for tap in range(CONV_W):
        conv = conv + a_ref[pl.ds(halo - (CONV_W - 1) + tap, tm), :] * cw_ref[tap:tap + 1, :]
    tail_ref[...] = a_ref[pl.ds(halo + tm - SUBLANES, SUBLANES), :]
    _ffn_finish(conv, gate, wd_ref, o_ref, gfin_ref, j, nj, tm)


def _ffn_seq(x, gain, w_up, conv_w, conv_b, w_down, gain_final, *, tm, seq, tf=512, halo=16):
    n, dm = x.shape
    dff = w_down.shape[0]
    nj = dff // tf
    body = functools.partial(_ffn_seq_body, tm=tm, halo=halo, tiles_per_seq=seq // tm, nj=nj)
    return pl.pallas_call(
        body,
        grid=(n // tm, nj),
        in_specs=[
            pl.BlockSpec((tm, dm), lambda i, j: (i, 0)),
            pl.BlockSpec((1, dm), lambda i, j: (0, 0)),
            pl.BlockSpec((dm, tf), lambda i, j: (0, j)),
            pl.BlockSpec((dm, tf), lambda i, j: (0, nj + j)),
            pl.BlockSpec((tf, dm), lambda i, j: (j, 0)),
            pl.BlockSpec((CONV_W, tf), lambda i, j: (0, j)),
            pl.BlockSpec((1, tf), lambda i, j: (0, j)),
            pl.BlockSpec((1, dm), lambda i, j: (0, 0)),
        ],
        out_specs=(pl.BlockSpec((tm, dm), lambda i, j: (i, 0)),
                   pl.BlockSpec((None, SUBLANES, tf), lambda i, j: (i, 0, j))),
        out_shape=(jax.ShapeDtypeStruct((n, dm), F32),
                   jax.ShapeDtypeStruct((n // tm, SUBLANES, dff), F32)),
        scratch_shapes=[pltpu.VMEM((halo + tm, dm), BF16), pltpu.VMEM((halo + tm, tf), F32)],
        compiler_params=_params("arbitrary", "arbitrary"),
        name="ffn_seq",
    )(x, gain.reshape(1, dm), w_up, w_up, w_down, conv_w, conv_b.reshape(1, dff), gain_final.reshape(1, dm))


def _dot_exact_rows(sel_b, x):
    hi, mid, lo = _split3_bf16(x)
    return _dot(sel_b, hi) + _dot(sel_b, mid) + _dot(sel_b, lo)


def _ffn_step_body(x_ref, gain_ref, wa_ref, wg_ref, wd_ref, cw_ref, cb_ref, gfin_ref, b0_ref, b1_ref,
                   o_ref, a_new_ref, h_ref, *, tm, seg, nj):
    j = pl.program_id(1)
    nseq = tm // seg
    shift = seg.bit_length() - 1

    @pl.when(j == 0)
    def _():
        _rms_rows_to_bf16(x_ref, gain_ref, h_ref, tm)
        o_ref[...] = x_ref[...]

    a = _dot(h_ref[...], wa_ref[...].astype(BF16))
    gate = _dot(h_ref[...], wg_ref[...].astype(BF16))
    t_in = lax.broadcasted_iota(jnp.int32, (tm, 1), 0) & (seg - 1)
    prev1 = jnp.where(t_in >= 1, pltpu.roll(a, 1, 0), 0.0)
    prev2 = jnp.where(t_in >= 2, pltpu.roll(a, 2, 0), 0.0)
    conv = cb_ref[...] + prev2 * cw_ref[0:1, :]
    conv = conv + prev1 * cw_ref[1:2, :]
    conv = conv + a * cw_ref[2:3, :]
    b0 = b0_ref[...]
    b1 = b1_ref[...]
    carried = jnp.concatenate([b0 * cw_ref[0:1, :] + b1 * cw_ref[1:2, :], b1 * cw_ref[0:1, :]], axis=0)
    r = lax.broadcasted_iota(jnp.int32, (tm, 2 * nseq), 0)
    c = lax.broadcasted_iota(jnp.int32, (tm, 2 * nseq), 1)
    spread = ((r >> shift) == (c & (nseq - 1))) & ((r & (seg - 1)) == (c >> (nseq.bit_length() - 1)))
    conv = conv + _dot_exact_rows(jnp.where(spread, 1.0, 0.0).astype(BF16), carried)
    r = lax.broadcasted_iota(jnp.int32, (2 * nseq, tm), 0)
    c = lax.broadcasted_iota(jnp.int32, (2 * nseq, tm), 1)
    pick = ((c >> shift) == (r & (nseq - 1))) & ((c & (seg - 1)) == seg - 2 + (r >> (nseq.bit_length() - 1)))
    picked = _dot_exact_rows(jnp.where(pick, 1.0, 0.0).astype(BF16), a)
    a_new_ref[0] = picked[0:nseq]
    a_new_ref[1] = picked[nseq:2 * nseq]
    _ffn_finish(conv, gate, wd_ref, o_ref, gfin_ref, j, nj, tm)


def _ffn_step(x, gain, w_up, conv_w, conv_b, w_down, gain_final, b0, b1, *, seg, tf=512):
    n, dm = x.shape
    dff = w_down.shape[0]
    nj = dff // tf
    tm = n
    nseq = n // seg
    body = functools.partial(_ffn_step_body, tm=tm, seg=seg, nj=nj)
    return pl.pallas_call(
        body,
        grid=(1, nj),
        in_specs=[
            pl.BlockSpec((tm, dm), lambda i, j: (i, 0)),
            pl.BlockSpec((1, dm), lambda i, j: (0, 0)),
            pl.BlockSpec((dm, tf), lambda i, j: (0, j)),
            pl.BlockSpec((dm, tf), lambda i, j: (0, nj + j)),
            pl.BlockSpec((tf, dm), lambda i, j: (j, 0)),
            pl.BlockSpec((CONV_W, tf), lambda i, j: (0, j)),
            pl.BlockSpec((1, tf), lambda i, j: (0, j)),
            pl.BlockSpec((1, dm), lambda i, j: (0, 0)),
            pl.BlockSpec((nseq, tf), lambda i, j: (0, j)),
            pl.BlockSpec((nseq, tf), lambda i, j: (0, j)),
        ],
        out_specs=(pl.BlockSpec((tm, dm), lambda i, j: (i, 0)),
                   pl.BlockSpec((CONV_W - 1, nseq, tf), lambda i, j: (0, 0, j))),
        out_shape=(jax.ShapeDtypeStruct((n, dm), F32), jax.ShapeDtypeStruct((CONV_W - 1, nseq, dff), F32)),
        scratch_shapes=[pltpu.VMEM((tm, dm), BF16)],
        compiler_params=_params("arbitrary", "arbitrary"),
        name="ffn_step",
    )(x, gain.reshape(1, dm), w_up, w_up, w_down, conv_w, conv_b.reshape(1, dff), gain_final.reshape(1, dm),
      b0, b1)


def kernel(x_prompt, x_sample, mem_prompt, cache_mem_k, cache_mem_v, state_s5_re, state_s5_im, state_gla, state_conv, norm_mix, w_in, s5_lambda_re, s5_lambda_im, s5_log_dt, s5_b_re, s5_b_im, s5_c_re, s5_c_im, s5_d, s5_w_glu, s5_b_glu, norm_s5_out, gla_w_g2, gla_b_g, norm_gla_out, w_out, norm_xattn, norm_mem, w_xq, w_xk, w_xv, w_xo, norm_ffn, w_up, conv_w, conv_b, w_down, norm_final):
    depth = w_in.shape[0]
    assert depth == 1
    bp, seq, dm = x_prompt.shape
    bs, seg, _ = x_sample.shape
    n_mem = mem_prompt.shape[1]
    dff = w_down.shape[1]
    l = 0
    s5_args = (s5_lambda_re[l], s5_lambda_im[l], s5_log_dt[l], s5_b_re[l], s5_b_im[l], s5_c_re[l], s5_c_im[l],
               s5_d[l], s5_w_glu[l], s5_b_glu[l])
    tm_p = min(1024, seq)
    n_s = bs * seg
    w_in, w_out, w_xq, w_xo, w_up, w_down = (w.astype(BF16) for w in (w_in, w_out, w_xq, w_xo, w_up, w_down))

    mk, mk_b = _mem_proj("mem_k", mem_prompt, norm_mem[l], w_xk[l], tm=min(1024, bp * n_mem))
    mv, mv_b = _mem_proj("mem_v", mem_prompt, norm_mem[l], w_xv[l], tm=min(1024, bp * n_mem))

    xp = x_prompt.reshape(bp * seq, dm)
    proj = _in_proj(xp, norm_mix[l], w_in[l], gla_w_g2[l], gla_b_g[l], tm=tm_p)
    proj3 = proj.reshape(bp, seq, -1)
    s5_w, s5_powers = _s5_weights(*s5_args)
    steps_p = SUBLANES // bp
    z_s5, hend_r, hend_i = _s5_mixer(proj3, s5_w, _s5_scan_table(s5_powers, steps_p, bp),
                                     tt=min(512 // bp, seq), steps=steps_p)
    y_gla, st_gla = _gla_seq(proj3, norm_gla_out[l], tt=min(512, seq))
    x2 = _mix_xattn_seq(z_s5.reshape(bp * seq, -1), y_gla.reshape(bp * seq, -1), xp, norm_s5_out[l], norm_xattn[l],
                        w_out[l], w_xq[l], w_xo[l], mk_b, mv_b, tm=tm_p, seq=seq)
    yp, tails = _ffn_seq(x2, norm_ffn[l], w_up[l], conv_w[l], conv_b[l], w_down[l], norm_final, tm=tm_p, seq=seq)

    g_total = s5_lambda_re.shape[1]
    n_state = s5_lambda_re.shape[2]
    p_re = jnp.swapaxes(hend_r[:, SUBLANES - bp:], 0, 1).reshape(1, bp, g_total, n_state)
    p_im = jnp.swapaxes(hend_i[:, SUBLANES - bp:], 0, 1).reshape(1, bp, g_total, n_state)
    p_gla = jnp.swapaxes(st_gla, -1, -2)[None]
    tiles_per_seq = seq // tm_p
    p_conv = tails.reshape(bp, tiles_per_seq, SUBLANES, dff)[:, -1, SUBLANES - (CONV_W - 1):, :][None]

    xs = x_sample.reshape(n_s, dm)
    proj_s = _in_proj(xs, norm_mix[l], w_in[l], gla_w_g2[l], gla_b_g[l], tm=n_s)
    h0r = state_s5_re[l].reshape(1, bs, -1)
    h0i = state_s5_im[l].reshape(1, bs, -1)
    z_s5_s, hs_r, hs_i = _s5_mixer(proj_s[None], s5_w, _s5_scan_table(s5_powers, seg, 1), tt=n_s, steps=seg,
                                   h0=(h0r, h0i))
    y_gla_s, s_gla = _gla_step(proj_s, norm_gla_out[l], state_gla[l], seg=seg)
    x1s = _out_proj(z_s5_s[0], y_gla_s, norm_s5_out[l], w_out[l], xs, tm=n_s)
    qs = _norm_linear("xattn_q_s", x1s, norm_xattn[l], w_xq[l], tm=n_s)
    att = _xattn_step(qs, cache_mem_k[l], cache_mem_v[l], seg=seg)
    x2s = _cast_linear("xattn_o_s", att, w_xo[l], x1s, tm=n_s)
    buf = state_conv[l]
    ys, a_new = _ffn_step(x2s, norm_ffn[l], w_up[l], conv_w[l], conv_b[l], w_down[l], norm_final,
                          buf[:, 0], buf[:, 1], seg=seg)

    s_re = hs_r.reshape(1, bs, g_total, n_state)
    s_im = hs_i.reshape(1, bs, g_total, n_state)
    s_conv = jnp.swapaxes(a_new, 0, 1)[None]

    return (yp.reshape(bp, seq, dm), ys.reshape(bs, seg, dm), mk, mv,
            p_re, p_im, p_gla, p_conv, s_re, s_im, s_gla[None], s_conv)
```

```python
import functools

import jax
import jax.numpy as jnp
from jax import lax
from jax.experimental import pallas as pl
from jax.experimental.pallas import tpu as pltpu

F32 = jnp.float32
BF16 = jnp.bfloat16

EPS = 1e-6
S5_GROUP = 16
S5_BLOCK_GROUPS = 16
GLA_HEADS = 4
GLA_RANK = 16
GLA_TAU = 16.0
GLA_CHUNK = 64
X_HEADS = 4
CONV_W = 3

SUBLANES = 8
LANES = 128
VMEM_LIMIT_BYTES = 56 * 1024 * 1024

_CONTRACT_LAST = (((1,), (1,)), ((), ()))
_CONTRACT_FIRST = (((0,), (0,)), ((), ()))


def _params(*semantics):
    return pltpu.CompilerParams(dimension_semantics=semantics, vmem_limit_bytes=VMEM_LIMIT_BYTES)


def _dot(a, b):
    return jnp.dot(a, b, preferred_element_type=F32)


def _for_row_chunks(n_rows, chunk, fn):
    def body(i, carry):
        fn(pl.multiple_of(i * chunk, chunk))
        return carry
    lax.fori_loop(0, n_rows // chunk, body, 0)


def _rms(x, gain):
    r = lax.rsqrt(jnp.mean(x * x, axis=-1, keepdims=True) + EPS)
    return (x * r) * gain


def _gelu(x):
    return 0.5 * x * (1.0 + lax.erf(x * (2.0 ** -0.5)))


def _rms_rows_to_bf16(x_ref, gain_ref, h_ref, n_rows, row0=0, chunk=128):
    def f(r0):
        h_ref[pl.ds(row0 + r0, chunk), :] = _rms(x_ref[pl.ds(r0, chunk), :], gain_ref[...]).astype(BF16)
    _for_row_chunks(n_rows, chunk, f)


def _s5_prep_body(lr_ref, li_ref, ldt_ref, br_ref, bi_ref, apr_ref, api_ref, bbr_ref, bbi_ref):
    lr = lr_ref[...]
    li = li_ref[...]
    dt = jnp.exp(ldt_ref[...])
    mag = jnp.exp(lr * dt)
    ar = mag * jnp.cos(li * dt)
    ai = mag * jnp.sin(li * dt)
    den = lr * lr + li * li
    pr = ar - 1.0
    qr = (pr * lr + ai * li) / den
    qi = (ai * lr - pr * li) / den
    for p in range(S5_GROUP):
        br = br_ref[p]
        bi = bi_ref[p]
        bbr_ref[p] = qr * br - qi * bi
        bbi_ref[p] = qr * bi + qi * br
    cr, ci = ar, ai
    apr_ref[0] = cr
    api_ref[0] = ci
    for k in range(1, SUBLANES):
        cr, ci = cr * ar - ci * ai, cr * ai + ci * ar
        apr_ref[k] = cr
        api_ref[k] = ci


def _s5_prep(lam_re, lam_im, log_dt, b_re, b_im):
    g, n = lam_re.shape
    out = jax.ShapeDtypeStruct((SUBLANES, g, n), F32)
    outb = jax.ShapeDtypeStruct((S5_GROUP, g, n), F32)
    return pl.pallas_call(
        _s5_prep_body, out_shape=(out, out, outb, outb), name="s5_prep",
    )(lam_re, lam_im, log_dt.reshape(g, 1), jnp.transpose(b_re, (2, 0, 1)), jnp.transpose(b_im, (2, 0, 1)))


def _s5_weights(lam_re, lam_im, log_dt, b_re, b_im, c_re, c_im, d, w_glu, b_glu):
    g, n = lam_re.shape
    p = S5_GROUP
    gb = S5_BLOCK_GROUPS
    nb = g // gb
    apr, api, bbr, bbi = _s5_prep(lam_re, lam_im, log_dt, b_re, b_im)

    def same_group(rows, row_w, cols, col_w):
        return (jnp.arange(rows)[:, None] // row_w) == (jnp.arange(cols)[None, :] // col_w)

    def tile_cols(x, reps):
        w = x.shape[-1]
        sel = (jnp.arange(reps * w)[None, :] % w == jnp.arange(w)[:, None]).astype(BF16)
        return jnp.einsum('brw,wj->brj', x, sel, preferred_element_type=F32)

    def tile_rows(x, reps):
        h = x.shape[1]
        sel = (jnp.arange(reps * h)[:, None] % h == jnp.arange(h)[None, :]).astype(BF16)
        return jnp.einsum('rh,bhc->brc', sel, x, preferred_element_type=F32)

    def bu_block(bb):
        per_block = jnp.transpose(bb.reshape(p, nb, gb * n), (1, 0, 2)).astype(BF16)
        return jnp.where(same_group(gb * p, p, gb * n, n), tile_rows(per_block, gb), 0.0)

    def c_block(c):
        per_block = jnp.transpose(c.reshape(nb, gb, p, n), (0, 1, 3, 2)).reshape(nb, gb * n, p).astype(BF16)
        return jnp.where(same_group(gb * n, n, gb * p, p), tile_cols(per_block, gb), 0.0)

    w_bu = jnp.concatenate([bu_block(bbr), bu_block(bbi)], axis=-1).astype(BF16)
    w_c = jnp.concatenate([c_block(c_re), -c_block(c_im)], axis=1).astype(BF16)
    w_g = jnp.where(same_group(gb * p, p, gb * p, p),
                    tile_cols(w_glu.reshape(nb, gb * p, p).astype(BF16), gb), 0.0).astype(BF16)
    d_b = d.reshape(nb, 1, gb * p)
    bg_b = b_glu.reshape(nb, 1, gb * p)
    pw_r = apr.reshape(SUBLANES, nb, gb * n)
    pw_i = api.reshape(SUBLANES, nb, gb * n)
    return (w_bu, w_c, w_g, d_b, bg_b), (pw_r, pw_i)


def _s5_scan_table(powers, steps, rows_per_step):
    pw_r, pw_i = powers
    t = (jnp.arange(SUBLANES) // rows_per_step) % steps

    def masked(k):
        m = (t >= k)[:, None, None]
        return jnp.where(m, pw_r[k - 1][None], 0.0), jnp.where(m, pw_i[k - 1][None], 0.0)

    tabs = []
    k = 1
    while k < steps:
        tabs.extend(masked(k))
        k *= 2
    tabs.append(pw_r[t])
    tabs.append(pw_i[t])
    whole = jnp.full((SUBLANES,), steps - 1)
    tabs.append(pw_r[whole])
    tabs.append(pw_i[whole])
    return jnp.transpose(jnp.stack(tabs), (2, 0, 1, 3))


def _in_proj_body(x_ref, gain_ref, w_ref, wr_ref, wg_ref, wg2_ref, bg_ref, o_ref, h_ref, *, tm, n_main, n_r):
    j = pl.program_id(1)

    @pl.when(j == 0)
    def _():
        _rms_rows_to_bf16(x_ref, gain_ref, h_ref, tm)

    @pl.when(j < n_main)
    def _():
        o_ref[...] = _dot(h_ref[...], w_ref[...].astype(BF16))

    @pl.when((j >= n_main) & (j < n_main + n_r))
    def _():
        o_ref[...] = _dot(h_ref[...], wr_ref[...].astype(BF16))

    @pl.when(j == n_main + n_r)
    def _():
        g_low = _dot(h_ref[...], wg_ref[...].astype(BF16))
        pre = _dot(g_low.astype(BF16), wg2_ref[...].astype(BF16)) + bg_ref[...]
        log_sig = jnp.minimum(pre, 0.0) - jnp.log1p(jnp.exp(-jnp.abs(pre)))
        o_ref[...] = log_sig * (1.0 / GLA_TAU)


def _in_proj(x, gain, w_in, w_g2, b_g, *, tm, tn=512):
    n, dm = x.shape
    mix = dm
    n_u_q_k_v = mix // 2 + 2 * (mix // 4) + mix // 2
    col_g = n_u_q_k_v
    col_r = n_u_q_k_v + GLA_RANK
    w_r = w_in[:, col_r:]
    w_g = w_in[:, col_g:col_r]
    n_main = n_u_q_k_v // tn
    n_r = w_r.shape[1] // tn
    n_cols = n_u_q_k_v + w_r.shape[1] + w_g2.shape[1]
    body = functools.partial(_in_proj_body, tm=tm, n_main=n_main, n_r=n_r)
    return pl.pallas_call(
        body,
        grid=(n // tm, n_main + n_r + 1),
        in_specs=[
            pl.BlockSpec((tm, dm), lambda i, j: (i, 0)),
            pl.BlockSpec((1, dm), lambda i, j: (0, 0)),
            pl.BlockSpec((dm, tn), lambda i, j: (0, jnp.minimum(j, n_main - 1))),
            pl.BlockSpec((dm, tn), lambda i, j: (0, jnp.clip(j - n_main, 0, n_r - 1))),
            pl.BlockSpec((dm, GLA_RANK), lambda i, j: (0, 0)),
            pl.BlockSpec(w_g2.shape, lambda i, j: (0, 0)),
            pl.BlockSpec((1, w_g2.shape[1]), lambda i, j: (0, 0)),
        ],
        out_specs=pl.BlockSpec((tm, tn), lambda i, j: (i, j)),
        out_shape=jax.ShapeDtypeStruct((n, n_cols), F32),
        scratch_shapes=[pltpu.VMEM((tm, dm), BF16)],
        compiler_params=_params("arbitrary", "arbitrary"),
        name="in_proj",
    )(x, gain.reshape(1, dm), w_in, w_r, w_g, w_g2, b_g.reshape(1, -1))


def _s5_body(*refs, tt, steps, nseq, ns, chunk):
    carried = nseq > 1
    if carried:
        (u_ref, wbu_ref, wc_ref, wg_ref, d_ref, bg_ref, tab_ref, z_ref, hr_ref, hi_ref,
         bu_ref, cr_ref, ci_ref, us_ref, zs_ref) = refs
    else:
        u_ref, wbu_ref, wc_ref, wg_ref, d_ref, bg_ref, tab_ref, h0r_ref, h0i_ref, z_ref, hr_ref, hi_ref, bu_ref = refs
    seg = steps
    n_lvl = steps.bit_length() - 1
    lane_group = 2 * LANES
    rid = lax.broadcasted_iota(jnp.int32, (SUBLANES, 1), 0)
    cw = u_ref.shape[-1]
    n_lt = cw // LANES

    if carried:
        @pl.when(pl.program_id(1) == 0)
        def _():
            cr_ref[...] = jnp.zeros_like(cr_ref)
            ci_ref[...] = jnp.zeros_like(ci_ref)

        per = chunk // nseq

    def chunk_u(c):
        if not carried:
            return u_ref[c * chunk:(c + 1) * chunk, :]
        for b in range(nseq):
            for lt in range(n_lt):
                us_ref[c * n_lt + lt, pl.ds(b, per, stride=nseq), :] = (
                    u_ref[b, c * per:(c + 1) * per, lt * LANES:(lt + 1) * LANES])
        return jnp.concatenate([us_ref[c * n_lt + lt] for lt in range(n_lt)], axis=1)

    def project_in(c):
        rows = slice(c * chunk, (c + 1) * chunk)
        bu = _dot(chunk_u(c).astype(BF16), wbu_ref[...])
        if not carried:
            seqs = slice(c * chunk // seg, (c + 1) * chunk // seg)
            a_r = tab_ref[2 * n_lvl, 0:1, :]
            a_i = tab_ref[2 * n_lvl + 1, 0:1, :]
            h0r = h0r_ref[seqs, :]
            h0i = h0i_ref[seqs, :]
            first = (lax.broadcasted_iota(jnp.int32, (chunk, chunk // seg), 0)
                     == seg * lax.broadcasted_iota(jnp.int32, (chunk, chunk // seg), 1))
            first_b = jnp.where(first, 1.0, 0.0).astype(BF16)
            bu = bu + jnp.concatenate([_dot_exact_rows(first_b, a_r * h0r - a_i * h0i),
                                       _dot_exact_rows(first_b, a_r * h0i + a_i * h0r)], axis=1)
        bu_ref[rows, :] = bu

    def scan(c, carry):
        carry = list(carry)
        for blk in range(chunk // SUBLANES):
            rows = slice(c * chunk + blk * SUBLANES, c * chunk + (blk + 1) * SUBLANES)
            for gi in range(ns // lane_group):
                cols = slice(gi * lane_group, (gi + 1) * lane_group)
                cols_im = slice(ns + gi * lane_group, ns + (gi + 1) * lane_group)
                xr = bu_ref[rows, cols]
                xi = bu_ref[rows, cols_im]
                for lvl in range(n_lvl):
                    k = 1 << lvl
                    ar = tab_ref[2 * lvl, :, cols]
                    ai = tab_ref[2 * lvl + 1, :, cols]
                    sr = pltpu.roll(xr, k * nseq, 0)
                    si = pltpu.roll(xi, k * nseq, 0)
                    xr, xi = xr + (ar * sr - ai * si), xi + (ar * si + ai * sr)
                if carried:
                    pr = tab_ref[2 * n_lvl, :, cols]
                    pi = tab_ref[2 * n_lvl + 1, :, cols]
                    c_r, c_i = carry[gi]
                    hr = xr + (pr * c_r - pi * c_i)
                    hi = xi + (pr * c_i + pi * c_r)
                    lr, li = xr, xi
                    half = SUBLANES // 2
                    while half >= nseq:
                        keep = (rid & (2 * half - 1)) >= half
                        lr = jnp.where(keep, lr, pltpu.roll(lr, SUBLANES - half, 0))
                        li = jnp.where(keep, li, pltpu.roll(li, SUBLANES - half, 0))
                        half //= 2
                    er = tab_ref[2 * n_lvl + 2, :, cols]
                    ei = tab_ref[2 * n_lvl + 3, :, cols]
                    carry[gi] = (lr + (er * c_r - ei * c_i), li + (er * c_i + ei * c_r))
                else:
                    hr, hi = xr, xi
                    for e in range(SUBLANES // seg):
                        src = e * seg + seg - 1
                        dst = (c * chunk + blk * SUBLANES) // seg + e
                        hr_ref[dst:dst + 1, cols] = hr[src:src + 1]
                        hi_ref[dst:dst + 1, cols] = hi[src:src + 1]
                bu_ref[rows, cols] = hr
                bu_ref[rows, cols_im] = hi
        return carry

    def project_out(c):
        rows = slice(c * chunk, (c + 1) * chunk)
        if carried:
            u = jnp.concatenate([us_ref[c * n_lt + lt] for lt in range(n_lt)], axis=1)
        else:
            u = u_ref[rows, :]
        y = _dot(bu_ref[rows, :].astype(BF16), wc_ref[...]) + d_ref[...] * u
        z = _gelu(y)
        z = z * jax.nn.sigmoid(_dot(z.astype(BF16), wg_ref[...]) + bg_ref[...])
        if carried:
            for lt in range(n_lt):
                zs_ref[c * n_lt + lt] = z[:, lt * LANES:(lt + 1) * LANES]
            for b in range(nseq):
                for lt in range(n_lt):
                    z_ref[b, c * per:(c + 1) * per, lt * LANES:(lt + 1) * LANES] = (
                        zs_ref[c * n_lt + lt, pl.ds(b, per, stride=nseq), :])
        else:
            z_ref[rows, :] = z

    n_chunks = tt // chunk
    n_groups = ns // lane_group
    group_cols = [slice(gi * lane_group, (gi + 1) * lane_group) for gi in range(n_groups)]
    carry = [(cr_ref[:, cols], ci_ref[:, cols]) for cols in group_cols] if carried else [None] * n_groups
    project_in(0)
    for c in range(n_chunks):
        if c + 1 < n_chunks:
            project_in(c + 1)
        carry = scan(c, carry)
        project_out(c)
    if carried:
        for cols, (c_r, c_i) in zip(group_cols, carry):
            cr_ref[:, cols] = c_r
            ci_ref[:, cols] = c_i
            hr_ref[:, cols] = c_r
            hi_ref[:, cols] = c_i


def _s5_mixer(proj3, weights, tab, *, tt, steps, h0=None, chunk=128):
    w_bu, w_c, w_g, d_b, bg_b = weights
    bsz, seq, _ = proj3.shape
    nb, cw, ns2 = w_bu.shape
    ns = ns2 // 2
    n_tab = tab.shape[1]
    carried = h0 is None
    nseq = bsz if carried else 1
    assert nseq * steps == SUBLANES or not carried
    rows = nseq * tt
    body = functools.partial(_s5_body, tt=rows, steps=steps, nseq=nseq, ns=ns, chunk=min(chunk, rows))
    wspec = lambda shape: pl.BlockSpec((None,) + shape, lambda g, t: (g,) + (0,) * len(shape))
    lead = bsz if carried else None
    uz_spec = pl.BlockSpec((lead, tt, cw), lambda g, t: (0, t, g))
    in_specs = [
        uz_spec,
        wspec((cw, ns2)), wspec((ns2, cw)), wspec((cw, cw)), wspec((1, cw)), wspec((1, cw)),
        wspec((n_tab, SUBLANES, ns)),
    ]
    args = [proj3, w_bu, w_c, w_g, d_b, bg_b, tab]
    z_shape = jax.ShapeDtypeStruct((bsz, seq, nb * cw), F32)
    scratch = [pltpu.VMEM((rows, ns2), F32)]
    if carried:
        st_shape = jax.ShapeDtypeStruct((nb, SUBLANES, ns), F32)
        st_spec = pl.BlockSpec((None, SUBLANES, ns), lambda g, t: (g, 0, 0))
        slabs = (rows // min(chunk, rows)) * (cw // LANES)
        scratch += [pltpu.VMEM((SUBLANES, ns), F32), pltpu.VMEM((SUBLANES, ns), F32),
                    pltpu.VMEM((slabs, min(chunk, rows), LANES), F32), pltpu.VMEM((slabs, min(chunk, rows), LANES), F32)]
    else:
        assert bsz == 1
        h0r, h0i = h0
        st_spec = pl.BlockSpec((None, tt // steps, ns), lambda g, t: (0, t, g))
        in_specs += [st_spec] * 2
        args += [h0r, h0i]
        st_shape = jax.ShapeDtypeStruct((bsz, seq // steps, nb * ns), F32)
    return pl.pallas_call(
        body,
        grid=(nb, seq // tt),
        in_specs=in_specs,
        out_specs=(uz_spec, st_spec, st_spec),
        out_shape=(z_shape, st_shape, st_shape),
        scratch_shapes=scratch,
        compiler_params=_params("arbitrary", "arbitrary"),
        name="s5_carried" if carried else "s5_segments",
    )(*args)


def _split3_bf16(x):
    hi = x.astype(BF16)
    r1 = x - hi.astype(F32)
    mid = r1.astype(BF16)
    lo = (r1 - mid.astype(F32)).astype(BF16)
    return hi, mid, lo


def _gla_out(o, gain, r):
    return _rms(o, gain) * (r * jax.nn.sigmoid(r))


def _gla_seq_body(q_ref, k_ref, v_ref, la_ref, r_ref, gain_ref, y_ref, st_out_ref, st_ref, *, tt, dk, dv):
    c = GLA_CHUNK
    hh = GLA_HEADS

    @pl.when(pl.program_id(1) == 0)
    def _():
        st_ref[...] = jnp.zeros_like(st_ref)

    row = lax.broadcasted_iota(jnp.int32, (tt, tt), 0)
    col = lax.broadcasted_iota(jnp.int32, (tt, tt), 1)
    shift = c.bit_length() - 1
    tri = (col <= row) & ((row >> shift) == (col >> shift))
    tri_b = jnp.where(tri, 1.0, 0.0).astype(BF16)
    hi, mid, lo = _split3_bf16(la_ref[...])
    bcum = _dot(tri_b, hi) + _dot(tri_b, mid) + _dot(tri_b, lo)
    causal = lax.broadcasted_iota(jnp.int32, (c, c), 1) <= lax.broadcasted_iota(jnp.int32, (c, c), 0)
    scale = dk ** -0.5
    chunks = [slice(ci * c, (ci + 1) * c) for ci in range(tt // c)]
    work = []
    for h in range(hh):
        ks = slice(h * dk, (h + 1) * dk)
        vs = slice(h * dv, (h + 1) * dv)
        qts, decays, o_intra, kvs = [], [], [], []
        for sl in chunks:
            bc = bcum[sl, ks]
            bl = bc[c - 1:c]
            qt = ((q_ref[sl, ks] * scale) * jnp.exp(bc)).astype(BF16)
            kc = k_ref[sl, ks]
            kt = (kc * jnp.exp(-bc)).astype(BF16)
            ke = (kc * jnp.exp(bl - bc)).astype(BF16)
            vb = v_ref[sl, vs].astype(BF16)
            s = lax.dot_general(qt, kt, _CONTRACT_LAST, preferred_element_type=F32)
            s = jnp.where(causal, s, 0.0)
            qts.append(qt)
            decays.append(jnp.exp(bl))
            o_intra.append(_dot(s.astype(BF16), vb))
            kvs.append(lax.dot_general(vb, ke, _CONTRACT_FIRST, preferred_element_type=F32))
        work.append((vs, qts, decays, o_intra, kvs))
    for h, (vs, qts, decays, o_intra, kvs) in enumerate(work):
        st = st_ref[h]
        for sl, qt, dec, oi, kv in zip(chunks, qts, decays, o_intra, kvs):
            o = oi + lax.dot_general(qt, st.astype(BF16), _CONTRACT_LAST, preferred_element_type=F32)
            st = st * dec + kv
            y_ref[sl, vs] = _gla_out(o, gain_ref[...], r_ref[sl, vs]).astype(y_ref.dtype)
        st_ref[h] = st
        st_out_ref[h] = st


def _gla_seq(proj3, gain, *, tt):
    bsz, seq, _ = proj3.shape
    hh = GLA_HEADS
    dv = gain.shape[0]
    dk = dv // 2
    wk, wv = hh * dk, hh * dv
    col_q = 1024 // wk
    col_k = col_q + 1
    col_v = (1024 + 2 * wk) // wv
    col_r = col_v + 1
    col_la = (1024 + 2 * wk + 2 * wv) // wk
    spec = lambda w, c0: pl.BlockSpec((None, tt, w), lambda b, t: (b, t, c0))
    body = functools.partial(_gla_seq_body, tt=tt, dk=dk, dv=dv)
    return pl.pallas_call(
        body,
        grid=(bsz, seq // tt),
        in_specs=[spec(wk, col_q), spec(wk, col_k), spec(wv, col_v), spec(wk, col_la), spec(wv, col_r),
                  pl.BlockSpec((1, dv), lambda b, t: (0, 0))],
        out_specs=(pl.BlockSpec((None, tt, wv), lambda b, t: (b, t, 0)),
                   pl.BlockSpec((None, hh, dv, dk), lambda b, t: (b, 0, 0, 0))),
        out_shape=(jax.ShapeDtypeStruct((bsz, seq, wv), BF16),
                   jax.ShapeDtypeStruct((bsz, hh, dv, dk), F32)),
        scratch_shapes=[pltpu.VMEM((hh, dv, dk), F32)],
        compiler_params=_params("arbitrary", "arbitrary"),
        name="gla_seq",
    )(proj3, proj3, proj3, proj3, proj3, gain.reshape(1, dv))


def _gla_step_body(q_ref, k_ref, v_ref, la_ref, r_ref, gain_ref, s_ref, y_ref, so_ref, *, nb, seg, dk, dv):
    hh = GLA_HEADS
    rows = SUBLANES
    per = rows // seg
    row = lax.broadcasted_iota(jnp.int32, (rows, rows), 0)
    col = lax.broadcasted_iota(jnp.int32, (rows, rows), 1)
    shift = seg.bit_length() - 1
    causal = (col <= row) & ((row >> shift) == (col >> shift))
    rid = lax.broadcasted_iota(jnp.int32, (rows, 1), 0)
    t_in = rid & (seg - 1)
    ones_b = jnp.ones((rows, dv), BF16)
    scale = dk ** -0.5

    def pair(i, carry):
        r0 = pl.multiple_of(i * rows, rows)
        prep = []
        for h in range(hh):
            ks = slice(h * dk, (h + 1) * dk)
            vs = slice(h * dv, (h + 1) * dv)
            bc = la_ref[pl.ds(r0, rows), ks]
            k = 1
            while k < seg:
                bc = bc + jnp.where(t_in >= k, pltpu.roll(bc, k, 0), 0.0)
                k *= 2
            bl = jnp.zeros_like(bc)
            for e in range(per):
                last = e * seg + seg - 1
                bl = jnp.where((rid >> shift) == e, bc[last:last + 1], bl)
            qt = ((q_ref[pl.ds(r0, rows), ks] * scale) * jnp.exp(bc)).astype(BF16)
            kc = k_ref[pl.ds(r0, rows), ks]
            kt = (kc * jnp.exp(-bc)).astype(BF16)
            ke = kc * jnp.exp(bl - bc)
            vb = v_ref[pl.ds(r0, rows), vs].astype(BF16)
            per_seq = []
            for e in range(per):
                last = e * seg + seg - 1
                dec = jnp.exp(bc[last:last + 1])
                d_hi, d_mid, d_lo = _split3_bf16(dec)
                d3 = jnp.where(rid == 0, d_hi.astype(F32),
                               jnp.where(rid == 1, d_mid.astype(F32),
                                         jnp.where(rid == 2, d_lo.astype(F32), 0.0))).astype(BF16)
                ke_e = jnp.where((rid >> shift) == e, ke, 0.0).astype(BF16)
                per_seq.append((d3, ke_e, s_ref[i * per + e, h]))
            prep.append((vs, qt, kt, vb, per_seq))
        scores = [lax.dot_general(qt, kt, _CONTRACT_LAST, preferred_element_type=F32) for _, qt, kt, _, _ in prep]
        inter, dec_cols, kvs = [], [], []
        for _, qt, _, vb, per_seq in prep:
            for d3, ke_e, st in per_seq:
                inter.append(_dot(qt, st.astype(BF16)))
                dec_cols.append(lax.dot_general(d3, ones_b, _CONTRACT_FIRST, preferred_element_type=F32))
                kvs.append(lax.dot_general(ke_e, vb, _CONTRACT_FIRST, preferred_element_type=F32))
        intra = [_dot(jnp.where(causal, s, 0.0).astype(BF16), vb) for s, (_, _, _, vb, _) in zip(scores, prep)]
        for h, (vs, _, _, _, per_seq) in enumerate(prep):
            o_inter = inter[h * per]
            for e in range(per):
                if e:
                    o_inter = jnp.where((rid >> shift) == e, inter[h * per + e], o_inter)
                so_ref[i * per + e, h] = dec_cols[h * per + e] * per_seq[e][2] + kvs[h * per + e]
            y_ref[pl.ds(r0, rows), vs] = _gla_out(intra[h] + o_inter, gain_ref[...], r_ref[pl.ds(r0, rows), vs])
        return carry

    lax.fori_loop(0, nb // per, pair, 0)


def _gla_step(proj, gain, state, *, seg, nb=8):
    n = proj.shape[0]
    bsz, hh, dk, dv = state.shape
    rows = nb * seg
    c_q = 1024 // (hh * dk)
    c_k = c_q + 1
    c_v = (1024 + 2 * hh * dk) // (hh * dv)
    c_r = c_v + 1
    c_la = (1024 + 2 * hh * dk + 2 * hh * dv) // (hh * dk)
    spec = lambda w, c0: pl.BlockSpec((rows, w), lambda i: (i, c0))
    body = functools.partial(_gla_step_body, nb=nb, seg=seg, dk=dk, dv=dv)
    return pl.pallas_call(
        body,
        grid=(bsz // nb,),
        in_specs=[spec(hh * dk, c_q), spec(hh * dk, c_k), spec(hh * dv, c_v), spec(hh * dk, c_la), spec(hh * dv, c_r),
                  pl.BlockSpec((1, dv), lambda i: (0, 0)),
                  pl.BlockSpec((nb, hh, dk, dv), lambda i: (i, 0, 0, 0))],
        out_specs=(pl.BlockSpec((rows, hh * dv), lambda i: (i, 0)),
                   pl.BlockSpec((nb, hh, dk, dv), lambda i: (i, 0, 0, 0))),
        out_shape=(jax.ShapeDtypeStruct((n, hh * dv), F32), jax.ShapeDtypeStruct(state.shape, F32)),
        compiler_params=_params("arbitrary"),
        name="gla_step",
    )(proj, proj, proj, proj, proj, gain.reshape(1, dv), state)


def _tiled_linear(name, prologue, row_ins, aux_ins, aux_specs, w, *, tm, tn, residual=None):
    n = row_ins[0].shape[0]
    assert n == tm
    k_dim, n_cols = w.shape
    nr, na = len(row_ins), len(aux_ins)
    has_res = residual is not None

    def body(*refs):
        row_refs = refs[:nr]
        aux_refs = refs[nr:nr + na]
        w_ref = refs[nr + na]
        o_ref, wb_ref, h_ref = refs[-3], refs[-2], refs[-1]

        @pl.when(pl.program_id(1) == 0)
        def _():
            prologue(row_refs, aux_refs, h_ref)

        wb = w_ref[...].astype(BF16)
        wb_ref[...] = wb
        acc = _dot(h_ref[...], wb)
        if has_res:
            acc = acc + refs[nr + na + 1][...]
        o_ref[...] = acc

    in_specs = [pl.BlockSpec((tm, a.shape[1]), lambda i, j: (i, 0)) for a in row_ins]
    in_specs += list(aux_specs)
    in_specs.append(pl.BlockSpec((k_dim, tn), lambda i, j: (0, j)))
    args = list(row_ins) + list(aux_ins) + [w]
    if has_res:
        in_specs.append(pl.BlockSpec((tm, tn), lambda i, j: (i, j)))
        args.append(residual)
    return pl.pallas_call(
        body,
        grid=(n // tm, n_cols // tn),
        in_specs=in_specs,
        out_specs=(pl.BlockSpec((tm, tn), lambda i, j: (i, j)), pl.BlockSpec((k_dim, tn), lambda i, j: (0, j))),
        out_shape=(jax.ShapeDtypeStruct((n, n_cols), F32), jax.ShapeDtypeStruct((k_dim, n_cols), BF16)),
        scratch_shapes=[pltpu.VMEM((tm, k_dim), BF16)],
        compiler_params=_params("arbitrary", "arbitrary"),
        name=name,
    )(*args)


def _whole(a):
    return pl.BlockSpec(a.shape, lambda i, j: (0,) * a.ndim)


def _norm_linear(name, x, gain, w, *, tm, tn=512):
    def prologue(row_refs, aux_refs, h_ref):
        _rms_rows_to_bf16(row_refs[0], aux_refs[0], h_ref, tm)
    g = gain.reshape(1, -1)
    return _tiled_linear(name, prologue, [x], [g], [_whole(g)], w, tm=tm, tn=tn)


def _cast_linear(name, x, w, residual, *, tm, tn=512):
    def prologue(row_refs, aux_refs, h_ref):
        def f(r0):
            h_ref[pl.ds(r0, 128), :] = row_refs[0][pl.ds(r0, 128), :].astype(BF16)
        _for_row_chunks(tm, 128, f)
    return _tiled_linear(name, prologue, [x], [], [], w, tm=tm, tn=tn, residual=residual)


def _out_proj(z_s5, y_gla, gain_s5, w_out, x, *, tm, tn=512):
    ws = z_s5.shape[1]

    def prologue(row_refs, aux_refs, h_ref):
        def f(r0):
            h_ref[pl.ds(r0, 128), 0:ws] = _rms(row_refs[0][pl.ds(r0, 128), :], aux_refs[0][...]).astype(BF16)
            h_ref[pl.ds(r0, 128), ws:] = row_refs[1][pl.ds(r0, 128), :].astype(BF16)
        _for_row_chunks(tm, 128, f)
    g = gain_s5.reshape(1, -1)
    return _tiled_linear("out_proj", prologue, [z_s5, y_gla], [g], [_whole(g)], w_out, tm=tm, tn=tn, residual=x)


def _mix_xattn_seq_body(z_ref, y_ref, xres_ref, gs5_ref, gx_ref, wout_ref, wq_ref, wo_ref, mk_ref, mv_ref,
                        o_ref, x1_ref, h_ref, q_ref, *, tm, tn, nt, hd, chunk):
    j = pl.program_id(1)
    ws = z_ref.shape[1]
    scale = hd ** -0.5

    @pl.when(j == 0)
    def _():
        def f(r0):
            h_ref[pl.ds(r0, 128), 0:ws] = _rms(z_ref[pl.ds(r0, 128), :], gs5_ref[...]).astype(BF16)
            h_ref[pl.ds(r0, 128), ws:] = y_ref[pl.ds(r0, 128), :].astype(BF16)
        _for_row_chunks(tm, 128, f)

    @pl.when(j == nt)
    def _():
        _rms_rows_to_bf16(x1_ref, gx_ref, h_ref, tm)

    @pl.when(j == 2 * nt)
    def _():
        heads = [slice(h * hd, (h + 1) * hd) for h in range(X_HEADS)]

        def f(r0):
            scores = [lax.dot_general(q_ref[pl.ds(r0, chunk), hs], mk_ref[:, hs], _CONTRACT_LAST,
                                      preferred_element_type=F32) * scale for hs in heads]
            probs = []
            for s in scores:
                e = jnp.exp(s - jnp.max(s, axis=-1, keepdims=True))
                probs.append((e / jnp.sum(e, axis=-1, keepdims=True)).astype(BF16))
            for hs, p in zip(heads, probs):
                h_ref[pl.ds(r0, chunk), hs] = _dot(p, mv_ref[:, hs]).astype(BF16)
        _for_row_chunks(tm, chunk, f)

    for jj in range(nt):
        cols = slice(jj * tn, (jj + 1) * tn)

        @pl.when(j == jj)
        def _(cols=cols):
            x1_ref[:, cols] = _dot(h_ref[...], wout_ref[...]) + xres_ref[...]

        @pl.when(j == nt + jj)
        def _(cols=cols):
            q_ref[:, cols] = _dot(h_ref[...], wq_ref[...]).astype(BF16)

        @pl.when(j == 2 * nt + jj)
        def _(cols=cols):
            o_ref[...] = _dot(h_ref[...], wo_ref[...]) + x1_ref[:, cols]


def _mix_xattn_seq(z_s5, y_gla, x, gain_s5, gain_x, w_out, w_xq, w_xo, mk_b, mv_b, *, tm, seq, tn=512, chunk=256):
    n, dm = x.shape
    nt = dm // tn
    tiles_per_seq = seq // tm
    body = functools.partial(_mix_xattn_seq_body, tm=tm, tn=tn, nt=nt, hd=dm // X_HEADS, chunk=chunk)
    gs = gain_s5.reshape(1, -1)
    gx = gain_x.reshape(1, -1)
    phase = lambda p: (lambda i, j: (0, jnp.clip(j - p * nt, 0, nt - 1)))
    mspec = pl.BlockSpec((None,) + mk_b.shape[1:], lambda i, j: (i // tiles_per_seq, 0, 0))
    return pl.pallas_call(
        body,
        grid=(n // tm, 3 * nt),
        in_specs=[
            pl.BlockSpec((tm, z_s5.shape[1]), lambda i, j: (i, 0)),
            pl.BlockSpec((tm, y_gla.shape[1]), lambda i, j: (i, 0)),
            pl.BlockSpec((tm, tn), lambda i, j: (i, jnp.minimum(j, nt - 1))),
            _whole(gs), _whole(gx),
            pl.BlockSpec((dm, tn), phase(0)), pl.BlockSpec((dm, tn), phase(1)), pl.BlockSpec((dm, tn), phase(2)),
            mspec, mspec,
        ],
        out_specs=pl.BlockSpec((tm, tn), lambda i, j: (i, jnp.clip(j - 2 * nt, 0, nt - 1))),
        out_shape=jax.ShapeDtypeStruct((n, dm), F32),
        scratch_shapes=[pltpu.VMEM((tm, dm), F32), pltpu.VMEM((tm, dm), BF16), pltpu.VMEM((tm, dm), BF16)],
        compiler_params=_params("arbitrary", "arbitrary"),
        name="mix_xattn_seq",
    )(z_s5, y_gla, x, gs, gx, w_out, w_xq, w_xo, mk_b, mv_b)


def _cache_rows(c):
    b, m, h, hd = c.shape
    n_chunk = hd // LANES
    return c.reshape(b, m, h, n_chunk, LANES).transpose(0, 1, 3, 2, 4).reshape(b * m * n_chunk * h, LANES)


def _mem_proj_body(x_ref, gain_ref, w_ref, o_ref, ob_ref, h_ref, *, tm, hd):
    j = pl.program_id(1)

    @pl.when(j == 0)
    def _():
        _rms_rows_to_bf16(x_ref, gain_ref, h_ref, tm)

    y = _dot(h_ref[...], w_ref[...].astype(BF16))
    ob_ref[...] = y.astype(BF16)
    n_chunk = hd // LANES
    pitch = n_chunk * X_HEADS
    for ck in range(n_chunk):
        o_ref[pl.ds(ck * X_HEADS + j, tm, stride=pitch), :] = y[:, ck * LANES:(ck + 1) * LANES]


def _mem_proj(name, mem, gain, w, *, tm):
    b, m, dm = mem.shape
    n = b * m
    hd = dm // X_HEADS
    n_chunk = hd // LANES
    pitch = n_chunk * X_HEADS
    body = functools.partial(_mem_proj_body, tm=tm, hd=hd)
    g = gain.reshape(1, dm)
    rows, plain = pl.pallas_call(
        body,
        grid=(n // tm, X_HEADS),
        in_specs=[pl.BlockSpec((tm, dm), lambda i, j: (i, 0)), _whole(g), pl.BlockSpec((dm, hd), lambda i, j: (0, j))],
        out_specs=(pl.BlockSpec((tm * pitch, LANES), lambda i, j: (i, 0)),
                   pl.BlockSpec((tm, hd), lambda i, j: (i, j))),
        out_shape=(jax.ShapeDtypeStruct((n * pitch, LANES), F32), jax.ShapeDtypeStruct((n, dm), BF16)),
        scratch_shapes=[pltpu.VMEM((tm, dm), BF16)],
        compiler_params=_params("arbitrary", "arbitrary"),
        name=name,
    )(mem.reshape(n, dm), g, w)
    full = rows.reshape(b, m, n_chunk, X_HEADS, LANES).transpose(0, 1, 3, 2, 4).reshape(1, b, m, X_HEADS, hd)
    return full, plain.reshape(b, m, dm)


def _xattn_step_body(q_ref, k_ref, v_ref, o_ref, *, nb, seg, hd, n_mem):
    rows = SUBLANES
    per = rows // seg
    shift = seg.bit_length() - 1
    rid = lax.broadcasted_iota(jnp.int32, (rows, 1), 0)
    scale = hd ** -0.5
    n_chunk = hd // LANES
    pitch = n_chunk * X_HEADS

    def head_rows(ref, b, h):
        parts = [ref[pl.ds(b * n_mem * pitch + ck * X_HEADS + h, n_mem, stride=pitch), :] for ck in range(n_chunk)]
        return jnp.concatenate(parts, axis=1).astype(BF16)

    pairs = [(b, h) for b in range(nb) for h in range(X_HEADS)]
    scores = []
    for b, h in pairs:
        r0 = (b // per) * rows
        q = q_ref[r0:r0 + rows, h * hd:(h + 1) * hd].astype(BF16)
        scores.append(lax.dot_general(q, head_rows(k_ref, b, h), _CONTRACT_LAST, preferred_element_type=F32) * scale)
    probs = []
    for s in scores:
        e = jnp.exp(s - jnp.max(s, axis=-1, keepdims=True))
        probs.append((e / jnp.sum(e, axis=-1, keepdims=True)).astype(BF16))
    outs = {}
    for (b, h), p in zip(pairs, probs):
        outs[b, h] = _dot(p, head_rows(v_ref, b, h))
    for g in range(nb // per):
        for h in range(X_HEADS):
            o = outs[g * per, h]
            for e in range(1, per):
                o = jnp.where((rid >> shift) == e, outs[g * per + e, h], o)
            o_ref[g * rows:(g + 1) * rows, h * hd:(h + 1) * hd] = o


def _xattn_step(q, mem_k, mem_v, *, seg, nb=4):
    n, dm = q.shape
    bsz, m, _, hd = mem_k.shape
    body = functools.partial(_xattn_step_body, nb=nb, seg=seg, hd=hd, n_mem=m)
    rows_per_seq = m * dm // LANES
    mspec = pl.BlockSpec((nb * rows_per_seq, LANES), lambda i: (i, 0))
    return pl.pallas_call(
        body,
        grid=(bsz // nb,),
        in_specs=[pl.BlockSpec((nb * seg, dm), lambda i: (i, 0)), mspec, mspec],
        out_specs=pl.BlockSpec((nb * seg, dm), lambda i: (i, 0)),
        out_shape=jax.ShapeDtypeStruct((n, dm), F32),
        compiler_params=_params("arbitrary"),
        name="xattn_step",
    )(q, _cache_rows(mem_k), _cache_rows(mem_v))


def _ffn_finish(conv, gate, wd_ref, o_ref, gfin_ref, j, nj, tm):
    act = (_gelu(conv) * gate).astype(BF16)
    o_ref[...] += _dot(act, wd_ref[...].astype(BF16))

    @pl.when(j == nj - 1)
    def _():
        def f(r0):
            o_ref[pl.ds(r0, 128), :] = _rms(o_ref[pl.ds(r0, 128), :], gfin_ref[...])
        _for_row_chunks(tm, 128, f)


def _ffn_seq_body(x_ref, gain_ref, wa_ref, wg_ref, wd_ref, cw_ref, cb_ref, gfin_ref, o_ref, tail_ref, h_ref, a_ref,
                  *, tm, halo, tiles_per_seq, nj):
    i = pl.program_id(0)
    j = pl.program_id(1)

    @pl.when(j == 0)
    def _():
        @pl.when(i % tiles_per_seq == 0)
        def _():
            h_ref[0:halo, :] = jnp.zeros((halo, h_ref.shape[1]), BF16)

        @pl.when(i % tiles_per_seq != 0)
        def _():
            h_ref[0:halo, :] = h_ref[tm:tm + halo, :]

        _rms_rows_to_bf16(x_ref, gain_ref, h_ref, tm, row0=halo)
        o_ref[...] = x_ref[...]

    a_ref[...] = _dot(h_ref[...], wa_ref[...].astype(BF16))
    gate = _dot(h_ref[halo:halo + tm, :], wg_ref[...].astype(BF16))
    conv = cb_ref[...]
    for tap in range(CONV_W):
        conv = conv + a_ref[pl.ds(halo - (CONV_W - 1) + tap, tm), :] * cw_ref[tap:tap + 1, :]
    tail_ref[...] = a_ref[pl.ds(halo + tm - SUBLANES, SUBLANES), :]
    _ffn_finish(conv, gate, wd_ref, o_ref, gfin_ref, j, nj, tm)


def _ffn_seq(x, gain, w_a, w_g, conv_w, conv_b, w_down, gain_final, *, tm, seq, tf=512, halo=16):
    n, dm = x.shape
    dff = w_down.shape[0]
    nj = dff // tf
    body = functools.partial(_ffn_seq_body, tm=tm, halo=halo, tiles_per_seq=seq // tm, nj=nj)
    return pl.pallas_call(
        body,
        grid=(n // tm, nj),
        in_specs=[
            pl.BlockSpec((tm, dm), lambda i, j: (i, 0)),
            pl.BlockSpec((1, dm), lambda i, j: (0, 0)),
            pl.BlockSpec((dm, tf), lambda i, j: (0, j)),
            pl.BlockSpec((dm, tf), lambda i, j: (0, j)),
            pl.BlockSpec((tf, dm), lambda i, j: (j, 0)),
            pl.BlockSpec((CONV_W, tf), lambda i, j: (0, j)),
            pl.BlockSpec((1, tf), lambda i, j: (0, j)),
            pl.BlockSpec((1, dm), lambda i, j: (0, 0)),
        ],
        out_specs=(pl.BlockSpec((tm, dm), lambda i, j: (i, 0)),
                   pl.BlockSpec((None, SUBLANES, tf), lambda i, j: (i, 0, j))),
        out_shape=(jax.ShapeDtypeStruct((n, dm), F32),
                   jax.ShapeDtypeStruct((n // tm, SUBLANES, dff), F32)),
        scratch_shapes=[pltpu.VMEM((halo + tm, dm), BF16), pltpu.VMEM((halo + tm, tf), F32)],
        compiler_params=_params("arbitrary", "arbitrary"),
        name="ffn_seq",
    )(x, gain.reshape(1, dm), w_a, w_g, w_down, conv_w, conv_b.reshape(1, dff), gain_final.reshape(1, dm))


def _dot_exact_rows(sel_b, x):
    hi, mid, lo = _split3_bf16(x)
    return _dot(sel_b, hi) + _dot(sel_b, mid) + _dot(sel_b, lo)


def _ffn_step_body(x_ref, gain_ref, wa_ref, wg_ref, wd_ref, cw_ref, cb_ref, gfin_ref, b0_ref, b1_ref,
                   o_ref, a_new_ref, wa_b_ref, wg_b_ref, wd_b_ref, h_ref, *, tm, seg, nj):
    j = pl.program_id(1)
    nseq = tm // seg
    shift = seg.bit_length() - 1

    @pl.when(j == 0)
    def _():
        _rms_rows_to_bf16(x_ref, gain_ref, h_ref, tm)
        o_ref[...] = x_ref[...]

    wa_b_ref[...] = wa_ref[...].astype(BF16)
    wg_b_ref[...] = wg_ref[...].astype(BF16)
    wd_b_ref[...] = wd_ref[...].astype(BF16)
    a = _dot(h_ref[...], wa_b_ref[...])
    gate = _dot(h_ref[...], wg_b_ref[...])
    t_in = lax.broadcasted_iota(jnp.int32, (tm, 1), 0) & (seg - 1)
    prev1 = jnp.where(t_in >= 1, pltpu.roll(a, 1, 0), 0.0)
    prev2 = jnp.where(t_in >= 2, pltpu.roll(a, 2, 0), 0.0)
    conv = cb_ref[...] + prev2 * cw_ref[0:1, :]
    conv = conv + prev1 * cw_ref[1:2, :]
    conv = conv + a * cw_ref[2:3, :]
    b0 = b0_ref[...]
    b1 = b1_ref[...]
    carried = jnp.concatenate([b0 * cw_ref[0:1, :] + b1 * cw_ref[1:2, :], b1 * cw_ref[0:1, :]], axis=0)
    r = lax.broadcasted_iota(jnp.int32, (tm, 2 * nseq), 0)
    c = lax.broadcasted_iota(jnp.int32, (tm, 2 * nseq), 1)
    spread = ((r >> shift) == (c & (nseq - 1))) & ((r & (seg - 1)) == (c >> (nseq.bit_length() - 1)))
    conv = conv + _dot_exact_rows(jnp.where(spread, 1.0, 0.0).astype(BF16), carried)
    r = lax.broadcasted_iota(jnp.int32, (2 * nseq, tm), 0)
    c = lax.broadcasted_iota(jnp.int32, (2 * nseq, tm), 1)
    pick = ((c >> shift) == (r & (nseq - 1))) & ((c & (seg - 1)) == seg - 2 + (r >> (nseq.bit_length() - 1)))
    picked = _dot_exact_rows(jnp.where(pick, 1.0, 0.0).astype(BF16), a)
    a_new_ref[0] = picked[0:nseq]
    a_new_ref[1] = picked[nseq:2 * nseq]
    _ffn_finish(conv, gate, wd_b_ref, o_ref, gfin_ref, j, nj, tm)


def _ffn_step(x, gain, w_up, conv_w, conv_b, w_down, gain_final, b0, b1, *, seg, tf=256):
    n, dm = x.shape
    dff = w_down.shape[0]
    nj = dff // tf
    tm = n
    nseq = n // seg
    body = functools.partial(_ffn_step_body, tm=tm, seg=seg, nj=nj)
    return pl.pallas_call(
        body,
        grid=(1, nj),
        in_specs=[
            pl.BlockSpec((tm, dm), lambda i, j: (i, 0)),
            pl.BlockSpec((1, dm), lambda i, j: (0, 0)),
            pl.BlockSpec((dm, tf), lambda i, j: (0, j)),
            pl.BlockSpec((dm, tf), lambda i, j: (0, nj + j)),
            pl.BlockSpec((tf, dm), lambda i, j: (j, 0)),
            pl.BlockSpec((CONV_W, tf), lambda i, j: (0, j)),
            pl.BlockSpec((1, tf), lambda i, j: (0, j)),
            pl.BlockSpec((1, dm), lambda i, j: (0, 0)),
            pl.BlockSpec((nseq, tf), lambda i, j: (0, j)),
            pl.BlockSpec((nseq, tf), lambda i, j: (0, j)),
        ],
        out_specs=(pl.BlockSpec((tm, dm), lambda i, j: (i, 0)),
                   pl.BlockSpec((CONV_W - 1, nseq, tf), lambda i, j: (0, 0, j)),
                   pl.BlockSpec((dm, tf), lambda i, j: (0, j)),
                   pl.BlockSpec((dm, tf), lambda i, j: (0, j)),
                   pl.BlockSpec((tf, dm), lambda i, j: (j, 0))),
        out_shape=(jax.ShapeDtypeStruct((n, dm), F32), jax.ShapeDtypeStruct((CONV_W - 1, nseq, dff), F32),
                   jax.ShapeDtypeStruct((dm, dff), BF16), jax.ShapeDtypeStruct((dm, dff), BF16),
                   jax.ShapeDtypeStruct((dff, dm), BF16)),
        scratch_shapes=[pltpu.VMEM((tm, dm), BF16)],
        compiler_params=_params("arbitrary", "arbitrary"),
        name="ffn_step",
    )(x, gain.reshape(1, dm), w_up, w_up, w_down, conv_w, conv_b.reshape(1, dff), gain_final.reshape(1, dm),
      b0, b1)


def kernel(x_prompt, x_sample, mem_prompt, cache_mem_k, cache_mem_v, state_s5_re, state_s5_im, state_gla, state_conv, norm_mix, w_in, s5_lambda_re, s5_lambda_im, s5_log_dt, s5_b_re, s5_b_im, s5_c_re, s5_c_im, s5_d, s5_w_glu, s5_b_glu, norm_s5_out, gla_w_g2, gla_b_g, norm_gla_out, w_out, norm_xattn, norm_mem, w_xq, w_xk, w_xv, w_xo, norm_ffn, w_up, conv_w, conv_b, w_down, norm_final):
    depth = w_in.shape[0]
    assert depth == 1
    bp, seq, dm = x_prompt.shape
    bs, seg, _ = x_sample.shape
    n_mem = mem_prompt.shape[1]
    dff = w_down.shape[1]
    l = 0
    s5_args = (s5_lambda_re[l], s5_lambda_im[l], s5_log_dt[l], s5_b_re[l], s5_b_im[l], s5_c_re[l], s5_c_im[l],
               s5_d[l], s5_w_glu[l], s5_b_glu[l])
    tm_p = min(1024, seq)
    n_s = bs * seg
    w_in = w_in.astype(BF16)
    s5_w, s5_powers = _s5_weights(*s5_args)
    g_total = s5_lambda_re.shape[1]
    n_state = s5_lambda_re.shape[2]

    xs = x_sample.reshape(n_s, dm)
    proj_s = _in_proj(xs, norm_mix[l], w_in[l], gla_w_g2[l], gla_b_g[l], tm=n_s)
    h0r = state_s5_re[l].reshape(1, bs, -1)
    h0i = state_s5_im[l].reshape(1, bs, -1)
    z_s5_s, hs_r, hs_i = _s5_mixer(proj_s[None], s5_w, _s5_scan_table(s5_powers, seg, 1), tt=n_s, steps=seg,
                                   h0=(h0r, h0i))
    y_gla_s, s_gla = _gla_step(proj_s, norm_gla_out[l], state_gla[l], seg=seg)
    x1s, w_out_b = _out_proj(z_s5_s[0], y_gla_s, norm_s5_out[l], w_out[l], xs, tm=n_s)
    qs, w_xq_b = _norm_linear("xattn_q_s", x1s, norm_xattn[l], w_xq[l], tm=n_s)
    att = _xattn_step(qs, cache_mem_k[l], cache_mem_v[l], seg=seg)
    x2s, w_xo_b = _cast_linear("xattn_o_s", att, w_xo[l], x1s, tm=n_s)
    buf = state_conv[l]
    ys, a_new, w_a_b, w_g_b, w_down_b = _ffn_step(x2s, norm_ffn[l], w_up[l], conv_w[l], conv_b[l], w_down[l],
                                                  norm_final, buf[:, 0], buf[:, 1], seg=seg)
    s_re = hs_r.reshape(1, bs, g_total, n_state)
    s_im = hs_i.reshape(1, bs, g_total, n_state)
    s_conv = jnp.swapaxes(a_new, 0, 1)[None]

    mk, mk_b = _mem_proj("mem_k", mem_prompt, norm_mem[l], w_xk[l], tm=min(1024, bp * n_mem))
    mv, mv_b = _mem_proj("mem_v", mem_prompt, norm_mem[l], w_xv[l], tm=min(1024, bp * n_mem))

    xp = x_prompt.reshape(bp * seq, dm)
    proj = _in_proj(xp, norm_mix[l], w_in[l], gla_w_g2[l], gla_b_g[l], tm=tm_p)
    proj3 = proj.reshape(bp, seq, -1)
    steps_p = SUBLANES // bp
    z_s5, hend_r, hend_i = _s5_mixer(proj3, s5_w, _s5_scan_table(s5_powers, steps_p, bp),
                                     tt=min(512 // bp, seq), steps=steps_p)
    y_gla, st_gla = _gla_seq(proj3, norm_gla_out[l], tt=min(512, seq))
    x2 = _mix_xattn_seq(z_s5.reshape(bp * seq, -1), y_gla.reshape(bp * seq, -1), xp, norm_s5_out[l], norm_xattn[l],
                        w_out_b, w_xq_b, w_xo_b, mk_b, mv_b, tm=tm_p, seq=seq)
    yp, tails = _ffn_seq(x2, norm_ffn[l], w_a_b, w_g_b, conv_w[l], conv_b[l], w_down_b, norm_final,
                         tm=tm_p, seq=seq)
    p_re = jnp.swapaxes(hend_r[:, SUBLANES - bp:], 0, 1).reshape(1, bp, g_total, n_state)
    p_im = jnp.swapaxes(hend_i[:, SUBLANES - bp:], 0, 1).reshape(1, bp, g_total, n_state)
    p_gla = jnp.swapaxes(st_gla, -1, -2)[None]
    tiles_per_seq = seq // tm_p
    p_conv = tails.reshape(bp, tiles_per_seq, SUBLANES, dff)[:, -1, SUBLANES - (CONV_W - 1):, :][None]

    return (yp.reshape(bp, seq, dm), ys.reshape(bs, seg, dm), mk, mv,
            p_re, p_im, p_gla, p_conv, s_re, s_im, s_gla[None], s_conv)
```

```python
import functools

import jax
import jax.numpy as jnp
from jax import lax
from jax.experimental import pallas as pl
from jax.experimental.pallas import tpu as pltpu

F32 = jnp.float32
BF16 = jnp.bfloat16

EPS = 1e-6
S5_GROUP = 16
S5_BLOCK_GROUPS = 16
GLA_HEADS = 4
GLA_RANK = 16
GLA_TAU = 16.0
GLA_CHUNK = 64
X_HEADS = 4
CONV_W = 3

SUBLANES = 8
LANES = 128
VMEM_LIMIT_BYTES = 56 * 1024 * 1024

_CONTRACT_LAST = (((1,), (1,)), ((), ()))
_CONTRACT_FIRST = (((0,), (0,)), ((), ()))


def _params(*semantics):
    return pltpu.CompilerParams(dimension_semantics=semantics, vmem_limit_bytes=VMEM_LIMIT_BYTES)


def _dot(a, b):
    return jnp.dot(a, b, preferred_element_type=F32)


def _for_row_chunks(n_rows, chunk, fn):
    def body(i, carry):
        fn(pl.multiple_of(i * chunk, chunk))
        return carry
    lax.fori_loop(0, n_rows // chunk, body, 0)


def _rms(x, gain):
    r = lax.rsqrt(jnp.mean(x * x, axis=-1, keepdims=True) + EPS)
    return (x * r) * gain


def _gelu(x):
    return 0.5 * x * (1.0 + lax.erf(x * (2.0 ** -0.5)))


def _rms_rows_to_bf16(x_ref, gain_ref, h_ref, n_rows, row0=0, chunk=128):
    def f(r0):
        h_ref[pl.ds(row0 + r0, chunk), :] = _rms(x_ref[pl.ds(r0, chunk), :], gain_ref[...]).astype(BF16)
    _for_row_chunks(n_rows, chunk, f)


def _s5_prep_body(lr_ref, li_ref, ldt_ref, br_ref, bi_ref, apr_ref, api_ref, bbr_ref, bbi_ref):
    lr = lr_ref[...]
    li = li_ref[...]
    dt = jnp.exp(ldt_ref[...])
    mag = jnp.exp(lr * dt)
    ar = mag * jnp.cos(li * dt)
    ai = mag * jnp.sin(li * dt)
    den = lr * lr + li * li
    pr = ar - 1.0
    qr = (pr * lr + ai * li) / den
    qi = (ai * lr - pr * li) / den
    for p in range(S5_GROUP):
        br = br_ref[p]
        bi = bi_ref[p]
        bbr_ref[p] = qr * br - qi * bi
        bbi_ref[p] = qr * bi + qi * br
    cr, ci = ar, ai
    apr_ref[0] = cr
    api_ref[0] = ci
    for k in range(1, SUBLANES):
        cr, ci = cr * ar - ci * ai, cr * ai + ci * ar
        apr_ref[k] = cr
        api_ref[k] = ci


def _s5_prep(lam_re, lam_im, log_dt, b_re, b_im):
    g, n = lam_re.shape
    out = jax.ShapeDtypeStruct((SUBLANES, g, n), F32)
    outb = jax.ShapeDtypeStruct((S5_GROUP, g, n), F32)
    return pl.pallas_call(
        _s5_prep_body, out_shape=(out, out, outb, outb), name="s5_prep",
    )(lam_re, lam_im, log_dt.reshape(g, 1), jnp.transpose(b_re, (2, 0, 1)), jnp.transpose(b_im, (2, 0, 1)))


def _s5_weights(lam_re, lam_im, log_dt, b_re, b_im, c_re, c_im, d, w_glu, b_glu):
    g, n = lam_re.shape
    p = S5_GROUP
    gb = S5_BLOCK_GROUPS
    nb = g // gb
    apr, api, bbr, bbi = _s5_prep(lam_re, lam_im, log_dt, b_re, b_im)

    def same_group(rows, row_w, cols, col_w):
        return (jnp.arange(rows)[:, None] // row_w) == (jnp.arange(cols)[None, :] // col_w)

    def tile_cols(x, reps):
        w = x.shape[-1]
        sel = (jnp.arange(reps * w)[None, :] % w == jnp.arange(w)[:, None]).astype(BF16)
        return jnp.einsum('brw,wj->brj', x, sel, preferred_element_type=F32)

    def tile_rows(x, reps):
        h = x.shape[1]
        sel = (jnp.arange(reps * h)[:, None] % h == jnp.arange(h)[None, :]).astype(BF16)
        return jnp.einsum('rh,bhc->brc', sel, x, preferred_element_type=F32)

    def bu_block(bb):
        per_block = jnp.transpose(bb.reshape(p, nb, gb * n), (1, 0, 2)).astype(BF16)
        return jnp.where(same_group(gb * p, p, gb * n, n), tile_rows(per_block, gb), 0.0)

    def c_block(c):
        per_block = jnp.transpose(c.reshape(nb, gb, p, n), (0, 1, 3, 2)).reshape(nb, gb * n, p).astype(BF16)
        return jnp.where(same_group(gb * n, n, gb * p, p), tile_cols(per_block, gb), 0.0)

    w_bu = jnp.concatenate([bu_block(bbr), bu_block(bbi)], axis=-1).astype(BF16)
    w_c = jnp.concatenate([c_block(c_re), -c_block(c_im)], axis=1).astype(BF16)
    w_g = jnp.where(same_group(gb * p, p, gb * p, p),
                    tile_cols(w_glu.reshape(nb, gb * p, p).astype(BF16), gb), 0.0).astype(BF16)
    d_b = d.reshape(nb, 1, gb * p)
    bg_b = b_glu.reshape(nb, 1, gb * p)
    pw_r = apr.reshape(SUBLANES, nb, gb * n)
    pw_i = api.reshape(SUBLANES, nb, gb * n)
    return (w_bu, w_c, w_g, d_b, bg_b), (pw_r, pw_i)


def _s5_scan_table(powers, steps, rows_per_step):
    pw_r, pw_i = powers
    t = (jnp.arange(SUBLANES) // rows_per_step) % steps

    def masked(k):
        m = (t >= k)[:, None, None]
        return jnp.where(m, pw_r[k - 1][None], 0.0), jnp.where(m, pw_i[k - 1][None], 0.0)

    tabs = []
    k = 1
    while k < steps:
        tabs.extend(masked(k))
        k *= 2
    tabs.append(pw_r[t])
    tabs.append(pw_i[t])
    whole = jnp.full((SUBLANES,), steps - 1)
    tabs.append(pw_r[whole])
    tabs.append(pw_i[whole])
    return jnp.transpose(jnp.stack(tabs), (2, 0, 1, 3))


def _in_proj_body(x_ref, gain_ref, w_ref, wr_ref, wg_ref, wg2_ref, bg_ref, o_ref, h_ref, *, tm, n_main, n_r):
    j = pl.program_id(1)

    @pl.when(j == 0)
    def _():
        _rms_rows_to_bf16(x_ref, gain_ref, h_ref, tm)

    @pl.when(j < n_main)
    def _():
        o_ref[...] = _dot(h_ref[...], w_ref[...].astype(BF16))

    @pl.when((j >= n_main) & (j < n_main + n_r))
    def _():
        o_ref[...] = _dot(h_ref[...], wr_ref[...].astype(BF16))

    @pl.when(j == n_main + n_r)
    def _():
        g_low = _dot(h_ref[...], wg_ref[...].astype(BF16))
        pre = _dot(g_low.astype(BF16), wg2_ref[...].astype(BF16)) + bg_ref[...]
        log_sig = jnp.minimum(pre, 0.0) - jnp.log1p(jnp.exp(-jnp.abs(pre)))
        o_ref[...] = log_sig * (1.0 / GLA_TAU)


def _in_proj(x, gain, w_in, w_g2, b_g, *, tm, tn=512):
    n, dm = x.shape
    mix = dm
    n_u_q_k_v = mix // 2 + 2 * (mix // 4) + mix // 2
    col_g = n_u_q_k_v
    col_r = n_u_q_k_v + GLA_RANK
    w_r = w_in[:, col_r:]
    w_g = w_in[:, col_g:col_r]
    n_main = n_u_q_k_v // tn
    n_r = w_r.shape[1] // tn
    n_cols = n_u_q_k_v + w_r.shape[1] + w_g2.shape[1]
    body = functools.partial(_in_proj_body, tm=tm, n_main=n_main, n_r=n_r)
    return pl.pallas_call(
        body,
        grid=(n // tm, n_main + n_r + 1),
        in_specs=[
            pl.BlockSpec((tm, dm), lambda i, j: (i, 0)),
            pl.BlockSpec((1, dm), lambda i, j: (0, 0)),
            pl.BlockSpec((dm, tn), lambda i, j: (0, jnp.minimum(j, n_main - 1))),
            pl.BlockSpec((dm, tn), lambda i, j: (0, jnp.clip(j - n_main, 0, n_r - 1))),
            pl.BlockSpec((dm, GLA_RANK), lambda i, j: (0, 0)),
            pl.BlockSpec(w_g2.shape, lambda i, j: (0, 0)),
            pl.BlockSpec((1, w_g2.shape[1]), lambda i, j: (0, 0)),
        ],
        out_specs=pl.BlockSpec((tm, tn), lambda i, j: (i, j)),
        out_shape=jax.ShapeDtypeStruct((n, n_cols), F32),
        scratch_shapes=[pltpu.VMEM((tm, dm), BF16)],
        compiler_params=_params("arbitrary", "arbitrary"),
        name="in_proj",
    )(x, gain.reshape(1, dm), w_in, w_r, w_g, w_g2, b_g.reshape(1, -1))


def _s5_body(*refs, tt, steps, nseq, ns, chunk):
    carried = nseq > 1
    if carried:
        (u_ref, wbu_ref, wc_ref, wg_ref, d_ref, bg_ref, tab_ref, z_ref, hr_ref, hi_ref,
         bu_ref, cr_ref, ci_ref, us_ref, zs_ref) = refs
    else:
        u_ref, wbu_ref, wc_ref, wg_ref, d_ref, bg_ref, tab_ref, h0r_ref, h0i_ref, z_ref, hr_ref, hi_ref, bu_ref = refs
    seg = steps
    n_lvl = steps.bit_length() - 1
    lane_group = 2 * LANES
    rid = lax.broadcasted_iota(jnp.int32, (SUBLANES, 1), 0)
    cw = u_ref.shape[-1]
    n_lt = cw // LANES

    if carried:
        @pl.when(pl.program_id(1) == 0)
        def _():
            cr_ref[...] = jnp.zeros_like(cr_ref)
            ci_ref[...] = jnp.zeros_like(ci_ref)

        per = chunk // nseq

    def chunk_u(c):
        if not carried:
            return u_ref[c * chunk:(c + 1) * chunk, :]
        for b in range(nseq):
            for lt in range(n_lt):
                us_ref[c * n_lt + lt, pl.ds(b, per, stride=nseq), :] = (
                    u_ref[b, c * per:(c + 1) * per, lt * LANES:(lt + 1) * LANES])
        return jnp.concatenate([us_ref[c * n_lt + lt] for lt in range(n_lt)], axis=1)

    def project_in(c):
        rows = slice(c * chunk, (c + 1) * chunk)
        bu = _dot(chunk_u(c).astype(BF16), wbu_ref[...])
        if not carried:
            seqs = slice(c * chunk // seg, (c + 1) * chunk // seg)
            a_r = tab_ref[2 * n_lvl, 0:1, :]
            a_i = tab_ref[2 * n_lvl + 1, 0:1, :]
            h0r = h0r_ref[seqs, :]
            h0i = h0i_ref[seqs, :]
            first = (lax.broadcasted_iota(jnp.int32, (chunk, chunk // seg), 0)
                     == seg * lax.broadcasted_iota(jnp.int32, (chunk, chunk // seg), 1))
            first_b = jnp.where(first, 1.0, 0.0).astype(BF16)
            bu = bu + jnp.concatenate([_dot_exact_rows(first_b, a_r * h0r - a_i * h0i),
                                       _dot_exact_rows(first_b, a_r * h0i + a_i * h0r)], axis=1)
        bu_ref[rows, :] = bu

    def scan(c, carry):
        carry = list(carry)
        for blk in range(chunk // SUBLANES):
            rows = slice(c * chunk + blk * SUBLANES, c * chunk + (blk + 1) * SUBLANES)
            for gi in range(ns // lane_group):
                cols = slice(gi * lane_group, (gi + 1) * lane_group)
                cols_im = slice(ns + gi * lane_group, ns + (gi + 1) * lane_group)
                xr = bu_ref[rows, cols]
                xi = bu_ref[rows, cols_im]
                for lvl in range(n_lvl):
                    k = 1 << lvl
                    ar = tab_ref[2 * lvl, :, cols]
                    ai = tab_ref[2 * lvl + 1, :, cols]
                    sr = pltpu.roll(xr, k * nseq, 0)
                    si = pltpu.roll(xi, k * nseq, 0)
                    xr, xi = xr + (ar * sr - ai * si), xi + (ar * si + ai * sr)
                if carried:
                    pr = tab_ref[2 * n_lvl, :, cols]
                    pi = tab_ref[2 * n_lvl + 1, :, cols]
                    c_r, c_i = carry[gi]
                    hr = xr + (pr * c_r - pi * c_i)
                    hi = xi + (pr * c_i + pi * c_r)
                    lr, li = xr, xi
                    half = SUBLANES // 2
                    while half >= nseq:
                        keep = (rid & (2 * half - 1)) >= half
                        lr = jnp.where(keep, lr, pltpu.roll(lr, SUBLANES - half, 0))
                        li = jnp.where(keep, li, pltpu.roll(li, SUBLANES - half, 0))
                        half //= 2
                    er = tab_ref[2 * n_lvl + 2, :, cols]
                    ei = tab_ref[2 * n_lvl + 3, :, cols]
                    carry[gi] = (lr + (er * c_r - ei * c_i), li + (er * c_i + ei * c_r))
                else:
                    hr, hi = xr, xi
                    for e in range(SUBLANES // seg):
                        src = e * seg + seg - 1
                        dst = (c * chunk + blk * SUBLANES) // seg + e
                        hr_ref[dst:dst + 1, cols] = hr[src:src + 1]
                        hi_ref[dst:dst + 1, cols] = hi[src:src + 1]
                bu_ref[rows, cols] = hr
                bu_ref[rows, cols_im] = hi
        return carry

    def project_out(c):
        rows = slice(c * chunk, (c + 1) * chunk)
        if carried:
            u = jnp.concatenate([us_ref[c * n_lt + lt] for lt in range(n_lt)], axis=1)
        else:
            u = u_ref[rows, :]
        y = _dot(bu_ref[rows, :].astype(BF16), wc_ref[...]) + d_ref[...] * u
        z = _gelu(y)
        z = z * jax.nn.sigmoid(_dot(z.astype(BF16), wg_ref[...]) + bg_ref[...])
        if carried:
            for lt in range(n_lt):
                zs_ref[c * n_lt + lt] = z[:, lt * LANES:(lt + 1) * LANES]
            for b in range(nseq):
                for lt in range(n_lt):
                    z_ref[b, c * per:(c + 1) * per, lt * LANES:(lt + 1) * LANES] = (
                        zs_ref[c * n_lt + lt, pl.ds(b, per, stride=nseq), :])
        else:
            z_ref[rows, :] = z

    n_chunks = tt // chunk
    n_groups = ns // lane_group
    group_cols = [slice(gi * lane_group, (gi + 1) * lane_group) for gi in range(n_groups)]
    carry = [(cr_ref[:, cols], ci_ref[:, cols]) for cols in group_cols] if carried else [None] * n_groups
    project_in(0)
    for c in range(n_chunks):
        if c + 1 < n_chunks:
            project_in(c + 1)
        carry = scan(c, carry)
        project_out(c)
    if carried:
        for cols, (c_r, c_i) in zip(group_cols, carry):
            cr_ref[:, cols] = c_r
            ci_ref[:, cols] = c_i
            hr_ref[:, cols] = c_r
            hi_ref[:, cols] = c_i


def _s5_mixer(proj3, weights, tab, *, tt, steps, h0=None, chunk=128):
    w_bu, w_c, w_g, d_b, bg_b = weights
    bsz, seq, _ = proj3.shape
    nb, cw, ns2 = w_bu.shape
    ns = ns2 // 2
    n_tab = tab.shape[1]
    carried = h0 is None
    nseq = bsz if carried else 1
    assert nseq * steps == SUBLANES or not carried
    rows = nseq * tt
    body = functools.partial(_s5_body, tt=rows, steps=steps, nseq=nseq, ns=ns, chunk=min(chunk, rows))
    wspec = lambda shape: pl.BlockSpec((None,) + shape, lambda g, t: (g,) + (0,) * len(shape))
    lead = bsz if carried else None
    uz_spec = pl.BlockSpec((lead, tt, cw), lambda g, t: (0, t, g))
    in_specs = [
        uz_spec,
        wspec((cw, ns2)), wspec((ns2, cw)), wspec((cw, cw)), wspec((1, cw)), wspec((1, cw)),
        wspec((n_tab, SUBLANES, ns)),
    ]
    args = [proj3, w_bu, w_c, w_g, d_b, bg_b, tab]
    z_shape = jax.ShapeDtypeStruct((bsz, seq, nb * cw), F32)
    scratch = [pltpu.VMEM((rows, ns2), F32)]
    if carried:
        st_shape = jax.ShapeDtypeStruct((nb, SUBLANES, ns), F32)
        st_spec = pl.BlockSpec((None, SUBLANES, ns), lambda g, t: (g, 0, 0))
        slabs = (rows // min(chunk, rows)) * (cw // LANES)
        scratch += [pltpu.VMEM((SUBLANES, ns), F32), pltpu.VMEM((SUBLANES, ns), F32),
                    pltpu.VMEM((slabs, min(chunk, rows), LANES), F32), pltpu.VMEM((slabs, min(chunk, rows), LANES), F32)]
    else:
        assert bsz == 1
        h0r, h0i = h0
        st_spec = pl.BlockSpec((None, tt // steps, ns), lambda g, t: (0, t, g))
        in_specs += [st_spec] * 2
        args += [h0r, h0i]
        st_shape = jax.ShapeDtypeStruct((bsz, seq // steps, nb * ns), F32)
    return pl.pallas_call(
        body,
        grid=(nb, seq // tt),
        in_specs=in_specs,
        out_specs=(uz_spec, st_spec, st_spec),
        out_shape=(z_shape, st_shape, st_shape),
        scratch_shapes=scratch,
        compiler_params=_params("arbitrary", "arbitrary"),
        name="s5_carried" if carried else "s5_segments",
    )(*args)


def _split3_bf16(x):
    hi = x.astype(BF16)
    r1 = x - hi.astype(F32)
    mid = r1.astype(BF16)
    lo = (r1 - mid.astype(F32)).astype(BF16)
    return hi, mid, lo


def _gla_out(o, gain, r):
    return _rms(o, gain) * (r * jax.nn.sigmoid(r))


def _gla_seq_body(q_ref, k_ref, v_ref, la_ref, r_ref, gain_ref, y_ref, st_out_ref, st_ref, *, tt, dk, dv):
    c = GLA_CHUNK
    hh = GLA_HEADS

    @pl.when(pl.program_id(1) == 0)
    def _():
        st_ref[...] = jnp.zeros_like(st_ref)

    row = lax.broadcasted_iota(jnp.int32, (tt, tt), 0)
    col = lax.broadcasted_iota(jnp.int32, (tt, tt), 1)
    shift = c.bit_length() - 1
    tri = (col <= row) & ((row >> shift) == (col >> shift))
    tri_b = jnp.where(tri, 1.0, 0.0).astype(BF16)
    hi, mid, lo = _split3_bf16(la_ref[...])
    bcum = _dot(tri_b, hi) + _dot(tri_b, mid) + _dot(tri_b, lo)
    causal = lax.broadcasted_iota(jnp.int32, (c, c), 1) <= lax.broadcasted_iota(jnp.int32, (c, c), 0)
    scale = dk ** -0.5
    chunks = [slice(ci * c, (ci + 1) * c) for ci in range(tt // c)]
    work = []
    for h in range(hh):
        ks = slice(h * dk, (h + 1) * dk)
        vs = slice(h * dv, (h + 1) * dv)
        qts, decays, o_intra, kvs = [], [], [], []
        for sl in chunks:
            bc = bcum[sl, ks]
            bl = bc[c - 1:c]
            qt = ((q_ref[sl, ks] * scale) * jnp.exp(bc)).astype(BF16)
            kc = k_ref[sl, ks]
            kt = (kc * jnp.exp(-bc)).astype(BF16)
            ke = (kc * jnp.exp(bl - bc)).astype(BF16)
            vb = v_ref[sl, vs].astype(BF16)
            s = lax.dot_general(qt, kt, _CONTRACT_LAST, preferred_element_type=F32)
            s = jnp.where(causal, s, 0.0)
            qts.append(qt)
            decays.append(jnp.exp(bl))
            o_intra.append(_dot(s.astype(BF16), vb))
            kvs.append(lax.dot_general(vb, ke, _CONTRACT_FIRST, preferred_element_type=F32))
        work.append((vs, qts, decays, o_intra, kvs))
    for h, (vs, qts, decays, o_intra, kvs) in enumerate(work):
        st = st_ref[h]
        for sl, qt, dec, oi, kv in zip(chunks, qts, decays, o_intra, kvs):
            o = oi + lax.dot_general(qt, st.astype(BF16), _CONTRACT_LAST, preferred_element_type=F32)
            st = st * dec + kv
            y_ref[sl, vs] = _gla_out(o, gain_ref[...], r_ref[sl, vs]).astype(y_ref.dtype)
        st_ref[h] = st
        st_out_ref[h] = st


def _gla_seq(proj3, gain, *, tt):
    bsz, seq, _ = proj3.shape
    hh = GLA_HEADS
    dv = gain.shape[0]
    dk = dv // 2
    wk, wv = hh * dk, hh * dv
    col_q = 1024 // wk
    col_k = col_q + 1
    col_v = (1024 + 2 * wk) // wv
    col_r = col_v + 1
    col_la = (1024 + 2 * wk + 2 * wv) // wk
    spec = lambda w, c0: pl.BlockSpec((None, tt, w), lambda b, t: (b, t, c0))
    body = functools.partial(_gla_seq_body, tt=tt, dk=dk, dv=dv)
    return pl.pallas_call(
        body,
        grid=(bsz, seq // tt),
        in_specs=[spec(wk, col_q), spec(wk, col_k), spec(wv, col_v), spec(wk, col_la), spec(wv, col_r),
                  pl.BlockSpec((1, dv), lambda b, t: (0, 0))],
        out_specs=(pl.BlockSpec((None, tt, wv), lambda b, t: (b, t, 0)),
                   pl.BlockSpec((None, hh, dv, dk), lambda b, t: (b, 0, 0, 0))),
        out_shape=(jax.ShapeDtypeStruct((bsz, seq, wv), BF16),
                   jax.ShapeDtypeStruct((bsz, hh, dv, dk), F32)),
        scratch_shapes=[pltpu.VMEM((hh, dv, dk), F32)],
        compiler_params=_params("arbitrary", "arbitrary"),
        name="gla_seq",
    )(proj3, proj3, proj3, proj3, proj3, gain.reshape(1, dv))


def _gla_step_body(q_ref, k_ref, v_ref, la_ref, r_ref, gain_ref, s_ref, y_ref, so_ref, *, nb, seg, dk, dv):
    hh = GLA_HEADS
    rows = SUBLANES
    per = rows // seg
    row = lax.broadcasted_iota(jnp.int32, (rows, rows), 0)
    col = lax.broadcasted_iota(jnp.int32, (rows, rows), 1)
    shift = seg.bit_length() - 1
    causal = (col <= row) & ((row >> shift) == (col >> shift))
    rid = lax.broadcasted_iota(jnp.int32, (rows, 1), 0)
    t_in = rid & (seg - 1)
    ones_b = jnp.ones((rows, dv), BF16)
    scale = dk ** -0.5

    def pair(i, carry):
        r0 = pl.multiple_of(i * rows, rows)
        prep = []
        for h in range(hh):
            ks = slice(h * dk, (h + 1) * dk)
            vs = slice(h * dv, (h + 1) * dv)
            bc = la_ref[pl.ds(r0, rows), ks]
            k = 1
            while k < seg:
                bc = bc + jnp.where(t_in >= k, pltpu.roll(bc, k, 0), 0.0)
                k *= 2
            bl = jnp.zeros_like(bc)
            for e in range(per):
                last = e * seg + seg - 1
                bl = jnp.where((rid >> shift) == e, bc[last:last + 1], bl)
            qt = ((q_ref[pl.ds(r0, rows), ks] * scale) * jnp.exp(bc)).astype(BF16)
            kc = k_ref[pl.ds(r0, rows), ks]
            kt = (kc * jnp.exp(-bc)).astype(BF16)
            ke = kc * jnp.exp(bl - bc)
            vb = v_ref[pl.ds(r0, rows), vs].astype(BF16)
            per_seq = []
            for e in range(per):
                last = e * seg + seg - 1
                dec = jnp.exp(bc[last:last + 1])
                d_hi, d_mid, d_lo = _split3_bf16(dec)
                d3 = jnp.where(rid == 0, d_hi.astype(F32),
                               jnp.where(rid == 1, d_mid.astype(F32),
                                         jnp.where(rid == 2, d_lo.astype(F32), 0.0))).astype(BF16)
                ke_e = jnp.where((rid >> shift) == e, ke, 0.0).astype(BF16)
                per_seq.append((d3, ke_e, s_ref[i * per + e, h]))
            prep.append((vs, qt, kt, vb, per_seq))
        scores = [lax.dot_general(qt, kt, _CONTRACT_LAST, preferred_element_type=F32) for _, qt, kt, _, _ in prep]
        inter, dec_cols, kvs = [], [], []
        for _, qt, _, vb, per_seq in prep:
            for d3, ke_e, st in per_seq:
                inter.append(_dot(qt, st.astype(BF16)))
                dec_cols.append(lax.dot_general(d3, ones_b, _CONTRACT_FIRST, preferred_element_type=F32))
                kvs.append(lax.dot_general(ke_e, vb, _CONTRACT_FIRST, preferred_element_type=F32))
        intra = [_dot(jnp.where(causal, s, 0.0).astype(BF16), vb) for s, (_, _, _, vb, _) in zip(scores, prep)]
        for h, (vs, _, _, _, per_seq) in enumerate(prep):
            o_inter = inter[h * per]
            for e in range(per):
                if e:
                    o_inter = jnp.where((rid >> shift) == e, inter[h * per + e], o_inter)
                so_ref[i * per + e, h] = dec_cols[h * per + e] * per_seq[e][2] + kvs[h * per + e]
            y_ref[pl.ds(r0, rows), vs] = _gla_out(intra[h] + o_inter, gain_ref[...], r_ref[pl.ds(r0, rows), vs])
        return carry

    lax.fori_loop(0, nb // per, pair, 0)


def _gla_step(proj, gain, state, *, seg, nb=8):
    n = proj.shape[0]
    bsz, hh, dk, dv = state.shape
    rows = nb * seg
    c_q = 1024 // (hh * dk)
    c_k = c_q + 1
    c_v = (1024 + 2 * hh * dk) // (hh * dv)
    c_r = c_v + 1
    c_la = (1024 + 2 * hh * dk + 2 * hh * dv) // (hh * dk)
    spec = lambda w, c0: pl.BlockSpec((rows, w), lambda i: (i, c0))
    body = functools.partial(_gla_step_body, nb=nb, seg=seg, dk=dk, dv=dv)
    return pl.pallas_call(
        body,
        grid=(bsz // nb,),
        in_specs=[spec(hh * dk, c_q), spec(hh * dk, c_k), spec(hh * dv, c_v), spec(hh * dk, c_la), spec(hh * dv, c_r),
                  pl.BlockSpec((1, dv), lambda i: (0, 0)),
                  pl.BlockSpec((nb, hh, dk, dv), lambda i: (i, 0, 0, 0))],
        out_specs=(pl.BlockSpec((rows, hh * dv), lambda i: (i, 0)),
                   pl.BlockSpec((nb, hh, dk, dv), lambda i: (i, 0, 0, 0))),
        out_shape=(jax.ShapeDtypeStruct((n, hh * dv), F32), jax.ShapeDtypeStruct(state.shape, F32)),
        compiler_params=_params("arbitrary"),
        name="gla_step",
    )(proj, proj, proj, proj, proj, gain.reshape(1, dv), state)


def _tiled_linear(name, prologue, row_ins, aux_ins, aux_specs, w, *, tm, tn, residual=None):
    n = row_ins[0].shape[0]
    assert n == tm
    k_dim, n_cols = w.shape
    nr, na = len(row_ins), len(aux_ins)
    has_res = residual is not None

    def body(*refs):
        row_refs = refs[:nr]
        aux_refs = refs[nr:nr + na]
        w_ref = refs[nr + na]
        o_ref, wb_ref, h_ref = refs[-3], refs[-2], refs[-1]

        @pl.when(pl.program_id(1) == 0)
        def _():
            prologue(row_refs, aux_refs, h_ref)

        wb = w_ref[...].astype(BF16)
        wb_ref[...] = wb
        acc = _dot(h_ref[...], wb)
        if has_res:
            acc = acc + refs[nr + na + 1][...]
        o_ref[...] = acc

    in_specs = [pl.BlockSpec((tm, a.shape[1]), lambda i, j: (i, 0)) for a in row_ins]
    in_specs += list(aux_specs)
    in_specs.append(pl.BlockSpec((k_dim, tn), lambda i, j: (0, j)))
    args = list(row_ins) + list(aux_ins) + [w]
    if has_res:
        in_specs.append(pl.BlockSpec((tm, tn), lambda i, j: (i, j)))
        args.append(residual)
    return pl.pallas_call(
        body,
        grid=(n // tm, n_cols // tn),
        in_specs=in_specs,
        out_specs=(pl.BlockSpec((tm, tn), lambda i, j: (i, j)), pl.BlockSpec((k_dim, tn), lambda i, j: (0, j))),
        out_shape=(jax.ShapeDtypeStruct((n, n_cols), F32), jax.ShapeDtypeStruct((k_dim, n_cols), BF16)),
        scratch_shapes=[pltpu.VMEM((tm, k_dim), BF16)],
        compiler_params=_params("arbitrary", "arbitrary"),
        name=name,
    )(*args)


def _whole(a):
    return pl.BlockSpec(a.shape, lambda i, j: (0,) * a.ndim)


def _norm_linear(name, x, gain, w, *, tm, tn=512):
    def prologue(row_refs, aux_refs, h_ref):
        _rms_rows_to_bf16(row_refs[0], aux_refs[0], h_ref, tm)
    g = gain.reshape(1, -1)
    return _tiled_linear(name, prologue, [x], [g], [_whole(g)], w, tm=tm, tn=tn)


def _cast_linear(name, x, w, residual, *, tm, tn=512):
    def prologue(row_refs, aux_refs, h_ref):
        def f(r0):
            h_ref[pl.ds(r0, 128), :] = row_refs[0][pl.ds(r0, 128), :].astype(BF16)
        _for_row_chunks(tm, 128, f)
    return _tiled_linear(name, prologue, [x], [], [], w, tm=tm, tn=tn, residual=residual)


def _out_proj(z_s5, y_gla, gain_s5, w_out, x, *, tm, tn=512):
    ws = z_s5.shape[1]

    def prologue(row_refs, aux_refs, h_ref):
        def f(r0):
            h_ref[pl.ds(r0, 128), 0:ws] = _rms(row_refs[0][pl.ds(r0, 128), :], aux_refs[0][...]).astype(BF16)
            h_ref[pl.ds(r0, 128), ws:] = row_refs[1][pl.ds(r0, 128), :].astype(BF16)
        _for_row_chunks(tm, 128, f)
    g = gain_s5.reshape(1, -1)
    return _tiled_linear("out_proj", prologue, [z_s5, y_gla], [g], [_whole(g)], w_out, tm=tm, tn=tn, residual=x)


def _mix_xattn_seq_body(z_ref, y_ref, xres_ref, gs5_ref, gx_ref, wout_ref, wq_ref, wo_ref, mk_ref, mv_ref,
                        o_ref, x1_ref, h_ref, q_ref, *, tm, tn, nt, hd, chunk):
    j = pl.program_id(1)
    ws = z_ref.shape[1]
    scale = hd ** -0.5

    @pl.when(j == 0)
    def _():
        def f(r0):
            h_ref[pl.ds(r0, 128), 0:ws] = _rms(z_ref[pl.ds(r0, 128), :], gs5_ref[...]).astype(BF16)
            h_ref[pl.ds(r0, 128), ws:] = y_ref[pl.ds(r0, 128), :].astype(BF16)
        _for_row_chunks(tm, 128, f)

    @pl.when(j == nt)
    def _():
        _rms_rows_to_bf16(x1_ref, gx_ref, h_ref, tm)

    @pl.when(j == 2 * nt)
    def _():
        heads = [slice(h * hd, (h + 1) * hd) for h in range(X_HEADS)]

        def f(r0):
            scores = [lax.dot_general(q_ref[pl.ds(r0, chunk), hs], mk_ref[:, hs], _CONTRACT_LAST,
                                      preferred_element_type=F32) * scale for hs in heads]
            probs = []
            for s in scores:
                e = jnp.exp(s - jnp.max(s, axis=-1, keepdims=True))
                probs.append((e / jnp.sum(e, axis=-1, keepdims=True)).astype(BF16))
            for hs, p in zip(heads, probs):
                h_ref[pl.ds(r0, chunk), hs] = _dot(p, mv_ref[:, hs]).astype(BF16)
        _for_row_chunks(tm, chunk, f)

    for jj in range(nt):
        cols = slice(jj * tn, (jj + 1) * tn)

        @pl.when(j == jj)
        def _(cols=cols):
            x1_ref[:, cols] = _dot(h_ref[...], wout_ref[...]) + xres_ref[...]

        @pl.when(j == nt + jj)
        def _(cols=cols):
            q_ref[:, cols] = _dot(h_ref[...], wq_ref[...]).astype(BF16)

        @pl.when(j == 2 * nt + jj)
        def _(cols=cols):
            o_ref[...] = _dot(h_ref[...], wo_ref[...]) + x1_ref[:, cols]


def _mix_xattn_seq(z_s5, y_gla, x, gain_s5, gain_x, w_out, w_xq, w_xo, mk_b, mv_b, *, tm, seq, tn=512, chunk=256):
    n, dm = x.shape
    nt = dm // tn
    tiles_per_seq = seq // tm
    body = functools.partial(_mix_xattn_seq_body, tm=tm, tn=tn, nt=nt, hd=dm // X_HEADS, chunk=chunk)
    gs = gain_s5.reshape(1, -1)
    gx = gain_x.reshape(1, -1)
    phase = lambda p: (lambda i, j: (0, jnp.clip(j - p * nt, 0, nt - 1)))
    mspec = pl.BlockSpec((None,) + mk_b.shape[1:], lambda i, j: (i // tiles_per_seq, 0, 0))
    return pl.pallas_call(
        body,
        grid=(n // tm, 3 * nt),
        in_specs=[
            pl.BlockSpec((tm, z_s5.shape[1]), lambda i, j: (i, 0)),
            pl.BlockSpec((tm, y_gla.shape[1]), lambda i, j: (i, 0)),
            pl.BlockSpec((tm, tn), lambda i, j: (i, jnp.minimum(j, nt - 1))),
            _whole(gs), _whole(gx),
            pl.BlockSpec((dm, tn), phase(0)), pl.BlockSpec((dm, tn), phase(1)), pl.BlockSpec((dm, tn), phase(2)),
            mspec, mspec,
        ],
        out_specs=pl.BlockSpec((tm, tn), lambda i, j: (i, jnp.clip(j - 2 * nt, 0, nt - 1))),
        out_shape=jax.ShapeDtypeStruct((n, dm), F32),
        scratch_shapes=[pltpu.VMEM((tm, dm), F32), pltpu.VMEM((tm, dm), BF16), pltpu.VMEM((tm, dm), BF16)],
        compiler_params=_params("arbitrary", "arbitrary"),
        name="mix_xattn_seq",
    )(z_s5, y_gla, x, gs, gx, w_out, w_xq, w_xo, mk_b, mv_b)


def _cache_rows(c):
    b, m, h, hd = c.shape
    n_chunk = hd // LANES
    return c.reshape(b, m, h, n_chunk, LANES).transpose(0, 1, 3, 2, 4).reshape(b * m * n_chunk * h, LANES)


def _mem_proj_body(x_ref, gain_ref, w_ref, o_ref, ob_ref, h_ref, *, tm, hd):
    j = pl.program_id(1)

    @pl.when(j == 0)
    def _():
        _rms_rows_to_bf16(x_ref, gain_ref, h_ref, tm)

    y = _dot(h_ref[...], w_ref[...].astype(BF16))
    ob_ref[...] = y.astype(BF16)
    n_chunk = hd // LANES
    pitch = n_chunk * X_HEADS
    for ck in range(n_chunk):
        o_ref[pl.ds(ck * X_HEADS + j, tm, stride=pitch), :] = y[:, ck * LANES:(ck + 1) * LANES]


def _mem_proj(name, mem, gain, w, *, tm):
    b, m, dm = mem.shape
    n = b * m
    hd = dm // X_HEADS
    n_chunk = hd // LANES
    pitch = n_chunk * X_HEADS
    body = functools.partial(_mem_proj_body, tm=tm, hd=hd)
    g = gain.reshape(1, dm)
    rows, plain = pl.pallas_call(
        body,
        grid=(n // tm, X_HEADS),
        in_specs=[pl.BlockSpec((tm, dm), lambda i, j: (i, 0)), _whole(g), pl.BlockSpec((dm, hd), lambda i, j: (0, j))],
        out_specs=(pl.BlockSpec((tm * pitch, LANES), lambda i, j: (i, 0)),
                   pl.BlockSpec((tm, hd), lambda i, j: (i, j))),
        out_shape=(jax.ShapeDtypeStruct((n * pitch, LANES), F32), jax.ShapeDtypeStruct((n, dm), BF16)),
        scratch_shapes=[pltpu.VMEM((tm, dm), BF16)],
        compiler_params=_params("arbitrary", "arbitrary"),
        name=name,
    )(mem.reshape(n, dm), g, w)
    full = rows.reshape(b, m, n_chunk, X_HEADS, LANES).transpose(0, 1, 3, 2, 4).reshape(1, b, m, X_HEADS, hd)
    return full, plain.reshape(b, m, dm)


def _xattn_step_body(q_ref, k_ref, v_ref, o_ref, *, nb, seg, hd, n_mem):
    rows = SUBLANES
    per = rows // seg
    shift = seg.bit_length() - 1
    rid = lax.broadcasted_iota(jnp.int32, (rows, 1), 0)
    scale = hd ** -0.5
    n_chunk = hd // LANES
    pitch = n_chunk * X_HEADS

    def head_rows(ref, b, h):
        parts = [ref[pl.ds(b * n_mem * pitch + ck * X_HEADS + h, n_mem, stride=pitch), :] for ck in range(n_chunk)]
        return jnp.concatenate(parts, axis=1).astype(BF16)

    pairs = [(b, h) for b in range(nb) for h in range(X_HEADS)]
    scores = []
    for b, h in pairs:
        r0 = (b // per) * rows
        q = q_ref[r0:r0 + rows, h * hd:(h + 1) * hd].astype(BF16)
        scores.append(lax.dot_general(q, head_rows(k_ref, b, h), _CONTRACT_LAST, preferred_element_type=F32) * scale)
    probs = []
    for s in scores:
        e = jnp.exp(s - jnp.max(s, axis=-1, keepdims=True))
        probs.append((e / jnp.sum(e, axis=-1, keepdims=True)).astype(BF16))
    outs = {}
    for (b, h), p in zip(pairs, probs):
        outs[b, h] = _dot(p, head_rows(v_ref, b, h))
    for g in range(nb // per):
        for h in range(X_HEADS):
            o = outs[g * per, h]
            for e in range(1, per):
                o = jnp.where((rid >> shift) == e, outs[g * per + e, h], o)
            o_ref[g * rows:(g + 1) * rows, h * hd:(h + 1) * hd] = o


def _xattn_step(q, mem_k, mem_v, *, seg, nb=4):
    n, dm = q.shape
    bsz, m, _, hd = mem_k.shape
    body = functools.partial(_xattn_step_body, nb=nb, seg=seg, hd=hd, n_mem=m)
    rows_per_seq = m * dm // LANES
    mspec = pl.BlockSpec((nb * rows_per_seq, LANES), lambda i: (i, 0))
    return pl.pallas_call(
        body,
        grid=(bsz // nb,),
        in_specs=[pl.BlockSpec((nb * seg, dm), lambda i: (i, 0)), mspec, mspec],
        out_specs=pl.BlockSpec((nb * seg, dm), lambda i: (i, 0)),
        out_shape=jax.ShapeDtypeStruct((n, dm), F32),
        compiler_params=_params("arbitrary"),
        name="xattn_step",
    )(q, _cache_rows(mem_k), _cache_rows(mem_v))


def _ffn_finish(conv, gate, wd_ref, o_ref, gfin_ref, j, nj, tm):
    act = (_gelu(conv) * gate).astype(BF16)
    o_ref[...] += _dot(act, wd_ref[...].astype(BF16))

    @pl.when(j == nj - 1)
    def _():
        def f(r0):
            o_ref[pl.ds(r0, 128), :] = _rms(o_ref[pl.ds(r0, 128), :], gfin_ref[...])
        _for_row_chunks(tm, 128, f)


def _ffn_seq_body(x_ref, gain_ref, wa_ref, wg_ref, wd_ref, cw_ref, cb_ref, gfin_ref, o_ref, tail_ref, h_ref, a_ref,
                  *, tm, halo, tiles_per_seq, nj):
    i = pl.program_id(0)
    j = pl.program_id(1)

    @pl.when(j == 0)
    def _():
        @pl.when(i % tiles_per_seq == 0)
        def _():
            h_ref[0:halo, :] = jnp.zeros((halo, h_ref.shape[1]), BF16)

        @pl.when(i % tiles_per_seq != 0)
        def _():
            h_ref[0:halo, :] = h_ref[tm:tm + halo, :]

        _rms_rows_to_bf16(x_ref, gain_ref, h_ref, tm, row0=halo)
        o_ref[...] = x_ref[...]

    a_ref[...] = _dot(h_ref[...], wa_ref[...].astype(BF16))
    gate = _dot(h_ref[halo:halo + tm, :], wg_ref[...].astype(BF16))
    conv = cb_ref[...]
    for tap in range(CONV_W):
        conv = conv + a_ref[pl.ds(halo - (CONV_W - 1) + tap, tm), :] * cw_ref[tap:tap + 1, :]
    tail_ref[...] = a_ref[pl.ds(halo + tm - SUBLANES, SUBLANES), :]
    _ffn_finish(conv, gate, wd_ref, o_ref, gfin_ref, j, nj, tm)


def _ffn_seq(x, gain, w_a, w_g, conv_w, conv_b, w_down, gain_final, *, tm, seq, tf=512, halo=16):
    n, dm = x.shape
    dff = w_down.shape[0]
    nj = dff // tf
    body = functools.partial(_ffn_seq_body, tm=tm, halo=halo, tiles_per_seq=seq // tm, nj=nj)
    return pl.pallas_call(
        body,
        grid=(n // tm, nj),
        in_specs=[
            pl.BlockSpec((tm, dm), lambda i, j: (i, 0)),
            pl.BlockSpec((1, dm), lambda i, j: (0, 0)),
            pl.BlockSpec((dm, tf), lambda i, j: (0, j)),
            pl.BlockSpec((dm, tf), lambda i, j: (0, j)),
            pl.BlockSpec((tf, dm), lambda i, j: (j, 0)),
            pl.BlockSpec((CONV_W, tf), lambda i, j: (0, j)),
            pl.BlockSpec((1, tf), lambda i, j: (0, j)),
            pl.BlockSpec((1, dm), lambda i, j: (0, 0)),
        ],
        out_specs=(pl.BlockSpec((tm, dm), lambda i, j: (i, 0)),
                   pl.BlockSpec((None, SUBLANES, tf), lambda i, j: (i, 0, j))),
        out_shape=(jax.ShapeDtypeStruct((n, dm), F32),
                   jax.ShapeDtypeStruct((n // tm, SUBLANES, dff), F32)),
        scratch_shapes=[pltpu.VMEM((halo + tm, dm), BF16), pltpu.VMEM((halo + tm, tf), F32)],
        compiler_params=_params("arbitrary", "arbitrary"),
        name="ffn_seq",
    )(x, gain.reshape(1, dm), w_a, w_g, w_down, conv_w, conv_b.reshape(1, dff), gain_final.reshape(1, dm))


def _dot_exact_rows(sel_b, x):
    hi, mid, lo = _split3_bf16(x)
    return _dot(sel_b, hi) + _dot(sel_b, mid) + _dot(sel_b, lo)


def _ffn_step_body(x_ref, gain_ref, wa_ref, wg_ref, wd_ref, cw_ref, cb_ref, gfin_ref, b0_ref, b1_ref,
                   o_ref, a_new_ref, wa_b_ref, wg_b_ref, wd_b_ref, h_ref, *, tm, seg, nj):
    j = pl.program_id(1)
    nseq = tm // seg
    shift = seg.bit_length() - 1

    @pl.when(j == 0)
    def _():
        _rms_rows_to_bf16(x_ref, gain_ref, h_ref, tm)
        o_ref[...] = x_ref[...]

    wa_b_ref[...] = wa_ref[...].astype(BF16)
    wg_b_ref[...] = wg_ref[...].astype(BF16)
    wd_b_ref[...] = wd_ref[...].astype(BF16)
    a = _dot(h_ref[...], wa_b_ref[...])
    gate = _dot(h_ref[...], wg_b_ref[...])
    t_in = lax.broadcasted_iota(jnp.int32, (tm, 1), 0) & (seg - 1)
    prev1 = jnp.where(t_in >= 1, pltpu.roll(a, 1, 0), 0.0)
    prev2 = jnp.where(t_in >= 2, pltpu.roll(a, 2, 0), 0.0)
    conv = cb_ref[...] + prev2 * cw_ref[0:1, :]
    conv = conv + prev1 * cw_ref[1:2, :]
    conv = conv + a * cw_ref[2:3, :]
    b0 = b0_ref[...]
    b1 = b1_ref[...]
    carried = jnp.concatenate([b0 * cw_ref[0:1, :] + b1 * cw_ref[1:2, :], b1 * cw_ref[0:1, :]], axis=0)
    r = lax.broadcasted_iota(jnp.int32, (tm, 2 * nseq), 0)
    c = lax.broadcasted_iota(jnp.int32, (tm, 2 * nseq), 1)
    spread = ((r >> shift) == (c & (nseq - 1))) & ((r & (seg - 1)) == (c >> (nseq.bit_length() - 1)))
    conv = conv + _dot_exact_rows(jnp.where(spread, 1.0, 0.0).astype(BF16), carried)
    r = lax.broadcasted_iota(jnp.int32, (2 * nseq, tm), 0)
    c = lax.broadcasted_iota(jnp.int32, (2 * nseq, tm), 1)
    pick = ((c >> shift) == (r & (nseq - 1))) & ((c & (seg - 1)) == seg - 2 + (r >> (nseq.bit_length() - 1)))
    picked = _dot_exact_rows(jnp.where(pick, 1.0, 0.0).astype(BF16), a)
    a_new_ref[0] = picked[0:nseq]
    a_new_ref[1] = picked[nseq:2 * nseq]
    _ffn_finish(conv, gate, wd_b_ref, o_ref, gfin_ref, j, nj, tm)


def _ffn_step(x, gain, w_up, conv_w, conv_b, w_down, gain_final, b0, b1, *, seg, tf=512):
    n, dm = x.shape
    dff = w_down.shape[0]
    nj = dff // tf
    tm = n
    nseq = n // seg
    body = functools.partial(_ffn_step_body, tm=tm, seg=seg, nj=nj)
    return pl.pallas_call(
        body,
        grid=(1, nj),
        in_specs=[
            pl.BlockSpec((tm, dm), lambda i, j: (i, 0)),
            pl.BlockSpec((1, dm), lambda i, j: (0, 0)),
            pl.BlockSpec((dm, tf), lambda i, j: (0, j)),
            pl.BlockSpec((dm, tf), lambda i, j: (0, nj + j)),
            pl.BlockSpec((tf, dm), lambda i, j: (j, 0)),
            pl.BlockSpec((CONV_W, tf), lambda i, j: (0, j)),
            pl.BlockSpec((1, tf), lambda i, j: (0, j)),
            pl.BlockSpec((1, dm), lambda i, j: (0, 0)),
            pl.BlockSpec((nseq, tf), lambda i, j: (0, j)),
            pl.BlockSpec((nseq, tf), lambda i, j: (0, j)),
        ],
        out_specs=(pl.BlockSpec((tm, dm), lambda i, j: (i, 0)),
                   pl.BlockSpec((CONV_W - 1, nseq, tf), lambda i, j: (0, 0, j)),
                   pl.BlockSpec((dm, tf), lambda i, j: (0, j)),
                   pl.BlockSpec((dm, tf), lambda i, j: (0, j)),
                   pl.BlockSpec((tf, dm), lambda i, j: (j, 0))),
        out_shape=(jax.ShapeDtypeStruct((n, dm), F32), jax.ShapeDtypeStruct((CONV_W - 1, nseq, dff), F32),
                   jax.ShapeDtypeStruct((dm, dff), BF16), jax.ShapeDtypeStruct((dm, dff), BF16),
                   jax.ShapeDtypeStruct((dff, dm), BF16)),
        scratch_shapes=[pltpu.VMEM((tm, dm), BF16)],
        compiler_params=_params("arbitrary", "arbitrary"),
        name="ffn_step",
    )(x, gain.reshape(1, dm), w_up, w_up, w_down, conv_w, conv_b.reshape(1, dff), gain_final.reshape(1, dm),
      b0, b1)


def kernel(x_prompt, x_sample, mem_prompt, cache_mem_k, cache_mem_v, state_s5_re, state_s5_im, state_gla, state_conv, norm_mix, w_in, s5_lambda_re, s5_lambda_im, s5_log_dt, s5_b_re, s5_b_im, s5_c_re, s5_c_im, s5_d, s5_w_glu, s5_b_glu, norm_s5_out, gla_w_g2, gla_b_g, norm_gla_out, w_out, norm_xattn, norm_mem, w_xq, w_xk, w_xv, w_xo, norm_ffn, w_up, conv_w, conv_b, w_down, norm_final):
    depth = w_in.shape[0]
    assert depth == 1
    bp, seq, dm = x_prompt.shape
    bs, seg, _ = x_sample.shape
    n_mem = mem_prompt.shape[1]
    dff = w_down.shape[1]
    l = 0
    s5_args = (s5_lambda_re[l], s5_lambda_im[l], s5_log_dt[l], s5_b_re[l], s5_b_im[l], s5_c_re[l], s5_c_im[l],
               s5_d[l], s5_w_glu[l], s5_b_glu[l])
    tm_p = min(1024, seq)
    n_s = bs * seg
    w_in = w_in.astype(BF16)
    s5_w, s5_powers = _s5_weights(*s5_args)
    g_total = s5_lambda_re.shape[1]
    n_state = s5_lambda_re.shape[2]

    xs = x_sample.reshape(n_s, dm)
    proj_s = _in_proj(xs, norm_mix[l], w_in[l], gla_w_g2[l], gla_b_g[l], tm=n_s)
    h0r = state_s5_re[l].reshape(1, bs, -1)
    h0i = state_s5_im[l].reshape(1, bs, -1)
    z_s5_s, hs_r, hs_i = _s5_mixer(proj_s[None], s5_w, _s5_scan_table(s5_powers, seg, 1), tt=n_s, steps=seg,
                                   h0=(h0r, h0i))
    y_gla_s, s_gla = _gla_step(proj_s, norm_gla_out[l], state_gla[l], seg=seg)
    x1s, w_out_b = _out_proj(z_s5_s[0], y_gla_s, norm_s5_out[l], w_out[l], xs, tm=n_s)
    qs, w_xq_b = _norm_linear("xattn_q_s", x1s, norm_xattn[l], w_xq[l], tm=n_s)
    att = _xattn_step(qs, cache_mem_k[l], cache_mem_v[l], seg=seg)
    x2s, w_xo_b = _cast_linear("xattn_o_s", att, w_xo[l], x1s, tm=n_s)
    buf = state_conv[l]
    ys, a_new, w_a_b, w_g_b, w_down_b = _ffn_step(x2s, norm_ffn[l], w_up[l], conv_w[l], conv_b[l], w_down[l],
                                                  norm_final, buf[:, 0], buf[:, 1], seg=seg)
    s_re = hs_r.reshape(1, bs, g_total, n_state)
    s_im = hs_i.reshape(1, bs, g_total, n_state)
    s_conv = jnp.swapaxes(a_new, 0, 1)[None]

    mk, mk_b = _mem_proj("mem_k", mem_prompt, norm_mem[l], w_xk[l], tm=min(1024, bp * n_mem))
    mv, mv_b = _mem_proj("mem_v", mem_prompt, norm_mem[l], w_xv[l], tm=min(1024, bp * n_mem))

    xp = x_prompt.reshape(bp * seq, dm)
    proj = _in_proj(xp, norm_mix[l], w_in[l], gla_w_g2[l], gla_b_g[l], tm=tm_p)
    proj3 = proj.reshape(bp, seq, -1)
    steps_p = SUBLANES // bp
    z_s5, hend_r, hend_i = _s5_mixer(proj3, s5_w, _s5_scan_table(s5_powers, steps_p, bp),
                                     tt=min(512 // bp, seq), steps=steps_p)
    y_gla, st_gla = _gla_seq(proj3, norm_gla_out[l], tt=min(512, seq))
    x2 = _mix_xattn_seq(z_s5.reshape(bp * seq, -1), y_gla.reshape(bp * seq, -1), xp, norm_s5_out[l], norm_xattn[l],
                        w_out_b, w_xq_b, w_xo_b, mk_b, mv_b, tm=tm_p, seq=seq)
    yp, tails = _ffn_seq(x2, norm_ffn[l], w_a_b, w_g_b, conv_w[l], conv_b[l], w_down_b, norm_final,
                         tm=tm_p, seq=seq)
    p_re = jnp.swapaxes(hend_r[:, SUBLANES - bp:], 0, 1).reshape(1, bp, g_total, n_state)
    p_im = jnp.swapaxes(hend_i[:, SUBLANES - bp:], 0, 1).reshape(1, bp, g_total, n_state)
    p_gla = jnp.swapaxes(st_gla, -1, -2)[None]
    tiles_per_seq = seq // tm_p
    p_conv = tails.reshape(bp, tiles_per_seq, SUBLANES, dff)[:, -1, SUBLANES - (CONV_W - 1):, :][None]

    return (yp.reshape(bp, seq, dm), ys.reshape(bs, seg, dm), mk, mv,
            p_re, p_im, p_gla, p_conv, s_re, s_im, s_gla[None], s_conv)
```

```python
import functools

import jax
import jax.numpy as jnp
from jax import lax
from jax.experimental import pallas as pl
from jax.experimental.pallas import tpu as pltpu

F32 = jnp.float32
BF16 = jnp.bfloat16

EPS = 1e-6
S5_GROUP = 16
S5_BLOCK_GROUPS = 16
GLA_HEADS = 4
GLA_RANK = 16
GLA_TAU = 16.0
GLA_CHUNK = 64
X_HEADS = 4
CONV_W = 3

SUBLANES = 8
LANES = 128
VMEM_LIMIT_BYTES = 56 * 1024 * 1024

_CONTRACT_LAST = (((1,), (1,)), ((), ()))
_CONTRACT_FIRST = (((0,), (0,)), ((), ()))


def _params(*semantics):
    return pltpu.CompilerParams(dimension_semantics=semantics, vmem_limit_bytes=VMEM_LIMIT_BYTES)


def _dot(a, b):
    return jnp.dot(a, b, preferred_element_type=F32)


def _for_row_chunks(n_rows, chunk, fn):
    def body(i, carry):
        fn(pl.multiple_of(i * chunk, chunk))
        return carry
    lax.fori_loop(0, n_rows // chunk, body, 0)


def _rms(x, gain):
    r = lax.rsqrt(jnp.mean(x * x, axis=-1, keepdims=True) + EPS)
    return (x * r) * gain


def _gelu(x):
    return 0.5 * x * (1.0 + lax.erf(x * (2.0 ** -0.5)))


def _rms_rows_to_bf16(x_ref, gain_ref, h_ref, n_rows, row0=0, chunk=128):
    def f(r0):
        h_ref[pl.ds(row0 + r0, chunk), :] = _rms(x_ref[pl.ds(r0, chunk), :], gain_ref[...]).astype(BF16)
    _for_row_chunks(n_rows, chunk, f)


def _s5_prep_body(lr_ref, li_ref, ldt_ref, br_ref, bi_ref, apr_ref, api_ref, bbr_ref, bbi_ref):
    lr = lr_ref[...]
    li = li_ref[...]
    dt = jnp.exp(ldt_ref[...])
    mag = jnp.exp(lr * dt)
    ar = mag * jnp.cos(li * dt)
    ai = mag * jnp.sin(li * dt)
    den = lr * lr + li * li
    pr = ar - 1.0
    qr = (pr * lr + ai * li) / den
    qi = (ai * lr - pr * li) / den
    for p in range(S5_GROUP):
        br = br_ref[p]
        bi = bi_ref[p]
        bbr_ref[p] = qr * br - qi * bi
        bbi_ref[p] = qr * bi + qi * br
    cr, ci = ar, ai
    apr_ref[0] = cr
    api_ref[0] = ci
    for k in range(1, SUBLANES):
        cr, ci = cr * ar - ci * ai, cr * ai + ci * ar
        apr_ref[k] = cr
        api_ref[k] = ci


def _s5_prep(lam_re, lam_im, log_dt, b_re, b_im):
    g, n = lam_re.shape
    out = jax.ShapeDtypeStruct((SUBLANES, g, n), F32)
    outb = jax.ShapeDtypeStruct((S5_GROUP, g, n), F32)
    return pl.pallas_call(
        _s5_prep_body, out_shape=(out, out, outb, outb), name="s5_prep",
    )(lam_re, lam_im, log_dt.reshape(g, 1), jnp.transpose(b_re, (2, 0, 1)), jnp.transpose(b_im, (2, 0, 1)))


def _s5_weights(lam_re, lam_im, log_dt, b_re, b_im, c_re, c_im, d, w_glu, b_glu):
    g, n = lam_re.shape
    p = S5_GROUP
    gb = S5_BLOCK_GROUPS
    nb = g // gb
    apr, api, bbr, bbi = _s5_prep(lam_re, lam_im, log_dt, b_re, b_im)

    def same_group(rows, row_w, cols, col_w):
        return (jnp.arange(rows)[:, None] // row_w) == (jnp.arange(cols)[None, :] // col_w)

    def tile_cols(x, reps):
        w = x.shape[-1]
        sel = (jnp.arange(reps * w)[None, :] % w == jnp.arange(w)[:, None]).astype(BF16)
        return jnp.einsum('brw,wj->brj', x, sel, preferred_element_type=F32)

    def tile_rows(x, reps):
        h = x.shape[1]
        sel = (jnp.arange(reps * h)[:, None] % h == jnp.arange(h)[None, :]).astype(BF16)
        return jnp.einsum('rh,bhc->brc', sel, x, preferred_element_type=F32)

    def bu_block(bb):
        per_block = jnp.transpose(bb.reshape(p, nb, gb * n), (1, 0, 2)).astype(BF16)
        return jnp.where(same_group(gb * p, p, gb * n, n), tile_rows(per_block, gb), 0.0)

    def c_block(c):
        per_block = jnp.transpose(c.reshape(nb, gb, p, n), (0, 1, 3, 2)).reshape(nb, gb * n, p).astype(BF16)
        return jnp.where(same_group(gb * n, n, gb * p, p), tile_cols(per_block, gb), 0.0)

    w_bu = jnp.concatenate([bu_block(bbr), bu_block(bbi)], axis=-1).astype(BF16)
    w_c = jnp.concatenate([c_block(c_re), -c_block(c_im)], axis=1).astype(BF16)
    w_g = jnp.where(same_group(gb * p, p, gb * p, p),
                    tile_cols(w_glu.reshape(nb, gb * p, p).astype(BF16), gb), 0.0).astype(BF16)
    d_b = d.reshape(nb, 1, gb * p)
    bg_b = b_glu.reshape(nb, 1, gb * p)
    pw_r = apr.reshape(SUBLANES, nb, gb * n)
    pw_i = api.reshape(SUBLANES, nb, gb * n)
    return (w_bu, w_c, w_g, d_b, bg_b), (pw_r, pw_i)


def _s5_scan_table(powers, steps, rows_per_step):
    pw_r, pw_i = powers
    t = (jnp.arange(SUBLANES) // rows_per_step) % steps

    def masked(k):
        m = (t >= k)[:, None, None]
        return jnp.where(m, pw_r[k - 1][None], 0.0), jnp.where(m, pw_i[k - 1][None], 0.0)

    tabs = []
    k = 1
    while k < steps:
        tabs.extend(masked(k))
        k *= 2
    tabs.append(pw_r[t])
    tabs.append(pw_i[t])
    whole = jnp.full((SUBLANES,), steps - 1)
    tabs.append(pw_r[whole])
    tabs.append(pw_i[whole])
    return jnp.transpose(jnp.stack(tabs), (2, 0, 1, 3))


def _in_proj_body(x_ref, gain_ref, w_ref, wr_ref, wg_ref, wg2_ref, bg_ref, o_ref, h_ref, *, tm, n_main, n_r):
    j = pl.program_id(1)

    @pl.when(j == 0)
    def _():
        _rms_rows_to_bf16(x_ref, gain_ref, h_ref, tm)

    @pl.when(j < n_main)
    def _():
        o_ref[...] = _dot(h_ref[...], w_ref[...].astype(BF16))

    @pl.when((j >= n_main) & (j < n_main + n_r))
    def _():
        o_ref[...] = _dot(h_ref[...], wr_ref[...].astype(BF16))

    @pl.when(j == n_main + n_r)
    def _():
        g_low = _dot(h_ref[...], wg_ref[...].astype(BF16))
        pre = _dot(g_low.astype(BF16), wg2_ref[...].astype(BF16)) + bg_ref[...]
        log_sig = jnp.minimum(pre, 0.0) - jnp.log1p(jnp.exp(-jnp.abs(pre)))
        o_ref[...] = log_sig * (1.0 / GLA_TAU)


def _in_proj(x, gain, w_in, w_g2, b_g, *, tm, tn=512):
    n, dm = x.shape
    mix = dm
    n_u_q_k_v = mix // 2 + 2 * (mix // 4) + mix // 2
    col_g = n_u_q_k_v
    col_r = n_u_q_k_v + GLA_RANK
    w_r = w_in[:, col_r:]
    w_g = w_in[:, col_g:col_r]
    n_main = n_u_q_k_v // tn
    n_r = w_r.shape[1] // tn
    n_cols = n_u_q_k_v + w_r.shape[1] + w_g2.shape[1]
    body = functools.partial(_in_proj_body, tm=tm, n_main=n_main, n_r=n_r)
    return pl.pallas_call(
        body,
        grid=(n // tm, n_main + n_r + 1),
        in_specs=[
            pl.BlockSpec((tm, dm), lambda i, j: (i, 0)),
            pl.BlockSpec((1, dm), lambda i, j: (0, 0)),
            pl.BlockSpec((dm, tn), lambda i, j: (0, jnp.minimum(j, n_main - 1))),
            pl.BlockSpec((dm, tn), lambda i, j: (0, jnp.clip(j - n_main, 0, n_r - 1))),
            pl.BlockSpec((dm, GLA_RANK), lambda i, j: (0, 0)),
            pl.BlockSpec(w_g2.shape, lambda i, j: (0, 0)),
            pl.BlockSpec((1, w_g2.shape[1]), lambda i, j: (0, 0)),
        ],
        out_specs=pl.BlockSpec((tm, tn), lambda i, j: (i, j)),
        out_shape=jax.ShapeDtypeStruct((n, n_cols), F32),
        scratch_shapes=[pltpu.VMEM((tm, dm), BF16)],
        compiler_params=_params("arbitrary", "arbitrary"),
        name="in_proj",
    )(x, gain.reshape(1, dm), w_in, w_r, w_g, w_g2, b_g.reshape(1, -1))


def _s5_body(*refs, tt, steps, nseq, ns, chunk):
    carried = nseq > 1
    if carried:
        (u_ref, wbu_ref, wc_ref, wg_ref, d_ref, bg_ref, tab_ref, z_ref, hr_ref, hi_ref,
         bu_ref, cr_ref, ci_ref, us_ref, zs_ref) = refs
    else:
        u_ref, wbu_ref, wc_ref, wg_ref, d_ref, bg_ref, tab_ref, h0r_ref, h0i_ref, z_ref, hr_ref, hi_ref, bu_ref = refs
    seg = steps
    n_lvl = steps.bit_length() - 1
    lane_group = 2 * LANES
    rid = lax.broadcasted_iota(jnp.int32, (SUBLANES, 1), 0)
    cw = u_ref.shape[-1]
    n_lt = cw // LANES

    if carried:
        @pl.when(pl.program_id(1) == 0)
        def _():
            cr_ref[...] = jnp.zeros_like(cr_ref)
            ci_ref[...] = jnp.zeros_like(ci_ref)

        per = chunk // nseq

    def chunk_u(c):
        if not carried:
            return u_ref[c * chunk:(c + 1) * chunk, :]
        for b in range(nseq):
            for lt in range(n_lt):
                us_ref[c * n_lt + lt, pl.ds(b, per, stride=nseq), :] = (
                    u_ref[b, c * per:(c + 1) * per, lt * LANES:(lt + 1) * LANES])
        return jnp.concatenate([us_ref[c * n_lt + lt] for lt in range(n_lt)], axis=1)

    def project_in(c):
        rows = slice(c * chunk, (c + 1) * chunk)
        bu = _dot(chunk_u(c).astype(BF16), wbu_ref[...])
        if not carried:
            seqs = slice(c * chunk // seg, (c + 1) * chunk // seg)
            a_r = tab_ref[2 * n_lvl, 0:1, :]
            a_i = tab_ref[2 * n_lvl + 1, 0:1, :]
            h0r = h0r_ref[seqs, :]
            h0i = h0i_ref[seqs, :]
            first = (lax.broadcasted_iota(jnp.int32, (chunk, chunk // seg), 0)
                     == seg * lax.broadcasted_iota(jnp.int32, (chunk, chunk // seg), 1))
            first_b = jnp.where(first, 1.0, 0.0).astype(BF16)
            bu = bu + jnp.concatenate([_dot_exact_rows(first_b, a_r * h0r - a_i * h0i),
                                       _dot_exact_rows(first_b, a_r * h0i + a_i * h0r)], axis=1)
        bu_ref[rows, :] = bu

    def scan(c, carry):
        carry = list(carry)
        for blk in range(chunk // SUBLANES):
            rows = slice(c * chunk + blk * SUBLANES, c * chunk + (blk + 1) * SUBLANES)
            for gi in range(ns // lane_group):
                cols = slice(gi * lane_group, (gi + 1) * lane_group)
                cols_im = slice(ns + gi * lane_group, ns + (gi + 1) * lane_group)
                xr = bu_ref[rows, cols]
                xi = bu_ref[rows, cols_im]
                for lvl in range(n_lvl):
                    k = 1 << lvl
                    ar = tab_ref[2 * lvl, :, cols]
                    ai = tab_ref[2 * lvl + 1, :, cols]
                    sr = pltpu.roll(xr, k * nseq, 0)
                    si = pltpu.roll(xi, k * nseq, 0)
                    xr, xi = xr + (ar * sr - ai * si), xi + (ar * si + ai * sr)
                if carried:
                    pr = tab_ref[2 * n_lvl, :, cols]
                    pi = tab_ref[2 * n_lvl + 1, :, cols]
                    c_r, c_i = carry[gi]
                    hr = xr + (pr * c_r - pi * c_i)
                    hi = xi + (pr * c_i + pi * c_r)
                    lr, li = xr, xi
                    half = SUBLANES // 2
                    while half >= nseq:
                        keep = (rid & (2 * half - 1)) >= half
                        lr = jnp.where(keep, lr, pltpu.roll(lr, SUBLANES - half, 0))
                        li = jnp.where(keep, li, pltpu.roll(li, SUBLANES - half, 0))
                        half //= 2
                    er = tab_ref[2 * n_lvl + 2, :, cols]
                    ei = tab_ref[2 * n_lvl + 3, :, cols]
                    carry[gi] = (lr + (er * c_r - ei * c_i), li + (er * c_i + ei * c_r))
                else:
                    hr, hi = xr, xi
                    for e in range(SUBLANES // seg):
                        src = e * seg + seg - 1
                        dst = (c * chunk + blk * SUBLANES) // seg + e
                        hr_ref[dst:dst + 1, cols] = hr[src:src + 1]
                        hi_ref[dst:dst + 1, cols] = hi[src:src + 1]
                bu_ref[rows, cols] = hr
                bu_ref[rows, cols_im] = hi
        return carry

    def project_out(c):
        rows = slice(c * chunk, (c + 1) * chunk)
        if carried:
            u = jnp.concatenate([us_ref[c * n_lt + lt] for lt in range(n_lt)], axis=1)
        else:
            u = u_ref[rows, :]
        y = _dot(bu_ref[rows, :].astype(BF16), wc_ref[...]) + d_ref[...] * u
        z = _gelu(y)
        z = z * jax.nn.sigmoid(_dot(z.astype(BF16), wg_ref[...]) + bg_ref[...])
        if carried:
            for lt in range(n_lt):
                zs_ref[c * n_lt + lt] = z[:, lt * LANES:(lt + 1) * LANES]
            for b in range(nseq):
                for lt in range(n_lt):
                    z_ref[b, c * per:(c + 1) * per, lt * LANES:(lt + 1) * LANES] = (
                        zs_ref[c * n_lt + lt, pl.ds(b, per, stride=nseq), :])
        else:
            z_ref[rows, :] = z

    n_chunks = tt // chunk
    n_groups = ns // lane_group
    group_cols = [slice(gi * lane_group, (gi + 1) * lane_group) for gi in range(n_groups)]
    carry = [(cr_ref[:, cols], ci_ref[:, cols]) for cols in group_cols] if carried else [None] * n_groups
    project_in(0)
    for c in range(n_chunks):
        if c + 1 < n_chunks:
            project_in(c + 1)
        carry = scan(c, carry)
        project_out(c)
    if carried:
        for cols, (c_r, c_i) in zip(group_cols, carry):
            cr_ref[:, cols] = c_r
            ci_ref[:, cols] = c_i
            hr_ref[:, cols] = c_r
            hi_ref[:, cols] = c_i


def _s5_mixer(proj3, weights, tab, *, tt, steps, h0=None, chunk=128):
    w_bu, w_c, w_g, d_b, bg_b = weights
    bsz, seq, _ = proj3.shape
    nb, cw, ns2 = w_bu.shape
    ns = ns2 // 2
    n_tab = tab.shape[1]
    carried = h0 is None
    nseq = bsz if carried else 1
    assert nseq * steps == SUBLANES or not carried
    rows = nseq * tt
    body = functools.partial(_s5_body, tt=rows, steps=steps, nseq=nseq, ns=ns, chunk=min(chunk, rows))
    wspec = lambda shape: pl.BlockSpec((None,) + shape, lambda g, t: (g,) + (0,) * len(shape))
    lead = bsz if carried else None
    uz_spec = pl.BlockSpec((lead, tt, cw), lambda g, t: (0, t, g))
    in_specs = [
        uz_spec,
        wspec((cw, ns2)), wspec((ns2, cw)), wspec((cw, cw)), wspec((1, cw)), wspec((1, cw)),
        wspec((n_tab, SUBLANES, ns)),
    ]
    args = [proj3, w_bu, w_c, w_g, d_b, bg_b, tab]
    z_shape = jax.ShapeDtypeStruct((bsz, seq, nb * cw), F32)
    scratch = [pltpu.VMEM((rows, ns2), F32)]
    if carried:
        st_shape = jax.ShapeDtypeStruct((nb, SUBLANES, ns), F32)
        st_spec = pl.BlockSpec((None, SUBLANES, ns), lambda g, t: (g, 0, 0))
        slabs = (rows // min(chunk, rows)) * (cw // LANES)
        scratch += [pltpu.VMEM((SUBLANES, ns), F32), pltpu.VMEM((SUBLANES, ns), F32),
                    pltpu.VMEM((slabs, min(chunk, rows), LANES), F32), pltpu.VMEM((slabs, min(chunk, rows), LANES), F32)]
    else:
        assert bsz == 1
        h0r, h0i = h0
        st_spec = pl.BlockSpec((None, tt // steps, ns), lambda g, t: (0, t, g))
        in_specs += [st_spec] * 2
        args += [h0r, h0i]
        st_shape = jax.ShapeDtypeStruct((bsz, seq // steps, nb * ns), F32)
    return pl.pallas_call(
        body,
        grid=(nb, seq // tt),
        in_specs=in_specs,
        out_specs=(uz_spec, st_spec, st_spec),
        out_shape=(z_shape, st_shape, st_shape),
        scratch_shapes=scratch,
        compiler_params=_params("arbitrary", "arbitrary"),
        name="s5_carried" if carried else "s5_segments",
    )(*args)


def _split3_bf16(x):
    hi = x.astype(BF16)
    r1 = x - hi.astype(F32)
    mid = r1.astype(BF16)
    lo = (r1 - mid.astype(F32)).astype(BF16)
    return hi, mid, lo


def _gla_out(o, gain, r):
    return _rms(o, gain) * (r * jax.nn.sigmoid(r))


def _gla_seq_body(q_ref, k_ref, v_ref, la_ref, r_ref, gain_ref, y_ref, st_out_ref, st_ref, *, tt, dk, dv):
    c = GLA_CHUNK
    hh = GLA_HEADS

    @pl.when(pl.program_id(1) == 0)
    def _():
        st_ref[...] = jnp.zeros_like(st_ref)

    row = lax.broadcasted_iota(jnp.int32, (tt, tt), 0)
    col = lax.broadcasted_iota(jnp.int32, (tt, tt), 1)
    shift = c.bit_length() - 1
    tri = (col <= row) & ((row >> shift) == (col >> shift))
    tri_b = jnp.where(tri, 1.0, 0.0).astype(BF16)
    hi, mid, lo = _split3_bf16(la_ref[...])
    bcum = _dot(tri_b, hi) + _dot(tri_b, mid) + _dot(tri_b, lo)
    causal = lax.broadcasted_iota(jnp.int32, (c, c), 1) <= lax.broadcasted_iota(jnp.int32, (c, c), 0)
    scale = dk ** -0.5
    chunks = [slice(ci * c, (ci + 1) * c) for ci in range(tt // c)]
    work = []
    for h in range(hh):
        ks = slice(h * dk, (h + 1) * dk)
        vs = slice(h * dv, (h + 1) * dv)
        qts, decays, o_intra, kvs = [], [], [], []
        for sl in chunks:
            bc = bcum[sl, ks]
            bl = bc[c - 1:c]
            qt = ((q_ref[sl, ks] * scale) * jnp.exp(bc)).astype(BF16)
            kc = k_ref[sl, ks]
            kt = (kc * jnp.exp(-bc)).astype(BF16)
            ke = (kc * jnp.exp(bl - bc)).astype(BF16)
            vb = v_ref[sl, vs].astype(BF16)
            s = lax.dot_general(qt, kt, _CONTRACT_LAST, preferred_element_type=F32)
            s = jnp.where(causal, s, 0.0)
            qts.append(qt)
            decays.append(jnp.exp(bl))
            o_intra.append(_dot(s.astype(BF16), vb))
            kvs.append(lax.dot_general(vb, ke, _CONTRACT_FIRST, preferred_element_type=F32))
        work.append((vs, qts, decays, o_intra, kvs))
    for h, (vs, qts, decays, o_intra, kvs) in enumerate(work):
        st = st_ref[h]
        for sl, qt, dec, oi, kv in zip(chunks, qts, decays, o_intra, kvs):
            o = oi + lax.dot_general(qt, st.astype(BF16), _CONTRACT_LAST, preferred_element_type=F32)
            st = st * dec + kv
            y_ref[sl, vs] = _gla_out(o, gain_ref[...], r_ref[sl, vs]).astype(y_ref.dtype)
        st_ref[h] = st
        st_out_ref[h] = st


def _gla_seq(proj3, gain, *, tt):
    bsz, seq, _ = proj3.shape
    hh = GLA_HEADS
    dv = gain.shape[0]
    dk = dv // 2
    wk, wv = hh * dk, hh * dv
    col_q = 1024 // wk
    col_k = col_q + 1
    col_v = (1024 + 2 * wk) // wv
    col_r = col_v + 1
    col_la = (1024 + 2 * wk + 2 * wv) // wk
    spec = lambda w, c0: pl.BlockSpec((None, tt, w), lambda b, t: (b, t, c0))
    body = functools.partial(_gla_seq_body, tt=tt, dk=dk, dv=dv)
    return pl.pallas_call(
        body,
        grid=(bsz, seq // tt),
        in_specs=[spec(wk, col_q), spec(wk, col_k), spec(wv, col_v), spec(wk, col_la), spec(wv, col_r),
                  pl.BlockSpec((1, dv), lambda b, t: (0, 0))],
        out_specs=(pl.BlockSpec((None, tt, wv), lambda b, t: (b, t, 0)),
                   pl.BlockSpec((None, hh, dv, dk), lambda b, t: (b, 0, 0, 0))),
        out_shape=(jax.ShapeDtypeStruct((bsz, seq, wv), BF16),
                   jax.ShapeDtypeStruct((bsz, hh, dv, dk), F32)),
        scratch_shapes=[pltpu.VMEM((hh, dv, dk), F32)],
        compiler_params=_params("arbitrary", "arbitrary"),
        name="gla_seq",
    )(proj3, proj3, proj3, proj3, proj3, gain.reshape(1, dv))


def _gla_step_body(q_ref, k_ref, v_ref, la_ref, r_ref, gain_ref, s_ref, y_ref, so_ref, *, nb, seg, dk, dv):
    hh = GLA_HEADS
    rows = SUBLANES
    per = rows // seg
    row = lax.broadcasted_iota(jnp.int32, (rows, rows), 0)
    col = lax.broadcasted_iota(jnp.int32, (rows, rows), 1)
    shift = seg.bit_length() - 1
    causal = (col <= row) & ((row >> shift) == (col >> shift))
    rid = lax.broadcasted_iota(jnp.int32, (rows, 1), 0)
    t_in = rid & (seg - 1)
    ones_b = jnp.ones((rows, dv), BF16)
    scale = dk ** -0.5

    def pair(i, carry):
        r0 = pl.multiple_of(i * rows, rows)
        prep = []
        for h in range(hh):
            ks = slice(h * dk, (h + 1) * dk)
            vs = slice(h * dv, (h + 1) * dv)
            bc = la_ref[pl.ds(r0, rows), ks]
            k = 1
            while k < seg:
                bc = bc + jnp.where(t_in >= k, pltpu.roll(bc, k, 0), 0.0)
                k *= 2
            bl = jnp.zeros_like(bc)
            for e in range(per):
                last = e * seg + seg - 1
                bl = jnp.where((rid >> shift) == e, bc[last:last + 1], bl)
            qt = ((q_ref[pl.ds(r0, rows), ks] * scale) * jnp.exp(bc)).astype(BF16)
            kc = k_ref[pl.ds(r0, rows), ks]
            kt = (kc * jnp.exp(-bc)).astype(BF16)
            ke = kc * jnp.exp(bl - bc)
            vb = v_ref[pl.ds(r0, rows), vs].astype(BF16)
            per_seq = []
            for e in range(per):
                last = e * seg + seg - 1
                dec = jnp.exp(bc[last:last + 1])
                d_hi, d_mid, d_lo = _split3_bf16(dec)
                d3 = jnp.where(rid == 0, d_hi.astype(F32),
                               jnp.where(rid == 1, d_mid.astype(F32),
                                         jnp.where(rid == 2, d_lo.astype(F32), 0.0))).astype(BF16)
                ke_e = jnp.where((rid >> shift) == e, ke, 0.0).astype(BF16)
                per_seq.append((d3, ke_e, s_ref[i * per + e, h]))
            prep.append((vs, qt, kt, vb, per_seq))
        scores = [lax.dot_general(qt, kt, _CONTRACT_LAST, preferred_element_type=F32) for _, qt, kt, _, _ in prep]
        inter, dec_cols, kvs = [], [], []
        for _, qt, _, vb, per_seq in prep:
            for d3, ke_e, st in per_seq:
                inter.append(_dot(qt, st.astype(BF16)))
                dec_cols.append(lax.dot_general(d3, ones_b, _CONTRACT_FIRST, preferred_element_type=F32))
                kvs.append(lax.dot_general(ke_e, vb, _CONTRACT_FIRST, preferred_element_type=F32))
        intra = [_dot(jnp.where(causal, s, 0.0).astype(BF16), vb) for s, (_, _, _, vb, _) in zip(scores, prep)]
        for h, (vs, _, _, _, per_seq) in enumerate(prep):
            o_inter = inter[h * per]
            for e in range(per):
                if e:
                    o_inter = jnp.where((rid >> shift) == e, inter[h * per + e], o_inter)
                so_ref[i * per + e, h] = dec_cols[h * per + e] * per_seq[e][2] + kvs[h * per + e]
            y_ref[pl.ds(r0, rows), vs] = _gla_out(intra[h] + o_inter, gain_ref[...], r_ref[pl.ds(r0, rows), vs])
        return carry

    lax.fori_loop(0, nb // per, pair, 0)


def _gla_step(proj, gain, state, *, seg, nb=8):
    n = proj.shape[0]
    bsz, hh, dk, dv = state.shape
    rows = nb * seg
    c_q = 1024 // (hh * dk)
    c_k = c_q + 1
    c_v = (1024 + 2 * hh * dk) // (hh * dv)
    c_r = c_v + 1
    c_la = (1024 + 2 * hh * dk + 2 * hh * dv) // (hh * dk)
    spec = lambda w, c0: pl.BlockSpec((rows, w), lambda i: (i, c0))
    body = functools.partial(_gla_step_body, nb=nb, seg=seg, dk=dk, dv=dv)
    return pl.pallas_call(
        body,
        grid=(bsz // nb,),
        in_specs=[spec(hh * dk, c_q), spec(hh * dk, c_k), spec(hh * dv, c_v), spec(hh * dk, c_la), spec(hh * dv, c_r),
                  pl.BlockSpec((1, dv), lambda i: (0, 0)),
                  pl.BlockSpec((nb, hh, dk, dv), lambda i: (i, 0, 0, 0))],
        out_specs=(pl.BlockSpec((rows, hh * dv), lambda i: (i, 0)),
                   pl.BlockSpec((nb, hh, dk, dv), lambda i: (i, 0, 0, 0))),
        out_shape=(jax.ShapeDtypeStruct((n, hh * dv), F32), jax.ShapeDtypeStruct(state.shape, F32)),
        compiler_params=_params("arbitrary"),
        name="gla_step",
    )(proj, proj, proj, proj, proj, gain.reshape(1, dv), state)


def _tiled_linear(name, prologue, row_ins, aux_ins, aux_specs, w, *, tm, tn, residual=None):
    n = row_ins[0].shape[0]
    assert n == tm
    k_dim, n_cols = w.shape
    nr, na = len(row_ins), len(aux_ins)
    has_res = residual is not None

    def body(*refs):
        row_refs = refs[:nr]
        aux_refs = refs[nr:nr + na]
        w_ref = refs[nr + na]
        o_ref, wb_ref, h_ref = refs[-3], refs[-2], refs[-1]

        @pl.when(pl.program_id(1) == 0)
        def _():
            prologue(row_refs, aux_refs, h_ref)

        wb = w_ref[...].astype(BF16)
        wb_ref[...] = wb
        acc = _dot(h_ref[...], wb)
        if has_res:
            acc = acc + refs[nr + na + 1][...]
        o_ref[...] = acc

    in_specs = [pl.BlockSpec((tm, a.shape[1]), lambda i, j: (i, 0)) for a in row_ins]
    in_specs += list(aux_specs)
    in_specs.append(pl.BlockSpec((k_dim, tn), lambda i, j: (0, j)))
    args = list(row_ins) + list(aux_ins) + [w]
    if has_res:
        in_specs.append(pl.BlockSpec((tm, tn), lambda i, j: (i, j)))
        args.append(residual)
    return pl.pallas_call(
        body,
        grid=(n // tm, n_cols // tn),
        in_specs=in_specs,
        out_specs=(pl.BlockSpec((tm, tn), lambda i, j: (i, j)), pl.BlockSpec((k_dim, tn), lambda i, j: (0, j))),
        out_shape=(jax.ShapeDtypeStruct((n, n_cols), F32), jax.ShapeDtypeStruct((k_dim, n_cols), BF16)),
        scratch_shapes=[pltpu.VMEM((tm, k_dim), BF16)],
        compiler_params=_params("arbitrary", "arbitrary"),
        name=name,
    )(*args)


def _whole(a):
    return pl.BlockSpec(a.shape, lambda i, j: (0,) * a.ndim)


def _norm_linear(name, x, gain, w, *, tm, tn=512):
    def prologue(row_refs, aux_refs, h_ref):
        _rms_rows_to_bf16(row_refs[0], aux_refs[0], h_ref, tm)
    g = gain.reshape(1, -1)
    return _tiled_linear(name, prologue, [x], [g], [_whole(g)], w, tm=tm, tn=tn)


def _cast_linear(name, x, w, residual, *, tm, tn=512):
    def prologue(row_refs, aux_refs, h_ref):
        def f(r0):
            h_ref[pl.ds(r0, 128), :] = row_refs[0][pl.ds(r0, 128), :].astype(BF16)
        _for_row_chunks(tm, 128, f)
    return _tiled_linear(name, prologue, [x], [], [], w, tm=tm, tn=tn, residual=residual)


def _out_proj(z_s5, y_gla, gain_s5, w_out, x, *, tm, tn=512):
    ws = z_s5.shape[1]

    def prologue(row_refs, aux_refs, h_ref):
        def f(r0):
            h_ref[pl.ds(r0, 128), 0:ws] = _rms(row_refs[0][pl.ds(r0, 128), :], aux_refs[0][...]).astype(BF16)
            h_ref[pl.ds(r0, 128), ws:] = row_refs[1][pl.ds(r0, 128), :].astype(BF16)
        _for_row_chunks(tm, 128, f)
    g = gain_s5.reshape(1, -1)
    return _tiled_linear("out_proj", prologue, [z_s5, y_gla], [g], [_whole(g)], w_out, tm=tm, tn=tn, residual=x)


def _mix_xattn_seq_body(z_ref, y_ref, xres_ref, gs5_ref, gx_ref, wout_ref, wq_ref, wo_ref, mk_ref, mv_ref,
                        o_ref, x1_ref, h_ref, q_ref, *, tm, tn, nt, hd, chunk):
    j = pl.program_id(1)
    ws = z_ref.shape[1]
    scale = hd ** -0.5

    @pl.when(j == 0)
    def _():
        def f(r0):
            h_ref[pl.ds(r0, 128), 0:ws] = _rms(z_ref[pl.ds(r0, 128), :], gs5_ref[...]).astype(BF16)
            h_ref[pl.ds(r0, 128), ws:] = y_ref[pl.ds(r0, 128), :].astype(BF16)
        _for_row_chunks(tm, 128, f)

    @pl.when(j == nt)
    def _():
        _rms_rows_to_bf16(x1_ref, gx_ref, h_ref, tm)

    @pl.when(j == 2 * nt)
    def _():
        heads = [slice(h * hd, (h + 1) * hd) for h in range(X_HEADS)]

        def f(r0):
            scores = [lax.dot_general(q_ref[pl.ds(r0, chunk), hs], mk_ref[:, hs], _CONTRACT_LAST,
                                      preferred_element_type=F32) * scale for hs in heads]
            probs = []
            for s in scores:
                e = jnp.exp(s - jnp.max(s, axis=-1, keepdims=True))
                probs.append((e / jnp.sum(e, axis=-1, keepdims=True)).astype(BF16))
            for hs, p in zip(heads, probs):
                h_ref[pl.ds(r0, chunk), hs] = _dot(p, mv_ref[:, hs]).astype(BF16)
        _for_row_chunks(tm, chunk, f)

    for jj in range(nt):
        cols = slice(jj * tn, (jj + 1) * tn)

        @pl.when(j == jj)
        def _(cols=cols):
            x1_ref[:, cols] = _dot(h_ref[...], wout_ref[...]) + xres_ref[...]

        @pl.when(j == nt + jj)
        def _(cols=cols):
            q_ref[:, cols] = _dot(h_ref[...], wq_ref[...]).astype(BF16)

        @pl.when(j == 2 * nt + jj)
        def _(cols=cols):
            o_ref[...] = _dot(h_ref[...], wo_ref[...]) + x1_ref[:, cols]


def _mix_xattn_seq(z_s5, y_gla, x, gain_s5, gain_x, w_out, w_xq, w_xo, mk_b, mv_b, *, tm, seq, tn=512, chunk=256):
    n, dm = x.shape
    nt = dm // tn
    tiles_per_seq = seq // tm
    body = functools.partial(_mix_xattn_seq_body, tm=tm, tn=tn, nt=nt, hd=dm // X_HEADS, chunk=chunk)
    gs = gain_s5.reshape(1, -1)
    gx = gain_x.reshape(1, -1)
    phase = lambda p: (lambda i, j: (0, jnp.clip(j - p * nt, 0, nt - 1)))
    mspec = pl.BlockSpec((None,) + mk_b.shape[1:], lambda i, j: (i // tiles_per_seq, 0, 0))
    return pl.pallas_call(
        body,
        grid=(n // tm, 3 * nt),
        in_specs=[
            pl.BlockSpec((tm, z_s5.shape[1]), lambda i, j: (i, 0)),
            pl.BlockSpec((tm, y_gla.shape[1]), lambda i, j: (i, 0)),
            pl.BlockSpec((tm, tn), lambda i, j: (i, jnp.minimum(j, nt - 1))),
            _whole(gs), _whole(gx),
            pl.BlockSpec((dm, tn), phase(0)), pl.BlockSpec((dm, tn), phase(1)), pl.BlockSpec((dm, tn), phase(2)),
            mspec, mspec,
        ],
        out_specs=pl.BlockSpec((tm, tn), lambda i, j: (i, jnp.clip(j - 2 * nt, 0, nt - 1))),
        out_shape=jax.ShapeDtypeStruct((n, dm), F32),
        scratch_shapes=[pltpu.VMEM((tm, dm), F32), pltpu.VMEM((tm, dm), BF16), pltpu.VMEM((tm, dm), BF16)],
        compiler_params=_params("arbitrary", "arbitrary"),
        name="mix_xattn_seq",
    )(z_s5, y_gla, x, gs, gx, w_out, w_xq, w_xo, mk_b, mv_b)


def _cache_rows(c):
    b, m, h, hd = c.shape
    n_chunk = hd // LANES
    return c.reshape(b, m, h, n_chunk, LANES).transpose(0, 1, 3, 2, 4).reshape(b * m * n_chunk * h, LANES)


def _mem_proj_body(x_ref, gain_ref, w_ref, o_ref, ob_ref, h_ref, *, tm, hd):
    j = pl.program_id(1)

    @pl.when(j == 0)
    def _():
        _rms_rows_to_bf16(x_ref, gain_ref, h_ref, tm)

    y = _dot(h_ref[...], w_ref[...].astype(BF16))
    ob_ref[...] = y.astype(BF16)
    n_chunk = hd // LANES
    pitch = n_chunk * X_HEADS
    for ck in range(n_chunk):
        o_ref[pl.ds(ck * X_HEADS + j, tm, stride=pitch), :] = y[:, ck * LANES:(ck + 1) * LANES]


def _mem_proj(name, mem, gain, w, *, tm):
    b, m, dm = mem.shape
    n = b * m
    hd = dm // X_HEADS
    n_chunk = hd // LANES
    pitch = n_chunk * X_HEADS
    body = functools.partial(_mem_proj_body, tm=tm, hd=hd)
    g = gain.reshape(1, dm)
    rows, plain = pl.pallas_call(
        body,
        grid=(n // tm, X_HEADS),
        in_specs=[pl.BlockSpec((tm, dm), lambda i, j: (i, 0)), _whole(g), pl.BlockSpec((dm, hd), lambda i, j: (0, j))],
        out_specs=(pl.BlockSpec((tm * pitch, LANES), lambda i, j: (i, 0)),
                   pl.BlockSpec((tm, hd), lambda i, j: (i, j))),
        out_shape=(jax.ShapeDtypeStruct((n * pitch, LANES), F32), jax.ShapeDtypeStruct((n, dm), BF16)),
        scratch_shapes=[pltpu.VMEM((tm, dm), BF16)],
        compiler_params=_params("arbitrary", "arbitrary"),
        name=name,
    )(mem.reshape(n, dm), g, w)
    full = rows.reshape(b, m, n_chunk, X_HEADS, LANES).transpose(0, 1, 3, 2, 4).reshape(1, b, m, X_HEADS, hd)
    return full, plain.reshape(b, m, dm)


def _xattn_step_body(q_ref, k_ref, v_ref, o_ref, *, nb, seg, hd, n_mem):
    rows = SUBLANES
    per = rows // seg
    shift = seg.bit_length() - 1
    rid = lax.broadcasted_iota(jnp.int32, (rows, 1), 0)
    scale = hd ** -0.5
    n_chunk = hd // LANES
    pitch = n_chunk * X_HEADS

    def head_rows(ref, b, h):
        parts = [ref[pl.ds(b * n_mem * pitch + ck * X_HEADS + h, n_mem, stride=pitch), :] for ck in range(n_chunk)]
        return jnp.concatenate(parts, axis=1).astype(BF16)

    pairs = [(b, h) for b in range(nb) for h in range(X_HEADS)]
    scores = []
    for b, h in pairs:
        r0 = (b // per) * rows
        q = q_ref[r0:r0 + rows, h * hd:(h + 1) * hd].astype(BF16)
        scores.append(lax.dot_general(q, head_rows(k_ref, b, h), _CONTRACT_LAST, preferred_element_type=F32) * scale)
    probs = []
    for s in scores:
        e = jnp.exp(s - jnp.max(s, axis=-1, keepdims=True))
        probs.append((e / jnp.sum(e, axis=-1, keepdims=True)).astype(BF16))
    outs = {}
    for (b, h), p in zip(pairs, probs):
        outs[b, h] = _dot(p, head_rows(v_ref, b, h))
    for g in range(nb // per):
        for h in range(X_HEADS):
            o = outs[g * per, h]
            for e in range(1, per):
                o = jnp.where((rid >> shift) == e, outs[g * per + e, h], o)
            o_ref[g * rows:(g + 1) * rows, h * hd:(h + 1) * hd] = o


def _xattn_step(q, mem_k, mem_v, *, seg, nb=4):
    n, dm = q.shape
    bsz, m, _, hd = mem_k.shape
    body = functools.partial(_xattn_step_body, nb=nb, seg=seg, hd=hd, n_mem=m)
    rows_per_seq = m * dm // LANES
    mspec = pl.BlockSpec((nb * rows_per_seq, LANES), lambda i: (i, 0))
    return pl.pallas_call(
        body,
        grid=(bsz // nb,),
        in_specs=[pl.BlockSpec((nb * seg, dm), lambda i: (i, 0)), mspec, mspec],
        out_specs=pl.BlockSpec((nb * seg, dm), lambda i: (i, 0)),
        out_shape=jax.ShapeDtypeStruct((n, dm), F32),
        compiler_params=_params("arbitrary"),
        name="xattn_step",
    )(q, _cache_rows(mem_k), _cache_rows(mem_v))


def _ffn_finish(conv, gate, wd_ref, o_ref, gfin_ref, j, nj, tm):
    act = (_gelu(conv) * gate).astype(BF16)
    o_ref[...] += _dot(act, wd_ref[...].astype(BF16))

    @pl.when(j == nj - 1)
    def _():
        def f(r0):
            o_ref[pl.ds(r0, 128), :] = _rms(o_ref[pl.ds(r0, 128), :], gfin_ref[...])
        _for_row_chunks(tm, 128, f)


def _ffn_seq_body(x_ref, gain_ref, wa_ref, wg_ref, wd_ref, cw_ref, cb_ref, gfin_ref, o_ref, tail_ref, h_ref, a_ref,
                  *, tm, halo, tiles_per_seq, nj):
    i = pl.program_id(0)
    j = pl.program_id(1)

    @pl.when(j == 0)
    def _():
        @pl.when(i % tiles_per_seq == 0)
        def _():
            h_ref[0:halo, :] = jnp.zeros((halo, h_ref.shape[1]), BF16)

        @pl.when(i % tiles_per_seq != 0)
        def _():
            h_ref[0:halo, :] = h_ref[tm:tm + halo, :]

        _rms_rows_to_bf16(x_ref, gain_ref, h_ref, tm, row0=halo)
        o_ref[...] = x_ref[...]

    a_ref[...] = _dot(h_ref[...], wa_ref[...].astype(BF16))
    gate = _dot(h_ref[halo:halo + tm, :], wg_ref[...].astype(BF16))
    conv = cb_ref[...]
    for tap in range(CONV_W):
        conv = conv + a_ref[pl.ds(halo - (CONV_W - 1) + tap, tm), :] * cw_ref[tap:tap + 1, :]
    tail_ref[...] = a_ref[pl.ds(halo + tm - SUBLANES, SUBLANES), :]
    _ffn_finish(conv, gate, wd_ref, o_ref, gfin_ref, j, nj, tm)


def _ffn_seq(x, gain, w_a, w_g, conv_w, conv_b, w_down, gain_final, *, tm, seq, tf=512, halo=16):
    n, dm = x.shape
    dff = w_down.shape[0]
    nj = dff // tf
    body = functools.partial(_ffn_seq_body, tm=tm, halo=halo, tiles_per_seq=seq // tm, nj=nj)
    return pl.pallas_call(
        body,
        grid=(n // tm, nj),
        in_specs=[
            pl.BlockSpec((tm, dm), lambda i, j: (i, 0)),
            pl.BlockSpec((1, dm), lambda i, j: (0, 0)),
            pl.BlockSpec((dm, tf), lambda i, j: (0, j)),
            pl.BlockSpec((dm, tf), lambda i, j: (0, j)),
            pl.BlockSpec((tf, dm), lambda i, j: (j, 0)),
            pl.BlockSpec((CONV_W, tf), lambda i, j: (0, j)),
            pl.BlockSpec((1, tf), lambda i, j: (0, j)),
            pl.BlockSpec((1, dm), lambda i, j: (0, 0)),
        ],
        out_specs=(pl.BlockSpec((tm, dm), lambda i, j: (i, 0)),
                   pl.BlockSpec((None, SUBLANES, tf), lambda i, j: (i, 0, j))),
        out_shape=(jax.ShapeDtypeStruct((n, dm), F32),
                   jax.ShapeDtypeStruct((n // tm, SUBLANES, dff), F32)),
        scratch_shapes=[pltpu.VMEM((halo + tm, dm), BF16), pltpu.VMEM((halo + tm, tf), F32)],
        compiler_params=_params("arbitrary", "arbitrary"),
        name="ffn_seq",
    )(x, gain.reshape(1, dm), w_a, w_g, w_down, conv_w, conv_b.reshape(1, dff), gain_final.reshape(1, dm))


def _dot_exact_rows(sel_b, x):
    hi, mid, lo = _split3_bf16(x)
    return _dot(sel_b, hi) + _dot(sel_b, mid) + _dot(sel_b, lo)


def _ffn_step_body(x_ref, gain_ref, wa_ref, wg_ref, wd_ref, cw_ref, cb_ref, gfin_ref, b0_ref, b1_ref,
                   o_ref, a_new_ref, wa_b_ref, wg_b_ref, wd_b_ref, h_ref, *, tm, seg, nj):
    j = pl.program_id(1)
    nseq = tm // seg
    shift = seg.bit_length() - 1

    @pl.when(j == 0)
    def _():
        _rms_rows_to_bf16(x_ref, gain_ref, h_ref, tm)
        o_ref[...] = x_ref[...]

    wa_b_ref[...] = wa_ref[...].astype(BF16)
    wg_b_ref[...] = wg_ref[...].astype(BF16)
    wd_b_ref[...] = wd_ref[...].astype(BF16)
    a = _dot(h_ref[...], wa_b_ref[...])
    gate = _dot(h_ref[...], wg_b_ref[...])
    t_in = lax.broadcasted_iota(jnp.int32, (tm, 1), 0) & (seg - 1)
    prev1 = jnp.where(t_in >= 1, pltpu.roll(a, 1, 0), 0.0)
    prev2 = jnp.where(t_in >= 2, pltpu.roll(a, 2, 0), 0.0)
    conv = cb_ref[...] + prev2 * cw_ref[0:1, :]
    conv = conv + prev1 * cw_ref[1:2, :]
    conv = conv + a * cw_ref[2:3, :]
    b0 = b0_ref[...]
    b1 = b1_ref[...]
    carried = jnp.concatenate([b0 * cw_ref[0:1, :] + b1 * cw_ref[1:2, :], b1 * cw_ref[0:1, :]], axis=0)
    r = lax.broadcasted_iota(jnp.int32, (tm, 2 * nseq), 0)
    c = lax.broadcasted_iota(jnp.int32, (tm, 2 * nseq), 1)
    spread = ((r >> shift) == (c & (nseq - 1))) & ((r & (seg - 1)) == (c >> (nseq.bit_length() - 1)))
    conv = conv + _dot_exact_rows(jnp.where(spread, 1.0, 0.0).astype(BF16), carried)
    r = lax.broadcasted_iota(jnp.int32, (2 * nseq, tm), 0)
    c = lax.broadcasted_iota(jnp.int32, (2 * nseq, tm), 1)
    pick = ((c >> shift) == (r & (nseq - 1))) & ((c & (seg - 1)) == seg - 2 + (r >> (nseq.bit_length() - 1)))
    picked = _dot_exact_rows(jnp.where(pick, 1.0, 0.0).astype(BF16), a)
    a_new_ref[0] = picked[0:nseq]
    a_new_ref[1] = picked[nseq:2 * nseq]
    _ffn_finish(conv, gate, wd_b_ref, o_ref, gfin_ref, j, nj, tm)


def _ffn_step(x, gain, w_up, conv_w, conv_b, w_down, gain_final, b0, b1, *, seg, tf=512):
    n, dm = x.shape
    dff = w_down.shape[0]
    nj = dff // tf
    tm = n
    nseq = n // seg
    body = functools.partial(_ffn_step_body, tm=tm, seg=seg, nj=nj)
    return pl.pallas_call(
        body,
        grid=(1, nj),
        in_specs=[
            pl.BlockSpec((tm, dm), lambda i, j: (i, 0)),
            pl.BlockSpec((1, dm), lambda i, j: (0, 0)),
            pl.BlockSpec((dm, tf), lambda i, j: (0, j)),
            pl.BlockSpec((dm, tf), lambda i, j: (0, nj + j)),
            pl.BlockSpec((tf, dm), lambda i, j: (j, 0)),
            pl.BlockSpec((CONV_W, tf), lambda i, j: (0, j)),
            pl.BlockSpec((1, tf), lambda i, j: (0, j)),
            pl.BlockSpec((1, dm), lambda i, j: (0, 0)),
            pl.BlockSpec((nseq, tf), lambda i, j: (0, j)),
            pl.BlockSpec((nseq, tf), lambda i, j: (0, j)),
        ],
        out_specs=(pl.BlockSpec((tm, dm), lambda i, j: (i, 0)),
                   pl.BlockSpec((CONV_W - 1, nseq, tf), lambda i, j: (0, 0, j)),
                   pl.BlockSpec((dm, tf), lambda i, j: (0, j)),
                   pl.BlockSpec((dm, tf), lambda i, j: (0, j)),
                   pl.BlockSpec((tf, dm), lambda i, j: (j, 0))),
        out_shape=(jax.ShapeDtypeStruct((n, dm), F32), jax.ShapeDtypeStruct((CONV_W - 1, nseq, dff), F32),
                   jax.ShapeDtypeStruct((dm, dff), BF16), jax.ShapeDtypeStruct((dm, dff), BF16),
                   jax.ShapeDtypeStruct((dff, dm), BF16)),
        scratch_shapes=[pltpu.VMEM((tm, dm), BF16)],
        compiler_params=_params("arbitrary", "arbitrary"),
        name="ffn_step",
    )(x, gain.reshape(1, dm), w_up, w_up, w_down, conv_w, conv_b.reshape(1, dff), gain_final.reshape(1, dm),
      b0, b1)


def kernel(x_prompt, x_sample, mem_prompt, cache_mem_k, cache_mem_v, state_s5_re, state_s5_im, state_gla, state_conv, norm_mix, w_in, s5_lambda_re, s5_lambda_im, s5_log_dt, s5_b_re, s5_b_im, s5_c_re, s5_c_im, s5_d, s5_w_glu, s5_b_glu, norm_s5_out, gla_w_g2, gla_b_g, norm_gla_out, w_out, norm_xattn, norm_mem, w_xq, w_xk, w_xv, w_xo, norm_ffn, w_up, conv_w, conv_b, w_down, norm_final):
    depth = w_in.shape[0]
    assert depth == 1
    bp, seq, dm = x_prompt.shape
    bs, seg, _ = x_sample.shape
    n_mem = mem_prompt.shape[1]
    dff = w_down.shape[1]
    l = 0
    s5_args = (s5_lambda_re[l], s5_lambda_im[l], s5_log_dt[l], s5_b_re[l], s5_b_im[l], s5_c_re[l], s5_c_im[l],
               s5_d[l], s5_w_glu[l], s5_b_glu[l])
    tm_p = min(1024, seq)
    n_s = bs * seg
    w_in = w_in.astype(BF16)
    s5_w, s5_powers = _s5_weights(*s5_args)
    g_total = s5_lambda_re.shape[1]
    n_state = s5_lambda_re.shape[2]

    xs = x_sample.reshape(n_s, dm)
    proj_s = _in_proj(xs, norm_mix[l], w_in[l], gla_w_g2[l], gla_b_g[l], tm=n_s)
    h0r = state_s5_re[l].reshape(1, bs, -1)
    h0i = state_s5_im[l].reshape(1, bs, -1)
    z_s5_s, hs_r, hs_i = _s5_mixer(proj_s[None], s5_w, _s5_scan_table(s5_powers, seg, 1), tt=n_s, steps=seg,
                                   h0=(h0r, h0i))
    y_gla_s, s_gla = _gla_step(proj_s, norm_gla_out[l], state_gla[l], seg=seg)
    x1s, w_out_b = _out_proj(z_s5_s[0], y_gla_s, norm_s5_out[l], w_out[l], xs, tm=n_s)
    qs, w_xq_b = _norm_linear("xattn_q_s", x1s, norm_xattn[l], w_xq[l], tm=n_s)
    att = _xattn_step(qs, cache_mem_k[l], cache_mem_v[l], seg=seg)
    x2s, w_xo_b = _cast_linear("xattn_o_s", att, w_xo[l], x1s, tm=n_s)
    buf = state_conv[l]
    ys, a_new, w_a_b, w_g_b, w_down_b = _ffn_step(x2s, norm_ffn[l], w_up[l], conv_w[l], conv_b[l], w_down[l],
                                                  norm_final, buf[:, 0], buf[:, 1], seg=seg)
    s_re = hs_r.reshape(1, bs, g_total, n_state)
    s_im = hs_i.reshape(1, bs, g_total, n_state)
    s_conv = jnp.swapaxes(a_new, 0, 1)[None]

    mk, mk_b = _mem_proj("mem_k", mem_prompt, norm_mem[l], w_xk[l], tm=min(1024, bp * n_mem))
    mv, mv_b = _mem_proj("mem_v", mem_prompt, norm_mem[l], w_xv[l], tm=min(1024, bp * n_mem))

    xp = x_prompt.reshape(bp * seq, dm)
    proj = _in_proj(xp, norm_mix[l], w_in[l], gla_w_g2[l], gla_b_g[l], tm=tm_p)
    proj3 = proj.reshape(bp, seq, -1)
    steps_p = SUBLANES // bp
    z_s5, hend_r, hend_i = _s5_mixer(proj3, s5_w, _s5_scan_table(s5_powers, steps_p, bp),
                                     tt=min(1024 // bp, seq), steps=steps_p)
    y_gla, st_gla = _gla_seq(proj3, norm_gla_out[l], tt=min(512, seq))
    x2 = _mix_xattn_seq(z_s5.reshape(bp * seq, -1), y_gla.reshape(bp * seq, -1), xp, norm_s5_out[l], norm_xattn[l],
                        w_out_b, w_xq_b, w_xo_b, mk_b, mv_b, tm=tm_p, seq=seq)
    yp, tails = _ffn_seq(x2, norm_ffn[l], w_a_b, w_g_b, conv_w[l], conv_b[l], w_down_b, norm_final,
                         tm=tm_p, seq=seq)
    p_re = jnp.swapaxes(hend_r[:, SUBLANES - bp:], 0, 1).reshape(1, bp, g_total, n_state)
    p_im = jnp.swapaxes(hend_i[:, SUBLANES - bp:], 0, 1).reshape(1, bp, g_total, n_state)
    p_gla = jnp.swapaxes(st_gla, -1, -2)[None]
    tiles_per_seq = seq // tm_p
    p_conv = tails.reshape(bp, tiles_per_seq, SUBLANES, dff)[:, -1, SUBLANES - (CONV_W - 1):, :][None]

    return (yp.reshape(bp, seq, dm), ys.reshape(bs, seg, dm), mk, mv,
            p_re, p_im, p_gla, p_conv, s_re, s_im, s_gla[None], s_conv)
```

```python
import functools

import jax
import jax.numpy as jnp
from jax import lax
from jax.experimental import pallas as pl
from jax.experimental.pallas import tpu as pltpu

F32 = jnp.float32
BF16 = jnp.bfloat16

EPS = 1e-6
S5_GROUP = 16
S5_BLOCK_GROUPS = 16
GLA_HEADS = 4
GLA_RANK = 16
GLA_TAU = 16.0
GLA_CHUNK = 64
X_HEADS = 4
CONV_W = 3

SUBLANES = 8
LANES = 128
VMEM_LIMIT_BYTES = 56 * 1024 * 1024

_CONTRACT_LAST = (((1,), (1,)), ((), ()))
_CONTRACT_FIRST = (((0,), (0,)), ((), ()))


def _params(*semantics):
    return pltpu.CompilerParams(dimension_semantics=semantics, vmem_limit_bytes=VMEM_LIMIT_BYTES)


def _dot(a, b):
    return jnp.dot(a, b, preferred_element_type=F32)


def _for_row_chunks(n_rows, chunk, fn):
    def body(i, carry):
        fn(pl.multiple_of(i * chunk, chunk))
        return carry
    lax.fori_loop(0, n_rows // chunk, body, 0)


def _rms(x, gain):
    r = lax.rsqrt(jnp.mean(x * x, axis=-1, keepdims=True) + EPS)
    return (x * r) * gain


def _gelu(x):
    return 0.5 * x * (1.0 + lax.erf(x * (2.0 ** -0.5)))


def _rms_rows_to_bf16(x_ref, gain_ref, h_ref, n_rows, row0=0, chunk=128):
    def f(r0):
        h_ref[pl.ds(row0 + r0, chunk), :] = _rms(x_ref[pl.ds(r0, chunk), :], gain_ref[...]).astype(BF16)
    _for_row_chunks(n_rows, chunk, f)


def _s5_prep_body(lr_ref, li_ref, ldt_ref, br_ref, bi_ref, apr_ref, api_ref, bbr_ref, bbi_ref):
    lr = lr_ref[...]
    li = li_ref[...]
    dt = jnp.exp(ldt_ref[...])
    mag = jnp.exp(lr * dt)
    ar = mag * jnp.cos(li * dt)
    ai = mag * jnp.sin(li * dt)
    den = lr * lr + li * li
    pr = ar - 1.0
    qr = (pr * lr + ai * li) / den
    qi = (ai * lr - pr * li) / den
    for p in range(S5_GROUP):
        br = br_ref[p]
        bi = bi_ref[p]
        bbr_ref[p] = qr * br - qi * bi
        bbi_ref[p] = qr * bi + qi * br
    cr, ci = ar, ai
    apr_ref[0] = cr
    api_ref[0] = ci
    for k in range(1, SUBLANES):
        cr, ci = cr * ar - ci * ai, cr * ai + ci * ar
        apr_ref[k] = cr
        api_ref[k] = ci


def _s5_prep(lam_re, lam_im, log_dt, b_re, b_im):
    g, n = lam_re.shape
    out = jax.ShapeDtypeStruct((SUBLANES, g, n), F32)
    outb = jax.ShapeDtypeStruct((S5_GROUP, g, n), F32)
    return pl.pallas_call(
        _s5_prep_body, out_shape=(out, out, outb, outb), name="s5_prep",
    )(lam_re, lam_im, log_dt.reshape(g, 1), jnp.transpose(b_re, (2, 0, 1)), jnp.transpose(b_im, (2, 0, 1)))


def _s5_weights(lam_re, lam_im, log_dt, b_re, b_im, c_re, c_im, d, w_glu, b_glu):
    g, n = lam_re.shape
    p = S5_GROUP
    gb = S5_BLOCK_GROUPS
    nb = g // gb
    apr, api, bbr, bbi = _s5_prep(lam_re, lam_im, log_dt, b_re, b_im)

    def same_group(rows, row_w, cols, col_w):
        return (jnp.arange(rows)[:, None] // row_w) == (jnp.arange(cols)[None, :] // col_w)

    def tile_cols(x, reps):
        w = x.shape[-1]
        sel = (jnp.arange(reps * w)[None, :] % w == jnp.arange(w)[:, None]).astype(BF16)
        return jnp.einsum('brw,wj->brj', x, sel, preferred_element_type=F32)

    def tile_rows(x, reps):
        h = x.shape[1]
        sel = (jnp.arange(reps * h)[:, None] % h == jnp.arange(h)[None, :]).astype(BF16)
        return jnp.einsum('rh,bhc->brc', sel, x, preferred_element_type=F32)

    def bu_block(bb):
        per_block = jnp.transpose(bb.reshape(p, nb, gb * n), (1, 0, 2)).astype(BF16)
        return jnp.where(same_group(gb * p, p, gb * n, n), tile_rows(per_block, gb), 0.0)

    def c_block(c):
        per_block = jnp.transpose(c.reshape(nb, gb, p, n), (0, 1, 3, 2)).reshape(nb, gb * n, p).astype(BF16)
        return jnp.where(same_group(gb * n, n, gb * p, p), tile_cols(per_block, gb), 0.0)

    w_bu = jnp.concatenate([bu_block(bbr), bu_block(bbi)], axis=-1).astype(BF16)
    w_c = jnp.concatenate([c_block(c_re), -c_block(c_im)], axis=1).astype(BF16)
    w_g = jnp.where(same_group(gb * p, p, gb * p, p),
                    tile_cols(w_glu.reshape(nb, gb * p, p).astype(BF16), gb), 0.0).astype(BF16)
    d_b = d.reshape(nb, 1, gb * p)
    bg_b = b_glu.reshape(nb, 1, gb * p)
    pw_r = apr.reshape(SUBLANES, nb, gb * n)
    pw_i = api.reshape(SUBLANES, nb, gb * n)
    return (w_bu, w_c, w_g, d_b, bg_b), (pw_r, pw_i)


def _s5_scan_table(powers, steps, rows_per_step):
    pw_r, pw_i = powers
    t = (jnp.arange(SUBLANES) // rows_per_step) % steps

    def masked(k):
        m = (t >= k)[:, None, None]
        return jnp.where(m, pw_r[k - 1][None], 0.0), jnp.where(m, pw_i[k - 1][None], 0.0)

    tabs = []
    k = 1
    while k < steps:
        tabs.extend(masked(k))
        k *= 2
    tabs.append(pw_r[t])
    tabs.append(pw_i[t])
    whole = jnp.full((SUBLANES,), steps - 1)
    tabs.append(pw_r[whole])
    tabs.append(pw_i[whole])
    return jnp.transpose(jnp.stack(tabs), (2, 0, 1, 3))


def _in_proj_body(x_ref, gain_ref, w_ref, wr_ref, wg_ref, wg2_ref, bg_ref, o_ref, h_ref, *, tm, n_main, n_r):
    j = pl.program_id(1)

    @pl.when(j == 0)
    def _():
        _rms_rows_to_bf16(x_ref, gain_ref, h_ref, tm)

    @pl.when(j < n_main)
    def _():
        o_ref[...] = _dot(h_ref[...], w_ref[...].astype(BF16))

    @pl.when((j >= n_main) & (j < n_main + n_r))
    def _():
        o_ref[...] = _dot(h_ref[...], wr_ref[...].astype(BF16))

    @pl.when(j == n_main + n_r)
    def _():
        g_low = _dot(h_ref[...], wg_ref[...].astype(BF16))
        pre = _dot(g_low.astype(BF16), wg2_ref[...].astype(BF16)) + bg_ref[...]
        log_sig = jnp.minimum(pre, 0.0) - jnp.log1p(jnp.exp(-jnp.abs(pre)))
        o_ref[...] = log_sig * (1.0 / GLA_TAU)


def _in_proj(x, gain, w_in, w_g2, b_g, *, tm, tn=512):
    n, dm = x.shape
    mix = dm
    n_u_q_k_v = mix // 2 + 2 * (mix // 4) + mix // 2
    col_g = n_u_q_k_v
    col_r = n_u_q_k_v + GLA_RANK
    w_r = w_in[:, col_r:]
    w_g = w_in[:, col_g:col_r]
    n_main = n_u_q_k_v // tn
    n_r = w_r.shape[1] // tn
    n_cols = n_u_q_k_v + w_r.shape[1] + w_g2.shape[1]
    body = functools.partial(_in_proj_body, tm=tm, n_main=n_main, n_r=n_r)
    return pl.pallas_call(
        body,
        grid=(n // tm, n_main + n_r + 1),
        in_specs=[
            pl.BlockSpec((tm, dm), lambda i, j: (i, 0)),
            pl.BlockSpec((1, dm), lambda i, j: (0, 0)),
            pl.BlockSpec((dm, tn), lambda i, j: (0, jnp.minimum(j, n_main - 1))),
            pl.BlockSpec((dm, tn), lambda i, j: (0, jnp.clip(j - n_main, 0, n_r - 1))),
            pl.BlockSpec((dm, GLA_RANK), lambda i, j: (0, 0)),
            pl.BlockSpec(w_g2.shape, lambda i, j: (0, 0)),
            pl.BlockSpec((1, w_g2.shape[1]), lambda i, j: (0, 0)),
        ],
        out_specs=pl.BlockSpec((tm, tn), lambda i, j: (i, j)),
        out_shape=jax.ShapeDtypeStruct((n, n_cols), F32),
        scratch_shapes=[pltpu.VMEM((tm, dm), BF16)],
        compiler_params=_params("arbitrary", "arbitrary"),
        name="in_proj",
    )(x, gain.reshape(1, dm), w_in, w_r, w_g, w_g2, b_g.reshape(1, -1))


def _s5_body(*refs, tt, steps, nseq, ns, chunk):
    carried = nseq > 1
    if carried:
        (u_ref, wbu_ref, wc_ref, wg_ref, d_ref, bg_ref, tab_ref, z_ref, hr_ref, hi_ref,
         bu_ref, cr_ref, ci_ref, us_ref, zs_ref) = refs
    else:
        u_ref, wbu_ref, wc_ref, wg_ref, d_ref, bg_ref, tab_ref, h0r_ref, h0i_ref, z_ref, hr_ref, hi_ref, bu_ref = refs
    seg = steps
    n_lvl = steps.bit_length() - 1
    lane_group = 2 * LANES
    rid = lax.broadcasted_iota(jnp.int32, (SUBLANES, 1), 0)
    cw = u_ref.shape[-1]
    n_lt = cw // LANES

    if carried:
        @pl.when(pl.program_id(1) == 0)
        def _():
            cr_ref[...] = jnp.zeros_like(cr_ref)
            ci_ref[...] = jnp.zeros_like(ci_ref)

        per = chunk // nseq

    def chunk_u(c):
        if not carried:
            return u_ref[c * chunk:(c + 1) * chunk, :]
        for b in range(nseq):
            for lt in range(n_lt):
                us_ref[c * n_lt + lt, pl.ds(b, per, stride=nseq), :] = (
                    u_ref[b, c * per:(c + 1) * per, lt * LANES:(lt + 1) * LANES])
        return jnp.concatenate([us_ref[c * n_lt + lt] for lt in range(n_lt)], axis=1)

    def project_in(c):
        rows = slice(c * chunk, (c + 1) * chunk)
        bu = _dot(chunk_u(c).astype(BF16), wbu_ref[...])
        if not carried:
            seqs = slice(c * chunk // seg, (c + 1) * chunk // seg)
            a_r = tab_ref[2 * n_lvl, 0:1, :]
            a_i = tab_ref[2 * n_lvl + 1, 0:1, :]
            h0r = h0r_ref[seqs, :]
            h0i = h0i_ref[seqs, :]
            first = (lax.broadcasted_iota(jnp.int32, (chunk, chunk // seg), 0)
                     == seg * lax.broadcasted_iota(jnp.int32, (chunk, chunk // seg), 1))
            first_b = jnp.where(first, 1.0, 0.0).astype(BF16)
            bu = bu + jnp.concatenate([_dot_exact_rows(first_b, a_r * h0r - a_i * h0i),
                                       _dot_exact_rows(first_b, a_r * h0i + a_i * h0r)], axis=1)
        bu_ref[rows, :] = bu

    def scan(c, carry):
        carry = list(carry)
        for blk in range(chunk // SUBLANES):
            rows = slice(c * chunk + blk * SUBLANES, c * chunk + (blk + 1) * SUBLANES)
            for gi in range(ns // lane_group):
                cols = slice(gi * lane_group, (gi + 1) * lane_group)
                cols_im = slice(ns + gi * lane_group, ns + (gi + 1) * lane_group)
                xr = bu_ref[rows, cols]
                xi = bu_ref[rows, cols_im]
                for lvl in range(n_lvl):
                    k = 1 << lvl
                    ar = tab_ref[2 * lvl, :, cols]
                    ai = tab_ref[2 * lvl + 1, :, cols]
                    sr = pltpu.roll(xr, k * nseq, 0)
                    si = pltpu.roll(xi, k * nseq, 0)
                    xr, xi = xr + (ar * sr - ai * si), xi + (ar * si + ai * sr)
                if carried:
                    pr = tab_ref[2 * n_lvl, :, cols]
                    pi = tab_ref[2 * n_lvl + 1, :, cols]
                    c_r, c_i = carry[gi]
                    hr = xr + (pr * c_r - pi * c_i)
                    hi = xi + (pr * c_i + pi * c_r)
                    lr, li = xr, xi
                    half = SUBLANES // 2
                    while half >= nseq:
                        keep = (rid & (2 * half - 1)) >= half
                        lr = jnp.where(keep, lr, pltpu.roll(lr, SUBLANES - half, 0))
                        li = jnp.where(keep, li, pltpu.roll(li, SUBLANES - half, 0))
                        half //= 2
                    er = tab_ref[2 * n_lvl + 2, :, cols]
                    ei = tab_ref[2 * n_lvl + 3, :, cols]
                    carry[gi] = (lr + (er * c_r - ei * c_i), li + (er * c_i + ei * c_r))
                else:
                    hr, hi = xr, xi
                    for e in range(SUBLANES // seg):
                        src = e * seg + seg - 1
                        dst = (c * chunk + blk * SUBLANES) // seg + e
                        hr_ref[dst:dst + 1, cols] = hr[src:src + 1]
                        hi_ref[dst:dst + 1, cols] = hi[src:src + 1]
                bu_ref[rows, cols] = hr
                bu_ref[rows, cols_im] = hi
        return carry

    def project_out(c):
        rows = slice(c * chunk, (c + 1) * chunk)
        if carried:
            u = jnp.concatenate([us_ref[c * n_lt + lt] for lt in range(n_lt)], axis=1)
        else:
            u = u_ref[rows, :]
        y = _dot(bu_ref[rows, :].astype(BF16), wc_ref[...]) + d_ref[...] * u
        z = _gelu(y)
        z = z * jax.nn.sigmoid(_dot(z.astype(BF16), wg_ref[...]) + bg_ref[...])
        if carried:
            for lt in range(n_lt):
                zs_ref[c * n_lt + lt] = z[:, lt * LANES:(lt + 1) * LANES]
            for b in range(nseq):
                for lt in range(n_lt):
                    z_ref[b, c * per:(c + 1) * per, lt * LANES:(lt + 1) * LANES] = (
                        zs_ref[c * n_lt + lt, pl.ds(b, per, stride=nseq), :])
        else:
            z_ref[rows, :] = z

    n_chunks = tt // chunk
    n_groups = ns // lane_group
    group_cols = [slice(gi * lane_group, (gi + 1) * lane_group) for gi in range(n_groups)]
    carry = [(cr_ref[:, cols], ci_ref[:, cols]) for cols in group_cols] if carried else [None] * n_groups
    project_in(0)
    for c in range(n_chunks):
        if c + 1 < n_chunks:
            project_in(c + 1)
        carry = scan(c, carry)
        project_out(c)
    if carried:
        for cols, (c_r, c_i) in zip(group_cols, carry):
            cr_ref[:, cols] = c_r
            ci_ref[:, cols] = c_i
            hr_ref[:, cols] = c_r
            hi_ref[:, cols] = c_i


def _s5_mixer(proj3, weights, tab, *, tt, steps, h0=None, chunk=128):
    w_bu, w_c, w_g, d_b, bg_b = weights
    bsz, seq, _ = proj3.shape
    nb, cw, ns2 = w_bu.shape
    ns = ns2 // 2
    n_tab = tab.shape[1]
    carried = h0 is None
    nseq = bsz if carried else 1
    assert nseq * steps == SUBLANES or not carried
    rows = nseq * tt
    body = functools.partial(_s5_body, tt=rows, steps=steps, nseq=nseq, ns=ns, chunk=min(chunk, rows))
    wspec = lambda shape: pl.BlockSpec((None,) + shape, lambda g, t: (g,) + (0,) * len(shape))
    lead = bsz if carried else None
    uz_spec = pl.BlockSpec((lead, tt, cw), lambda g, t: (0, t, g))
    in_specs = [
        uz_spec,
        wspec((cw, ns2)), wspec((ns2, cw)), wspec((cw, cw)), wspec((1, cw)), wspec((1, cw)),
        wspec((n_tab, SUBLANES, ns)),
    ]
    args = [proj3, w_bu, w_c, w_g, d_b, bg_b, tab]
    z_shape = jax.ShapeDtypeStruct((bsz, seq, nb * cw), F32)
    scratch = [pltpu.VMEM((rows, ns2), F32)]
    if carried:
        st_shape = jax.ShapeDtypeStruct((nb, SUBLANES, ns), F32)
        st_spec = pl.BlockSpec((None, SUBLANES, ns), lambda g, t: (g, 0, 0))
        slabs = (rows // min(chunk, rows)) * (cw // LANES)
        scratch += [pltpu.VMEM((SUBLANES, ns), F32), pltpu.VMEM((SUBLANES, ns), F32),
                    pltpu.VMEM((slabs, min(chunk, rows), LANES), F32), pltpu.VMEM((slabs, min(chunk, rows), LANES), F32)]
    else:
        assert bsz == 1
        h0r, h0i = h0
        st_spec = pl.BlockSpec((None, tt // steps, ns), lambda g, t: (0, t, g))
        in_specs += [st_spec] * 2
        args += [h0r, h0i]
        st_shape = jax.ShapeDtypeStruct((bsz, seq // steps, nb * ns), F32)
    return pl.pallas_call(
        body,
        grid=(nb, seq // tt),
        in_specs=in_specs,
        out_specs=(uz_spec, st_spec, st_spec),
        out_shape=(z_shape, st_shape, st_shape),
        scratch_shapes=scratch,
        compiler_params=_params("arbitrary", "arbitrary"),
        name="s5_carried" if carried else "s5_segments",
    )(*args)


def _split3_bf16(x):
    hi = x.astype(BF16)
    r1 = x - hi.astype(F32)
    mid = r1.astype(BF16)
    lo = (r1 - mid.astype(F32)).astype(BF16)
    return hi, mid, lo


def _gla_out(o, gain, r):
    return _rms(o, gain) * (r * jax.nn.sigmoid(r))


def _gla_seq_body(q_ref, k_ref, v_ref, la_ref, r_ref, gain_ref, y_ref, st_out_ref, st_ref, *, tt, dk, dv):
    c = GLA_CHUNK
    hh = GLA_HEADS

    @pl.when(pl.program_id(1) == 0)
    def _():
        st_ref[...] = jnp.zeros_like(st_ref)

    row = lax.broadcasted_iota(jnp.int32, (tt, tt), 0)
    col = lax.broadcasted_iota(jnp.int32, (tt, tt), 1)
    shift = c.bit_length() - 1
    tri = (col <= row) & ((row >> shift) == (col >> shift))
    tri_b = jnp.where(tri, 1.0, 0.0).astype(BF16)
    hi, mid, lo = _split3_bf16(la_ref[...])
    bcum = _dot(tri_b, hi) + _dot(tri_b, mid) + _dot(tri_b, lo)
    causal = lax.broadcasted_iota(jnp.int32, (c, c), 1) <= lax.broadcasted_iota(jnp.int32, (c, c), 0)
    scale = dk ** -0.5
    chunks = [slice(ci * c, (ci + 1) * c) for ci in range(tt // c)]
    work = []
    for h in range(hh):
        ks = slice(h * dk, (h + 1) * dk)
        vs = slice(h * dv, (h + 1) * dv)
        qts, decays, o_intra, kvs = [], [], [], []
        for sl in chunks:
            bc = bcum[sl, ks]
            bl = bc[c - 1:c]
            qt = ((q_ref[sl, ks] * scale) * jnp.exp(bc)).astype(BF16)
            kc = k_ref[sl, ks]
            kt = (kc * jnp.exp(-bc)).astype(BF16)
            ke = (kc * jnp.exp(bl - bc)).astype(BF16)
            vb = v_ref[sl, vs].astype(BF16)
            s = lax.dot_general(qt, kt, _CONTRACT_LAST, preferred_element_type=F32)
            s = jnp.where(causal, s, 0.0)
            qts.append(qt)
            decays.append(jnp.exp(bl))
            o_intra.append(_dot(s.astype(BF16), vb))
            kvs.append(lax.dot_general(vb, ke, _CONTRACT_FIRST, preferred_element_type=F32))
        work.append((vs, qts, decays, o_intra, kvs))
    for h, (vs, qts, decays, o_intra, kvs) in enumerate(work):
        st = st_ref[h]
        for sl, qt, dec, oi, kv in zip(chunks, qts, decays, o_intra, kvs):
            o = oi + lax.dot_general(qt, st.astype(BF16), _CONTRACT_LAST, preferred_element_type=F32)
            st = st * dec + kv
            y_ref[sl, vs] = _gla_out(o, gain_ref[...], r_ref[sl, vs]).astype(y_ref.dtype)
        st_ref[h] = st
        st_out_ref[h] = st


def _gla_seq(proj3, gain, *, tt):
    bsz, seq, _ = proj3.shape
    hh = GLA_HEADS
    dv = gain.shape[0]
    dk = dv // 2
    wk, wv = hh * dk, hh * dv
    col_q = 1024 // wk
    col_k = col_q + 1
    col_v = (1024 + 2 * wk) // wv
    col_r = col_v + 1
    col_la = (1024 + 2 * wk + 2 * wv) // wk
    spec = lambda w, c0: pl.BlockSpec((None, tt, w), lambda b, t: (b, t, c0))
    body = functools.partial(_gla_seq_body, tt=tt, dk=dk, dv=dv)
    return pl.pallas_call(
        body,
        grid=(bsz, seq // tt),
        in_specs=[spec(wk, col_q), spec(wk, col_k), spec(wv, col_v), spec(wk, col_la), spec(wv, col_r),
                  pl.BlockSpec((1, dv), lambda b, t: (0, 0))],
        out_specs=(pl.BlockSpec((None, tt, wv), lambda b, t: (b, t, 0)),
                   pl.BlockSpec((None, hh, dv, dk), lambda b, t: (b, 0, 0, 0))),
        out_shape=(jax.ShapeDtypeStruct((bsz, seq, wv), BF16),
                   jax.ShapeDtypeStruct((bsz, hh, dv, dk), F32)),
        scratch_shapes=[pltpu.VMEM((hh, dv, dk), F32)],
        compiler_params=_params("arbitrary", "arbitrary"),
        name="gla_seq",
    )(proj3, proj3, proj3, proj3, proj3, gain.reshape(1, dv))


def _gla_step_body(q_ref, k_ref, v_ref, la_ref, r_ref, gain_ref, s_ref, y_ref, so_ref, *, nb, seg, dk, dv):
    hh = GLA_HEADS
    rows = SUBLANES
    per = rows // seg
    row = lax.broadcasted_iota(jnp.int32, (rows, rows), 0)
    col = lax.broadcasted_iota(jnp.int32, (rows, rows), 1)
    shift = seg.bit_length() - 1
    causal = (col <= row) & ((row >> shift) == (col >> shift))
    rid = lax.broadcasted_iota(jnp.int32, (rows, 1), 0)
    t_in = rid & (seg - 1)
    ones_b = jnp.ones((rows, dv), BF16)
    scale = dk ** -0.5

    def pair(i, carry):
        r0 = pl.multiple_of(i * rows, rows)
        prep = []
        for h in range(hh):
            ks = slice(h * dk, (h + 1) * dk)
            vs = slice(h * dv, (h + 1) * dv)
            bc = la_ref[pl.ds(r0, rows), ks]
            k = 1
            while k < seg:
                bc = bc + jnp.where(t_in >= k, pltpu.roll(bc, k, 0), 0.0)
                k *= 2
            bl = jnp.zeros_like(bc)
            for e in range(per):
                last = e * seg + seg - 1
                bl = jnp.where((rid >> shift) == e, bc[last:last + 1], bl)
            qt = ((q_ref[pl.ds(r0, rows), ks] * scale) * jnp.exp(bc)).astype(BF16)
            kc = k_ref[pl.ds(r0, rows), ks]
            kt = (kc * jnp.exp(-bc)).astype(BF16)
            ke = kc * jnp.exp(bl - bc)
            vb = v_ref[pl.ds(r0, rows), vs].astype(BF16)
            per_seq = []
            for e in range(per):
                last = e * seg + seg - 1
                dec = jnp.exp(bc[last:last + 1])
                d_hi, d_mid, d_lo = _split3_bf16(dec)
                d3 = jnp.where(rid == 0, d_hi.astype(F32),
                               jnp.where(rid == 1, d_mid.astype(F32),
                                         jnp.where(rid == 2, d_lo.astype(F32), 0.0))).astype(BF16)
                ke_e = jnp.where((rid >> shift) == e, ke, 0.0).astype(BF16)
                per_seq.append((d3, ke_e, s_ref[i * per + e, h]))
            prep.append((vs, qt, kt, vb, per_seq))
        scores = [lax.dot_general(qt, kt, _CONTRACT_LAST, preferred_element_type=F32) for _, qt, kt, _, _ in prep]
        inter, dec_cols, kvs = [], [], []
        for _, qt, _, vb, per_seq in prep:
            for d3, ke_e, st in per_seq:
                inter.append(_dot(qt, st.astype(BF16)))
                dec_cols.append(lax.dot_general(d3, ones_b, _CONTRACT_FIRST, preferred_element_type=F32))
                kvs.append(lax.dot_general(ke_e, vb, _CONTRACT_FIRST, preferred_element_type=F32))
        intra = [_dot(jnp.where(causal, s, 0.0).astype(BF16), vb) for s, (_, _, _, vb, _) in zip(scores, prep)]
        for h, (vs, _, _, _, per_seq) in enumerate(prep):
            o_inter = inter[h * per]
            for e in range(per):
                if e:
                    o_inter = jnp.where((rid >> shift) == e, inter[h * per + e], o_inter)
                so_ref[i * per + e, h] = dec_cols[h * per + e] * per_seq[e][2] + kvs[h * per + e]
            y_ref[pl.ds(r0, rows), vs] = _gla_out(intra[h] + o_inter, gain_ref[...], r_ref[pl.ds(r0, rows), vs])
        return carry

    lax.fori_loop(0, nb // per, pair, 0)


def _gla_step(proj, gain, state, *, seg, nb=8):
    n = proj.shape[0]
    bsz, hh, dk, dv = state.shape
    rows = nb * seg
    c_q = 1024 // (hh * dk)
    c_k = c_q + 1
    c_v = (1024 + 2 * hh * dk) // (hh * dv)
    c_r = c_v + 1
    c_la = (1024 + 2 * hh * dk + 2 * hh * dv) // (hh * dk)
    spec = lambda w, c0: pl.BlockSpec((rows, w), lambda i: (i, c0))
    body = functools.partial(_gla_step_body, nb=nb, seg=seg, dk=dk, dv=dv)
    return pl.pallas_call(
        body,
        grid=(bsz // nb,),
        in_specs=[spec(hh * dk, c_q), spec(hh * dk, c_k), spec(hh * dv, c_v), spec(hh * dk, c_la), spec(hh * dv, c_r),
                  pl.BlockSpec((1, dv), lambda i: (0, 0)),
                  pl.BlockSpec((nb, hh, dk, dv), lambda i: (i, 0, 0, 0))],
        out_specs=(pl.BlockSpec((rows, hh * dv), lambda i: (i, 0)),
                   pl.BlockSpec((nb, hh, dk, dv), lambda i: (i, 0, 0, 0))),
        out_shape=(jax.ShapeDtypeStruct((n, hh * dv), F32), jax.ShapeDtypeStruct(state.shape, F32)),
        compiler_params=_params("arbitrary"),
        name="gla_step",
    )(proj, proj, proj, proj, proj, gain.reshape(1, dv), state)


def _tiled_linear(name, prologue, row_ins, aux_ins, aux_specs, w, *, tm, tn, residual=None):
    n = row_ins[0].shape[0]
    assert n == tm
    k_dim, n_cols = w.shape
    nr, na = len(row_ins), len(aux_ins)
    has_res = residual is not None

    def body(*refs):
        row_refs = refs[:nr]
        aux_refs = refs[nr:nr + na]
        w_ref = refs[nr + na]
        o_ref, wb_ref, h_ref = refs[-3], refs[-2], refs[-1]

        @pl.when(pl.program_id(1) == 0)
        def _():
            prologue(row_refs, aux_refs, h_ref)

        wb = w_ref[...].astype(BF16)
        wb_ref[...] = wb
        acc = _dot(h_ref[...], wb)
        if has_res:
            acc = acc + refs[nr + na + 1][...]
        o_ref[...] = acc

    in_specs = [pl.BlockSpec((tm, a.shape[1]), lambda i, j: (i, 0)) for a in row_ins]
    in_specs += list(aux_specs)
    in_specs.append(pl.BlockSpec((k_dim, tn), lambda i, j: (0, j)))
    args = list(row_ins) + list(aux_ins) + [w]
    if has_res:
        in_specs.append(pl.BlockSpec((tm, tn), lambda i, j: (i, j)))
        args.append(residual)
    return pl.pallas_call(
        body,
        grid=(n // tm, n_cols // tn),
        in_specs=in_specs,
        out_specs=(pl.BlockSpec((tm, tn), lambda i, j: (i, j)), pl.BlockSpec((k_dim, tn), lambda i, j: (0, j))),
        out_shape=(jax.ShapeDtypeStruct((n, n_cols), F32), jax.ShapeDtypeStruct((k_dim, n_cols), BF16)),
        scratch_shapes=[pltpu.VMEM((tm, k_dim), BF16)],
        compiler_params=_params("arbitrary", "arbitrary"),
        name=name,
    )(*args)


def _whole(a):
    return pl.BlockSpec(a.shape, lambda i, j: (0,) * a.ndim)


def _norm_linear(name, x, gain, w, *, tm, tn=512):
    def prologue(row_refs, aux_refs, h_ref):
        _rms_rows_to_bf16(row_refs[0], aux_refs[0], h_ref, tm)
    g = gain.reshape(1, -1)
    return _tiled_linear(name, prologue, [x], [g], [_whole(g)], w, tm=tm, tn=tn)


def _cast_linear(name, x, w, residual, *, tm, tn=512):
    def prologue(row_refs, aux_refs, h_ref):
        def f(r0):
            h_ref[pl.ds(r0, 128), :] = row_refs[0][pl.ds(r0, 128), :].astype(BF16)
        _for_row_chunks(tm, 128, f)
    return _tiled_linear(name, prologue, [x], [], [], w, tm=tm, tn=tn, residual=residual)


def _out_proj(z_s5, y_gla, gain_s5, w_out, x, *, tm, tn=512):
    ws = z_s5.shape[1]

    def prologue(row_refs, aux_refs, h_ref):
        def f(r0):
            h_ref[pl.ds(r0, 128), 0:ws] = _rms(row_refs[0][pl.ds(r0, 128), :], aux_refs[0][...]).astype(BF16)
            h_ref[pl.ds(r0, 128), ws:] = row_refs[1][pl.ds(r0, 128), :].astype(BF16)
        _for_row_chunks(tm, 128, f)
    g = gain_s5.reshape(1, -1)
    return _tiled_linear("out_proj", prologue, [z_s5, y_gla], [g], [_whole(g)], w_out, tm=tm, tn=tn, residual=x)


def _mix_xattn_seq_body(z_ref, y_ref, xres_ref, gs5_ref, gx_ref, wout_ref, wq_ref, wo_ref, mk_ref, mv_ref,
                        o_ref, x1_ref, h_ref, q_ref, *, tm, tn, nt, hd, chunk):
    j = pl.program_id(1)
    ws = z_ref.shape[1]
    scale = hd ** -0.5

    @pl.when(j == 0)
    def _():
        def f(r0):
            h_ref[pl.ds(r0, 128), 0:ws] = _rms(z_ref[pl.ds(r0, 128), :], gs5_ref[...]).astype(BF16)
            h_ref[pl.ds(r0, 128), ws:] = y_ref[pl.ds(r0, 128), :].astype(BF16)
        _for_row_chunks(tm, 128, f)

    @pl.when(j == nt)
    def _():
        _rms_rows_to_bf16(x1_ref, gx_ref, h_ref, tm)

    @pl.when(j == 2 * nt)
    def _():
        heads = [slice(h * hd, (h + 1) * hd) for h in range(X_HEADS)]

        def f(r0):
            scores = [lax.dot_general(q_ref[pl.ds(r0, chunk), hs], mk_ref[:, hs], _CONTRACT_LAST,
                                      preferred_element_type=F32) * scale for hs in heads]
            probs = []
            for s in scores:
                e = jnp.exp(s - jnp.max(s, axis=-1, keepdims=True))
                probs.append((e / jnp.sum(e, axis=-1, keepdims=True)).astype(BF16))
            for hs, p in zip(heads, probs):
                h_ref[pl.ds(r0, chunk), hs] = _dot(p, mv_ref[:, hs]).astype(BF16)
        _for_row_chunks(tm, chunk, f)

    for jj in range(nt):
        cols = slice(jj * tn, (jj + 1) * tn)

        @pl.when(j == jj)
        def _(cols=cols):
            x1_ref[:, cols] = _dot(h_ref[...], wout_ref[...]) + xres_ref[...]

        @pl.when(j == nt + jj)
        def _(cols=cols):
            q_ref[:, cols] = _dot(h_ref[...], wq_ref[...]).astype(BF16)

        @pl.when(j == 2 * nt + jj)
        def _(cols=cols):
            o_ref[...] = _dot(h_ref[...], wo_ref[...]) + x1_ref[:, cols]


def _mix_xattn_seq(z_s5, y_gla, x, gain_s5, gain_x, w_out, w_xq, w_xo, mk_b, mv_b, *, tm, seq, tn=512, chunk=512):
    n, dm = x.shape
    nt = dm // tn
    tiles_per_seq = seq // tm
    body = functools.partial(_mix_xattn_seq_body, tm=tm, tn=tn, nt=nt, hd=dm // X_HEADS, chunk=chunk)
    gs = gain_s5.reshape(1, -1)
    gx = gain_x.reshape(1, -1)
    phase = lambda p: (lambda i, j: (0, jnp.clip(j - p * nt, 0, nt - 1)))
    mspec = pl.BlockSpec((None,) + mk_b.shape[1:], lambda i, j: (i // tiles_per_seq, 0, 0))
    return pl.pallas_call(
        body,
        grid=(n // tm, 3 * nt),
        in_specs=[
            pl.BlockSpec((tm, z_s5.shape[1]), lambda i, j: (i, 0)),
            pl.BlockSpec((tm, y_gla.shape[1]), lambda i, j: (i, 0)),
            pl.BlockSpec((tm, tn), lambda i, j: (i, jnp.minimum(j, nt - 1))),
            _whole(gs), _whole(gx),
            pl.BlockSpec((dm, tn), phase(0)), pl.BlockSpec((dm, tn), phase(1)), pl.BlockSpec((dm, tn), phase(2)),
            mspec, mspec,
        ],
        out_specs=pl.BlockSpec((tm, tn), lambda i, j: (i, jnp.clip(j - 2 * nt, 0, nt - 1))),
        out_shape=jax.ShapeDtypeStruct((n, dm), F32),
        scratch_shapes=[pltpu.VMEM((tm, dm), F32), pltpu.VMEM((tm, dm), BF16), pltpu.VMEM((tm, dm), BF16)],
        compiler_params=_params("arbitrary", "arbitrary"),
        name="mix_xattn_seq",
    )(z_s5, y_gla, x, gs, gx, w_out, w_xq, w_xo, mk_b, mv_b)


def _cache_rows(c):
    b, m, h, hd = c.shape
    n_chunk = hd // LANES
    return c.reshape(b, m, h, n_chunk, LANES).transpose(0, 1, 3, 2, 4).reshape(b * m * n_chunk * h, LANES)


def _mem_proj_body(x_ref, gain_ref, w_ref, o_ref, ob_ref, h_ref, *, tm, hd):
    j = pl.program_id(1)

    @pl.when(j == 0)
    def _():
        _rms_rows_to_bf16(x_ref, gain_ref, h_ref, tm)

    y = _dot(h_ref[...], w_ref[...].astype(BF16))
    ob_ref[...] = y.astype(BF16)
    n_chunk = hd // LANES
    pitch = n_chunk * X_HEADS
    for ck in range(n_chunk):
        o_ref[pl.ds(ck * X_HEADS + j, tm, stride=pitch), :] = y[:, ck * LANES:(ck + 1) * LANES]


def _mem_proj(name, mem, gain, w, *, tm):
    b, m, dm = mem.shape
    n = b * m
    hd = dm // X_HEADS
    n_chunk = hd // LANES
    pitch = n_chunk * X_HEADS
    body = functools.partial(_mem_proj_body, tm=tm, hd=hd)
    g = gain.reshape(1, dm)
    rows, plain = pl.pallas_call(
        body,
        grid=(n // tm, X_HEADS),
        in_specs=[pl.BlockSpec((tm, dm), lambda i, j: (i, 0)), _whole(g), pl.BlockSpec((dm, hd), lambda i, j: (0, j))],
        out_specs=(pl.BlockSpec((tm * pitch, LANES), lambda i, j: (i, 0)),
                   pl.BlockSpec((tm, hd), lambda i, j: (i, j))),
        out_shape=(jax.ShapeDtypeStruct((n * pitch, LANES), F32), jax.ShapeDtypeStruct((n, dm), BF16)),
        scratch_shapes=[pltpu.VMEM((tm, dm), BF16)],
        compiler_params=_params("arbitrary", "arbitrary"),
        name=name,
    )(mem.reshape(n, dm), g, w)
    full = rows.reshape(b, m, n_chunk, X_HEADS, LANES).transpose(0, 1, 3, 2, 4).reshape(1, b, m, X_HEADS, hd)
    return full, plain.reshape(b, m, dm)


def _xattn_step_body(q_ref, k_ref, v_ref, o_ref, *, nb, seg, hd, n_mem):
    rows = SUBLANES
    per = rows // seg
    shift = seg.bit_length() - 1
    rid = lax.broadcasted_iota(jnp.int32, (rows, 1), 0)
    scale = hd ** -0.5
    n_chunk = hd // LANES
    pitch = n_chunk * X_HEADS

    def head_rows(ref, b, h):
        parts = [ref[pl.ds(b * n_mem * pitch + ck * X_HEADS + h, n_mem, stride=pitch), :] for ck in range(n_chunk)]
        return jnp.concatenate(parts, axis=1).astype(BF16)

    pairs = [(b, h) for b in range(nb) for h in range(X_HEADS)]
    scores = []
    for b, h in pairs:
        r0 = (b // per) * rows
        q = q_ref[r0:r0 + rows, h * hd:(h + 1) * hd].astype(BF16)
        scores.append(lax.dot_general(q, head_rows(k_ref, b, h), _CONTRACT_LAST, preferred_element_type=F32) * scale)
    probs = []
    for s in scores:
        e = jnp.exp(s - jnp.max(s, axis=-1, keepdims=True))
        probs.append((e / jnp.sum(e, axis=-1, keepdims=True)).astype(BF16))
    outs = {}
    for (b, h), p in zip(pairs, probs):
        outs[b, h] = _dot(p, head_rows(v_ref, b, h))
    for g in range(nb // per):
        for h in range(X_HEADS):
            o = outs[g * per, h]
            for e in range(1, per):
                o = jnp.where((rid >> shift) == e, outs[g * per + e, h], o)
            o_ref[g * rows:(g + 1) * rows, h * hd:(h + 1) * hd] = o


def _xattn_step(q, mem_k, mem_v, *, seg, nb=4):
    n, dm = q.shape
    bsz, m, _, hd = mem_k.shape
    body = functools.partial(_xattn_step_body, nb=nb, seg=seg, hd=hd, n_mem=m)
    rows_per_seq = m * dm // LANES
    mspec = pl.BlockSpec((nb * rows_per_seq, LANES), lambda i: (i, 0))
    return pl.pallas_call(
        body,
        grid=(bsz // nb,),
        in_specs=[pl.BlockSpec((nb * seg, dm), lambda i: (i, 0)), mspec, mspec],
        out_specs=pl.BlockSpec((nb * seg, dm), lambda i: (i, 0)),
        out_shape=jax.ShapeDtypeStruct((n, dm), F32),
        compiler_params=_params("arbitrary"),
        name="xattn_step",
    )(q, _cache_rows(mem_k), _cache_rows(mem_v))


def _ffn_finish(conv, gate, wd_ref, o_ref, gfin_ref, j, nj, tm):
    act = (_gelu(conv) * gate).astype(BF16)
    o_ref[...] += _dot(act, wd_ref[...].astype(BF16))

    @pl.when(j == nj - 1)
    def _():
        def f(r0):
            o_ref[pl.ds(r0, 128), :] = _rms(o_ref[pl.ds(r0, 128), :], gfin_ref[...])
        _for_row_chunks(tm, 128, f)


def _ffn_seq_body(x_ref, gain_ref, wa_ref, wg_ref, wd_ref, cw_ref, cb_ref, gfin_ref, o_ref, tail_ref, h_ref, a_ref,
                  *, tm, halo, tiles_per_seq, nj):
    i = pl.program_id(0)
    j = pl.program_id(1)

    @pl.when(j == 0)
    def _():
        @pl.when(i % tiles_per_seq == 0)
        def _():
            h_ref[0:halo, :] = jnp.zeros((halo, h_ref.shape[1]), BF16)

        @pl.when(i % tiles_per_seq != 0)
        def _():
            h_ref[0:halo, :] = h_ref[tm:tm + halo, :]

        _rms_rows_to_bf16(x_ref, gain_ref, h_ref, tm, row0=halo)
        o_ref[...] = x_ref[...]

    a_ref[...] = _dot(h_ref[...], wa_ref[...].astype(BF16))
    gate = _dot(h_ref[halo:halo + tm, :], wg_ref[...].astype(BF16))
    conv = cb_ref[...]
    for tap in range(CONV_W):
        conv = conv + a_ref[pl.ds(halo - (CONV_W - 1) + tap, tm), :] * cw_ref[tap:tap + 1, :]
    tail_ref[...] = a_ref[pl.ds(halo + tm - SUBLANES, SUBLANES), :]
    _ffn_finish(conv, gate, wd_ref, o_ref, gfin_ref, j, nj, tm)


def _ffn_seq(x, gain, w_a, w_g, conv_w, conv_b, w_down, gain_final, *, tm, seq, tf=512, halo=16):
    n, dm = x.shape
    dff = w_down.shape[0]
    nj = dff // tf
    body = functools.partial(_ffn_seq_body, tm=tm, halo=halo, tiles_per_seq=seq // tm, nj=nj)
    return pl.pallas_call(
        body,
        grid=(n // tm, nj),
        in_specs=[
            pl.BlockSpec((tm, dm), lambda i, j: (i, 0)),
            pl.BlockSpec((1, dm), lambda i, j: (0, 0)),
            pl.BlockSpec((dm, tf), lambda i, j: (0, j)),
            pl.BlockSpec((dm, tf), lambda i, j: (0, j)),
            pl.BlockSpec((tf, dm), lambda i, j: (j, 0)),
            pl.BlockSpec((CONV_W, tf), lambda i, j: (0, j)),
            pl.BlockSpec((1, tf), lambda i, j: (0, j)),
            pl.BlockSpec((1, dm), lambda i, j: (0, 0)),
        ],
        out_specs=(pl.BlockSpec((tm, dm), lambda i, j: (i, 0)),
                   pl.BlockSpec((None, SUBLANES, tf), lambda i, j: (i, 0, j))),
        out_shape=(jax.ShapeDtypeStruct((n, dm), F32),
                   jax.ShapeDtypeStruct((n // tm, SUBLANES, dff), F32)),
        scratch_shapes=[pltpu.VMEM((halo + tm, dm), BF16), pltpu.VMEM((halo + tm, tf), F32)],
        compiler_params=_params("arbitrary", "arbitrary"),
        name="ffn_seq",
    )(x, gain.reshape(1, dm), w_a, w_g, w_down, conv_w, conv_b.reshape(1, dff), gain_final.reshape(1, dm))


def _dot_exact_rows(sel_b, x):
    hi, mid, lo = _split3_bf16(x)
    return _dot(sel_b, hi) + _dot(sel_b, mid) + _dot(sel_b, lo)


def _ffn_step_body(x_ref, gain_ref, wa_ref, wg_ref, wd_ref, cw_ref, cb_ref, gfin_ref, b0_ref, b1_ref,
                   o_ref, a_new_ref, wa_b_ref, wg_b_ref, wd_b_ref, h_ref, *, tm, seg, nj):
    j = pl.program_id(1)
    nseq = tm // seg
    shift = seg.bit_length() - 1

    @pl.when(j == 0)
    def _():
        _rms_rows_to_bf16(x_ref, gain_ref, h_ref, tm)
        o_ref[...] = x_ref[...]

    wa_b_ref[...] = wa_ref[...].astype(BF16)
    wg_b_ref[...] = wg_ref[...].astype(BF16)
    wd_b_ref[...] = wd_ref[...].astype(BF16)
    a = _dot(h_ref[...], wa_b_ref[...])
    gate = _dot(h_ref[...], wg_b_ref[...])
    t_in = lax.broadcasted_iota(jnp.int32, (tm, 1), 0) & (seg - 1)
    prev1 = jnp.where(t_in >= 1, pltpu.roll(a, 1, 0), 0.0)
    prev2 = jnp.where(t_in >= 2, pltpu.roll(a, 2, 0), 0.0)
    conv = cb_ref[...] + prev2 * cw_ref[0:1, :]
    conv = conv + prev1 * cw_ref[1:2, :]
    conv = conv + a * cw_ref[2:3, :]
    b0 = b0_ref[...]
    b1 = b1_ref[...]
    carried = jnp.concatenate([b0 * cw_ref[0:1, :] + b1 * cw_ref[1:2, :], b1 * cw_ref[0:1, :]], axis=0)
    r = lax.broadcasted_iota(jnp.int32, (tm, 2 * nseq), 0)
    c = lax.broadcasted_iota(jnp.int32, (tm, 2 * nseq), 1)
    spread = ((r >> shift) == (c & (nseq - 1))) & ((r & (seg - 1)) == (c >> (nseq.bit_length() - 1)))
    conv = conv + _dot_exact_rows(jnp.where(spread, 1.0, 0.0).astype(BF16), carried)
    r = lax.broadcasted_iota(jnp.int32, (2 * nseq, tm), 0)
    c = lax.broadcasted_iota(jnp.int32, (2 * nseq, tm), 1)
    pick = ((c >> shift) == (r & (nseq - 1))) & ((c & (seg - 1)) == seg - 2 + (r >> (nseq.bit_length() - 1)))
    picked = _dot_exact_rows(jnp.where(pick, 1.0, 0.0).astype(BF16), a)
    a_new_ref[0] = picked[0:nseq]
    a_new_ref[1] = picked[nseq:2 * nseq]
    _ffn_finish(conv, gate, wd_b_ref, o_ref, gfin_ref, j, nj, tm)


def _ffn_step(x, gain, w_up, conv_w, conv_b, w_down, gain_final, b0, b1, *, seg, tf=512):
    n, dm = x.shape
    dff = w_down.shape[0]
    nj = dff // tf
    tm = n
    nseq = n // seg
    body = functools.partial(_ffn_step_body, tm=tm, seg=seg, nj=nj)
    return pl.pallas_call(
        body,
        grid=(1, nj),
        in_specs=[
            pl.BlockSpec((tm, dm), lambda i, j: (i, 0)),
            pl.BlockSpec((1, dm), lambda i, j: (0, 0)),
            pl.BlockSpec((dm, tf), lambda i, j: (0, j)),
            pl.BlockSpec((dm, tf), lambda i, j: (0, nj + j)),
            pl.BlockSpec((tf, dm), lambda i, j: (j, 0)),
            pl.BlockSpec((CONV_W, tf), lambda i, j: (0, j)),
            pl.BlockSpec((1, tf), lambda i, j: (0, j)),
            pl.BlockSpec((1, dm), lambda i, j: (0, 0)),
            pl.BlockSpec((nseq, tf), lambda i, j: (0, j)),
            pl.BlockSpec((nseq, tf), lambda i, j: (0, j)),
        ],
        out_specs=(pl.BlockSpec((tm, dm), lambda i, j: (i, 0)),
                   pl.BlockSpec((CONV_W - 1, nseq, tf), lambda i, j: (0, 0, j)),
                   pl.BlockSpec((dm, tf), lambda i, j: (0, j)),
                   pl.BlockSpec((dm, tf), lambda i, j: (0, j)),
                   pl.BlockSpec((tf, dm), lambda i, j: (j, 0))),
        out_shape=(jax.ShapeDtypeStruct((n, dm), F32), jax.ShapeDtypeStruct((CONV_W - 1, nseq, dff), F32),
                   jax.ShapeDtypeStruct((dm, dff), BF16), jax.ShapeDtypeStruct((dm, dff), BF16),
                   jax.ShapeDtypeStruct((dff, dm), BF16)),
        scratch_shapes=[pltpu.VMEM((tm, dm), BF16)],
        compiler_params=_params("arbitrary", "arbitrary"),
        name="ffn_step",
    )(x, gain.reshape(1, dm), w_up, w_up, w_down, conv_w, conv_b.reshape(1, dff), gain_final.reshape(1, dm),
      b0, b1)


def kernel(x_prompt, x_sample, mem_prompt, cache_mem_k, cache_mem_v, state_s5_re, state_s5_im, state_gla, state_conv, norm_mix, w_in, s5_lambda_re, s5_lambda_im, s5_log_dt, s5_b_re, s5_b_im, s5_c_re, s5_c_im, s5_d, s5_w_glu, s5_b_glu, norm_s5_out, gla_w_g2, gla_b_g, norm_gla_out, w_out, norm_xattn, norm_mem, w_xq, w_xk, w_xv, w_xo, norm_ffn, w_up, conv_w, conv_b, w_down, norm_final):
    depth = w_in.shape[0]
    assert depth == 1
    bp, seq, dm = x_prompt.shape
    bs, seg, _ = x_sample.shape
    n_mem = mem_prompt.shape[1]
    dff = w_down.shape[1]
    l = 0
    s5_args = (s5_lambda_re[l], s5_lambda_im[l], s5_log_dt[l], s5_b_re[l], s5_b_im[l], s5_c_re[l], s5_c_im[l],
               s5_d[l], s5_w_glu[l], s5_b_glu[l])
    tm_p = min(1024, seq)
    n_s = bs * seg
    w_in = w_in.astype(BF16)
    s5_w, s5_powers = _s5_weights(*s5_args)
    g_total = s5_lambda_re.shape[1]
    n_state = s5_lambda_re.shape[2]

    xs = x_sample.reshape(n_s, dm)
    proj_s = _in_proj(xs, norm_mix[l], w_in[l], gla_w_g2[l], gla_b_g[l], tm=n_s)
    h0r = state_s5_re[l].reshape(1, bs, -1)
    h0i = state_s5_im[l].reshape(1, bs, -1)
    z_s5_s, hs_r, hs_i = _s5_mixer(proj_s[None], s5_w, _s5_scan_table(s5_powers, seg, 1), tt=n_s, steps=seg,
                                   h0=(h0r, h0i))
    y_gla_s, s_gla = _gla_step(proj_s, norm_gla_out[l], state_gla[l], seg=seg)
    x1s, w_out_b = _out_proj(z_s5_s[0], y_gla_s, norm_s5_out[l], w_out[l], xs, tm=n_s)
    qs, w_xq_b = _norm_linear("xattn_q_s", x1s, norm_xattn[l], w_xq[l], tm=n_s)
    att = _xattn_step(qs, cache_mem_k[l], cache_mem_v[l], seg=seg)
    x2s, w_xo_b = _cast_linear("xattn_o_s", att, w_xo[l], x1s, tm=n_s)
    buf = state_conv[l]
    ys, a_new, w_a_b, w_g_b, w_down_b = _ffn_step(x2s, norm_ffn[l], w_up[l], conv_w[l], conv_b[l], w_down[l],
                                                  norm_final, buf[:, 0], buf[:, 1], seg=seg)
    s_re = hs_r.reshape(1, bs, g_total, n_state)
    s_im = hs_i.reshape(1, bs, g_total, n_state)
    s_conv = jnp.swapaxes(a_new, 0, 1)[None]

    mk, mk_b = _mem_proj("mem_k", mem_prompt, norm_mem[l], w_xk[l], tm=min(1024, bp * n_mem))
    mv, mv_b = _mem_proj("mem_v", mem_prompt, norm_mem[l], w_xv[l], tm=min(1024, bp * n_mem))

    xp = x_prompt.reshape(bp * seq, dm)
    proj = _in_proj(xp, norm_mix[l], w_in[l], gla_w_g2[l], gla_b_g[l], tm=tm_p)
    proj3 = proj.reshape(bp, seq, -1)
    steps_p = SUBLANES // bp
    z_s5, hend_r, hend_i = _s5_mixer(proj3, s5_w, _s5_scan_table(s5_powers, steps_p, bp),
                                     tt=min(1024 // bp, seq), steps=steps_p)
    y_gla, st_gla = _gla_seq(proj3, norm_gla_out[l], tt=min(512, seq))
    x2 = _mix_xattn_seq(z_s5.reshape(bp * seq, -1), y_gla.reshape(bp * seq, -1), xp, norm_s5_out[l], norm_xattn[l],
                        w_out_b, w_xq_b, w_xo_b, mk_b, mv_b, tm=tm_p, seq=seq)
    yp, tails = _ffn_seq(x2, norm_ffn[l], w_a_b, w_g_b, conv_w[l], conv_b[l], w_down_b, norm_final,
                         tm=tm_p, seq=seq)
    p_re = jnp.swapaxes(hend_r[:, SUBLANES - bp:], 0, 1).reshape(1, bp, g_total, n_state)
    p_im = jnp.swapaxes(hend_i[:, SUBLANES - bp:], 0, 1).reshape(1, bp, g_total, n_state)
    p_gla = jnp.swapaxes(st_gla, -1, -2)[None]
    tiles_per_seq = seq // tm_p
    p_conv = tails.reshape(bp, tiles_per_seq, SUBLANES, dff)[:, -1, SUBLANES - (CONV_W - 1):, :][None]

    return (yp.reshape(bp, seq, dm), ys.reshape(bs, seg, dm), mk, mv,
            p_re, p_im, p_gla, p_conv, s_re, s_im, s_gla[None], s_conv)
```
